```python
import math
import jax, jax.numpy as jnp
from jax import lax
import numpy as np

D_MODEL = 1024
BATCH = 8
SEQ = 2048
DEPTH = 2
DEC_BATCH = 128
DEC_SEQ = 4
PAST_LEN = 2048
PAGE_SIZE = 128

N_A_LAYERS = DEPTH // 2
N_B_LAYERS = DEPTH - N_A_LAYERS
GDN_DK = 128
GDN_DV = 128
GDN_HEADS = D_MODEL // GDN_DK
GDN_QK = GDN_HEADS * GDN_DK
GDN_VW = GDN_HEADS * GDN_DV
GDN_CONV_CH = 2 * GDN_QK + GDN_VW
GDN_IN = GDN_CONV_CH + GDN_VW + 2 * GDN_HEADS
CONV_W = 4
GDN_CHUNK = 64
DIFF_HEADS = 8
DIFF_DH = D_MODEL // (2 * DIFF_HEADS)
ROPE_THETA = 10000.0
Q_BLOCK = 128
N_GROUPS = 4
EXP_PER_GROUP = 8
N_EXPERTS = N_GROUPS * EXP_PER_GROUP
D_FF_EXPERT = 512
TOP_K = 2
MOE_BLOCK = 128
EPS = 1e-6

kernel_name = 'yoco_gdn_diffattn_hmoe_step'

F32 = jnp.float32


def rms_norm(x, g):
    xf = x.astype(F32)
    y = xf * lax.rsqrt(jnp.mean(xf * xf, axis=-1, keepdims=True) + EPS)
    return (y * g.astype(F32)).astype(x.dtype)


def rope(x, pos):
    half = x.shape[-1] // 2
    inv = ROPE_THETA ** (-jnp.arange(half, dtype=F32) / half)
    ang = pos.astype(F32)[:, None] * inv[None, :]
    cos = jnp.cos(ang)[:, None, None, :]
    sin = jnp.sin(ang)[:, None, None, :]
    xf = x.astype(F32)
    x1, x2 = xf[..., :half], xf[..., half:]
    return jnp.concatenate([x1 * cos - x2 * sin, x2 * cos + x1 * sin], axis=-1).astype(x.dtype)


def lambda_init(layer):
    return 0.8 - 0.6 * math.exp(-0.3 * layer)


def causal_conv_silu(u, buf, w):
    L = u.shape[1]
    full = jnp.concatenate([buf.astype(u.dtype), u], axis=1)
    acc = full[:, 0:L] * w[0]
    for i in range(1, CONV_W):
        acc = acc + full[:, i:i + L] * w[i]
    return jax.nn.silu(acc), full[:, -(CONV_W - 1):]


def gated_delta_chunked(q, k, v, beta, g, s0):
    B, L, H, DK = q.shape
    DV = v.shape[-1]
    C = min(GDN_CHUNK, L)
    n = -(-L // C)
    pad = n * C - L

    def blocks(t):
        t = jnp.pad(t, [(0, 0), (0, pad)] + [(0, 0)] * (t.ndim - 2))
        t = t.reshape((B, n, C) + t.shape[2:])
        return jnp.moveaxis(jnp.moveaxis(t, 1, 0), 2, 3)

    incl = jnp.tril(jnp.ones((C, C), bool))
    strict = jnp.tril(jnp.ones((C, C), bool), -1)
    eye = jnp.eye(C, dtype=F32)

    def step(S, inp):
        qc, kc, vc, bc, gc = inp
        G = jnp.cumsum(gc, axis=-1)
        decay = jnp.exp(jnp.where(incl, G[..., :, None] - G[..., None, :], -jnp.inf))
        kk = jnp.einsum('bhik,bhjk->bhij', kc, kc)
        lmat = jnp.where(strict, bc[..., :, None] * kk * decay, 0.0)
        rhs = jnp.concatenate([bc[..., None] * vc, (bc * jnp.exp(G))[..., None] * kc], axis=-1)
        sol = lax.linalg.triangular_solve(lmat + eye, rhs, left_side=True, lower=True, unit_diagonal=True)
        u = sol[..., :DV] - jnp.einsum('bhck,bhkv->bhcv', sol[..., DV:], S)
        qk = jnp.einsum('bhik,bhjk->bhij', qc, kc) * decay
        o = jnp.einsum('bhck,bhkv->bhcv', qc * jnp.exp(G)[..., None], S) + jnp.einsum('bhij,bhjv->bhiv', qk, u)
        g_last = G[..., -1:]
        S_new = jnp.exp(g_last)[..., None] * S + jnp.einsum('bhck,bhcv->bhkv', kc * jnp.exp(g_last - G)[..., None], u)
        return S_new, o

    s_fin, o = lax.scan(step, s0, (blocks(q), blocks(k), blocks(v), blocks(beta), blocks(g)))
    o = jnp.moveaxis(jnp.moveaxis(o, 3, 2), 0, 1).reshape(B, n * C, H, DV)[:, :L]
    return o, s_fin


def gdn_mixer(xn, conv_buf, s0, w_in, conv_w, a_log, dt_bias, onorm, w_out):
    B, L, _ = xn.shape
    proj = xn @ w_in
    qkv, new_buf = causal_conv_silu(proj[..., :GDN_CONV_CH], conv_buf, conv_w)
    z = proj[..., GDN_CONV_CH:GDN_CONV_CH + GDN_VW].reshape(B, L, GDN_HEADS, GDN_DV)
    b_raw = proj[..., GDN_CONV_CH + GDN_VW:GDN_CONV_CH + GDN_VW + GDN_HEADS].astype(F32)
    a_raw = proj[..., GDN_CONV_CH + GDN_VW + GDN_HEADS:].astype(F32)
    q = qkv[..., :GDN_QK].reshape(B, L, GDN_HEADS, GDN_DK).astype(F32)
    k = qkv[..., GDN_QK:2 * GDN_QK].reshape(B, L, GDN_HEADS, GDN_DK).astype(F32)
    v = qkv[..., 2 * GDN_QK:].reshape(B, L, GDN_HEADS, GDN_DV).astype(F32)
    q = q * lax.rsqrt(jnp.sum(q * q, axis=-1, keepdims=True) + EPS) * (GDN_DK ** -0.5)
    k = k * lax.rsqrt(jnp.sum(k * k, axis=-1, keepdims=True) + EPS)
    beta = jax.nn.sigmoid(b_raw)
    g = -jnp.exp(a_log.astype(F32)) * jax.nn.softplus(a_raw + dt_bias.astype(F32))
    o, s_new = gated_delta_chunked(q, k, v, beta, g, s0.astype(F32))
    o = rms_norm(o, onorm) * jax.nn.silu(z.astype(F32))
    return o.reshape(B, L, GDN_VW).astype(xn.dtype) @ w_out, new_buf, s_new


def shared_kv(h, pos, kv_norm, w_kv):
    B, L, _ = h.shape
    kv = rms_norm(h, kv_norm) @ w_kv
    k = rope(kv[..., :D_MODEL].reshape(B, L, DIFF_HEADS, 2, DIFF_DH), pos)
    k = k.reshape(B, L, DIFF_HEADS, 2 * DIFF_DH)
    v = kv[..., D_MODEL:].reshape(B, L, DIFF_HEADS, 2 * DIFF_DH)
    return k, v


def diff_attention(q, k, v, q_pos, k_pos, lam):
    B, Lq, H, _, d = q.shape
    qb = Q_BLOCK if Lq % Q_BLOCK == 0 else Lq
    nb = Lq // qb
    qs = q.reshape(B, nb, qb, H, 2, d).swapaxes(0, 1)
    ps = q_pos.reshape(nb, qb)

    def block(args):
        qblk, pblk = args
        s = jnp.einsum('bqhcd,bkhcd->bhcqk', qblk, k).astype(F32)
        mask = k_pos[None, :] <= pblk[:, None]
        p = jax.nn.softmax(jnp.where(mask, s, -jnp.inf), axis=-1)
        a = p[:, :, 0] - lam * p[:, :, 1]
        return jnp.einsum('bhqk,bkhe->bqhe', a.astype(v.dtype), v)

    o = lax.map(block, (qs, ps))
    return o.swapaxes(0, 1).reshape(B, Lq, H, 2 * d)


def diff_mixer(xn, k_rows, v_rows, q_pos, k_pos, w_q, lam_p, subln, w_out, lam_init):
    B, L, _ = xn.shape
    q = (xn @ w_q).reshape(B, L, DIFF_HEADS, 2, DIFF_DH)
    q = rope(q, q_pos) * (DIFF_DH ** -0.5)
    k = k_rows.reshape(k_rows.shape[0], k_rows.shape[1], DIFF_HEADS, 2, DIFF_DH)
    lp = lam_p.astype(F32)
    lam = jnp.exp(jnp.sum(lp[0] * lp[1])) - jnp.exp(jnp.sum(lp[2] * lp[3])) + lam_init
    o = diff_attention(q, k, v_rows, q_pos, k_pos, lam)
    o = rms_norm(o, subln) * (1.0 - lam_init)
    return o.reshape(B, L, D_MODEL).astype(xn.dtype) @ w_out


def routed_experts(xt, eidx, gate, w_gate, w_up, w_down):
    T, D = xt.shape
    A = T * TOP_K
    flat_e = eidx.reshape(-1)
    order = jnp.argsort(flat_e)
    sorted_e = flat_e[order]
    counts = jnp.bincount(flat_e, length=N_EXPERTS)
    padded = (counts + MOE_BLOCK - 1) // MOE_BLOCK * MOE_BLOCK
    pad_end = jnp.cumsum(padded)
    pad_start = pad_end - padded
    start = jnp.cumsum(counts) - counts
    dest = (pad_start[sorted_e] + jnp.arange(A) - start[sorted_e]).astype(jnp.int32)
    n_blocks = -(-A // MOE_BLOCK) + N_EXPERTS
    P = n_blocks * MOE_BLOCK
    row_tok = jnp.zeros((P,), jnp.int32).at[dest].set((order // TOP_K).astype(jnp.int32))
    block_e = jnp.minimum(jnp.searchsorted(pad_end, jnp.arange(n_blocks) * MOE_BLOCK, side='right'), N_EXPERTS - 1)
    xb = xt[row_tok].reshape(n_blocks, MOE_BLOCK, D)

    def expert_block(args):
        xblk, e = args
        hid = jax.nn.silu(xblk @ w_gate[e]) * (xblk @ w_up[e])
        return hid @ w_down[e]

    ys = lax.map(expert_block, (xb, block_e)).reshape(P, D)
    dest_orig = jnp.zeros((A,), jnp.int32).at[order].set(dest)
    return jnp.einsum('tkd,tk->td', ys[dest_orig].reshape(T, TOP_K, D), gate.astype(ys.dtype))


def hier_moe(xn, w_rg, b_rg, w_re, b_re, w_gate, w_up, w_down):
    B, L, D = xn.shape
    T = B * L
    xt = xn.reshape(T, D)
    p_g = jax.nn.softmax((xt @ w_rg).astype(F32) + b_rg.astype(F32), axis=-1)
    g_sel = jnp.argmax(p_g, axis=-1)
    pg_sel = jnp.max(p_g, axis=-1, keepdims=True)
    e_logits = ((xt @ w_re).astype(F32) + b_re.astype(F32)).reshape(T, N_GROUPS, EXP_PER_GROUP)
    e_in = e_logits[jnp.arange(T), g_sel]
    top_p, top_i = lax.top_k(jax.nn.softmax(e_in, axis=-1), TOP_K)
    gate = pg_sel * top_p / jnp.sum(top_p, axis=-1, keepdims=True)
    eidx = (g_sel[:, None] * EXP_PER_GROUP + top_i).astype(jnp.int32)
    return routed_experts(xt, eidx, gate, w_gate, w_up, w_down).reshape(B, L, D)


def setup_inputs(seed: int = 0) -> dict:
    key = jax.random.key(seed)
    ks = jax.random.split(key, 32)

    def nrm(i, shape, scale):
        return jax.random.normal(ks[i], shape, F32) * scale

    n_pages = PAST_LEN // PAGE_SIZE
    n_used = DEC_BATCH * n_pages
    n_phys = n_used + max(1, n_used // 4)
    page_table = jax.random.permutation(ks[0], n_phys)[:n_used].reshape(DEC_BATCH, n_pages).astype(jnp.int32)
    dt = jnp.exp(jax.random.uniform(ks[9], (N_A_LAYERS, GDN_HEADS), F32, math.log(1e-3), math.log(1e-1)))
    dt_bias = dt + jnp.log(-jnp.expm1(-dt))
    a_log = jnp.log(jax.random.uniform(ks[8], (N_A_LAYERS, GDN_HEADS), F32, 1.0, 16.0))
    return {
        'x_prompt': nrm(1, (BATCH, SEQ, D_MODEL), 1.0),
        'x_sample': nrm(2, (DEC_BATCH, DEC_SEQ, D_MODEL), 1.0),
        'state_delta': nrm(3, (N_A_LAYERS, DEC_BATCH, GDN_HEADS, GDN_DK, GDN_DV), GDN_DK ** -0.5),
        'state_conv': nrm(4, (N_A_LAYERS, DEC_BATCH, CONV_W - 1, GDN_CONV_CH), 1.0),
        'cache_k': nrm(5, (n_phys, PAGE_SIZE, DIFF_HEADS, 2 * DIFF_DH), 1.0),
        'cache_v': nrm(6, (n_phys, PAGE_SIZE, DIFF_HEADS, 2 * DIFF_DH), 1.0),
        'page_table': page_table,
        'norm_a': 1.0 + nrm(7, (N_A_LAYERS, D_MODEL), 0.02),
        'w_in_a': nrm(10, (N_A_LAYERS, D_MODEL, GDN_IN), D_MODEL ** -0.5),
        'conv_w_a': nrm(11, (N_A_LAYERS, CONV_W, GDN_CONV_CH), CONV_W ** -0.5),
        'a_log': a_log,
        'dt_bias': dt_bias,
        'onorm_a': 1.0 + nrm(12, (N_A_LAYERS, GDN_DV), 0.02),
        'w_out_a': nrm(13, (N_A_LAYERS, GDN_VW, D_MODEL), 0.5 * GDN_VW ** -0.5),
        'kv_norm': 1.0 + nrm(14, (D_MODEL,), 0.02),
        'w_kv': nrm(15, (D_MODEL, 2 * D_MODEL), D_MODEL ** -0.5),
        'norm_b': 1.0 + nrm(16, (N_B_LAYERS, D_MODEL), 0.02),
        'w_q_b': nrm(17, (N_B_LAYERS, D_MODEL, D_MODEL), D_MODEL ** -0.5),
        'lam_b': nrm(18, (N_B_LAYERS, 4, DIFF_DH), 0.1),
        'subln_b': 1.0 + nrm(19, (N_B_LAYERS, 2 * DIFF_DH), 0.02),
        'w_out_b': nrm(20, (N_B_LAYERS, D_MODEL, D_MODEL), 0.5 * D_MODEL ** -0.5),
        'norm_m': 1.0 + nrm(21, (DEPTH, D_MODEL), 0.02),
        'w_rg': nrm(22, (DEPTH, D_MODEL, N_GROUPS), D_MODEL ** -0.5),
        'b_rg': nrm(23, (DEPTH, N_GROUPS), 0.01),
        'w_re': nrm(24, (DEPTH, D_MODEL, N_EXPERTS), D_MODEL ** -0.5),
        'b_re': nrm(25, (DEPTH, N_EXPERTS), 0.01),
        'w_gate_e': nrm(26, (DEPTH, N_EXPERTS, D_MODEL, D_FF_EXPERT), D_MODEL ** -0.5),
        'w_up_e': nrm(27, (DEPTH, N_EXPERTS, D_MODEL, D_FF_EXPERT), D_MODEL ** -0.5),
        'w_down_e': nrm(28, (DEPTH, N_EXPERTS, D_FF_EXPERT, D_MODEL), 0.5 * D_FF_EXPERT ** -0.5),
        'final_norm': 1.0 + nrm(29, (D_MODEL,), 0.02),
    }


def reference(x_prompt, x_sample, state_delta, state_conv, cache_k, cache_v, page_table,
              norm_a, w_in_a, conv_w_a, a_log, dt_bias, onorm_a, w_out_a,
              kv_norm, w_kv, norm_b, w_q_b, lam_b, subln_b, w_out_b,
              norm_m, w_rg, b_rg, w_re, b_re, w_gate_e, w_up_e, w_down_e, final_norm):

    def trunk(x, conv0, s0, past_k, past_v, past_len):
        B, L, _ = x.shape
        pos = past_len + jnp.arange(L, dtype=jnp.int32)
        k_pos = jnp.arange(past_len + L, dtype=jnp.int32)
        h = x
        convs, states = [], []
        k_new = v_new = k_all = v_all = None
        for layer in range(DEPTH):
            if layer < N_A_LAYERS:
                i = layer
                o, cb, s = gdn_mixer(rms_norm(h, norm_a[i]), conv0[i], s0[i], w_in_a[i], conv_w_a[i],
                                     a_log[i], dt_bias[i], onorm_a[i], w_out_a[i])
                convs.append(cb)
                states.append(s)
            else:
                j = layer - N_A_LAYERS
                if j == 0:
                    k_new, v_new = shared_kv(h, pos, kv_norm, w_kv)
                    if past_k is None:
                        k_all, v_all = k_new, v_new
                    else:
                        k_all = jnp.concatenate([past_k.astype(k_new.dtype), k_new], axis=1)
                        v_all = jnp.concatenate([past_v.astype(v_new.dtype), v_new], axis=1)
                o = diff_mixer(rms_norm(h, norm_b[j]), k_all, v_all, pos, k_pos, w_q_b[j], lam_b[j],
                               subln_b[j], w_out_b[j], lambda_init(layer))
            h = h + o
            h = h + hier_moe(rms_norm(h, norm_m[layer]), w_rg[layer], b_rg[layer], w_re[layer], b_re[layer],
                             w_gate_e[layer], w_up_e[layer], w_down_e[layer])
        return rms_norm(h, final_norm), jnp.stack(states), jnp.stack(convs), k_new, v_new

    bp = x_prompt.shape[0]
    conv0_p = jnp.zeros((N_A_LAYERS, bp, CONV_W - 1, GDN_CONV_CH), x_prompt.dtype)
    s0_p = jnp.zeros((N_A_LAYERS, bp, GDN_HEADS, GDN_DK, GDN_DV), F32)
    y_prompt, sd_p, sc_p, k_p, v_p = trunk(x_prompt, conv0_p, s0_p, None, None, 0)

    db = x_sample.shape[0]
    n_pages = page_table.shape[1]
    past_len = n_pages * PAGE_SIZE
    past_k = cache_k[page_table].reshape(db, past_len, DIFF_HEADS, 2 * DIFF_DH)
    past_v = cache_v[page_table].reshape(db, past_len, DIFF_HEADS, 2 * DIFF_DH)
    y_sample, sd_s, sc_s, k_s, v_s = trunk(x_sample, state_conv, state_delta, past_k, past_v, past_len)

    return (y_prompt, y_sample, sd_p, sc_p, k_p, v_p, sd_s, sc_s, k_s, v_s)
```

```python
import functools
import math

import jax
import jax.numpy as jnp
from jax import lax
from jax.experimental import pallas as pl
from jax.experimental.pallas import tpu as pltpu

F32 = jnp.float32
BF16 = jnp.bfloat16
EPS = 1e-6
LANES = 128
SUBLANES = 8
VMEM_LIMIT = 56 * 1024 * 1024

CONV_W = 4
GDN_HEADS = 8
GDN_DK = 128
GDN_CHUNK = 64
DIFF_HEADS = 8
DIFF_DH = 64
ROPE_THETA = 10000.0
N_GROUPS = 4
EXP_PER_GROUP = 8
N_EXPERTS = N_GROUPS * EXP_PER_GROUP
TOP_K = 2
MOE_BLOCK = 128
ROUTE_OFF = N_GROUPS


def _params(*sem):
    return pltpu.CompilerParams(dimension_semantics=sem, vmem_limit_bytes=VMEM_LIMIT)


def _dot(a, b):
    return jnp.dot(a.astype(BF16), b.astype(BF16), preferred_element_type=F32)


def _dot_nt(a, b):
    return lax.dot_general(a.astype(BF16), b.astype(BF16), (((1,), (1,)), ((), ())),
                           preferred_element_type=F32)


def _dot_tn(a, b):
    return lax.dot_general(a.astype(BF16), b.astype(BF16), (((0,), (0,)), ((), ())),
                           preferred_element_type=F32)


def _split2(x):
    hi = x.astype(BF16)
    lo = (x - hi.astype(F32)).astype(BF16)
    return hi, lo


def _dot3(a, b):
    ah, al = _split2(a)
    bh, bl = _split2(b)
    d = functools.partial(jnp.dot, preferred_element_type=F32)
    return d(ah, bh) + (d(ah, bl) + d(al, bh))


def _dot_exact_lhs(a_bf, b):
    b1 = b.astype(BF16)
    r1 = b - b1.astype(F32)
    b2 = r1.astype(BF16)
    b3 = (r1 - b2.astype(F32)).astype(BF16)
    d = functools.partial(jnp.dot, preferred_element_type=F32)
    return d(a_bf, b1) + (d(a_bf, b2) + d(a_bf, b3))


def _sigmoid(x):
    return 1.0 / (1.0 + jnp.exp(-x))


def _silu(x):
    return x * _sigmoid(x)


def _softplus(x):
    return jnp.maximum(x, 0.0) + jnp.log1p(jnp.exp(-jnp.abs(x)))


def _rms(x, g):
    return x * lax.rsqrt(jnp.mean(x * x, axis=-1, keepdims=True) + EPS) * g


def _norm_mm_kernel(x_ref, g_ref, w_ref, o_ref):
    xn = _rms(x_ref[...], g_ref[...])
    o_ref[...] = jnp.dot(xn.astype(BF16), w_ref[...], preferred_element_type=F32)


def _norm_matmul(x, g, w_bf, tm):
    T, D = x.shape
    N = w_bf.shape[1]
    return pl.pallas_call(
        _norm_mm_kernel,
        grid=(T // tm,),
        in_specs=[pl.BlockSpec((tm, D), lambda i: (i, 0)),
                  pl.BlockSpec((1, D), lambda i: (0, 0)),
                  pl.BlockSpec((D, N), lambda i: (0, 0))],
        out_specs=pl.BlockSpec((tm, N), lambda i: (i, 0)),
        out_shape=jax.ShapeDtypeStruct((T, N), F32),
        compiler_params=_params("parallel"),
        name="norm_matmul",
    )(x, g.reshape(1, D), w_bf)


def _mm_res_kernel(x_ref, w_ref, h_ref, o_ref):
    o_ref[...] = h_ref[...] + jnp.dot(x_ref[...].astype(BF16), w_ref[...], preferred_element_type=F32)


def _matmul_residual(x, w_bf, h, tm):
    T, K = x.shape
    N = w_bf.shape[1]
    return pl.pallas_call(
        _mm_res_kernel,
        grid=(T // tm,),
        in_specs=[pl.BlockSpec((tm, K), lambda i: (i, 0)),
                  pl.BlockSpec((K, N), lambda i: (0, 0)),
                  pl.BlockSpec((tm, N), lambda i: (i, 0))],
        out_specs=pl.BlockSpec((tm, N), lambda i: (i, 0)),
        out_shape=jax.ShapeDtypeStruct((T, N), F32),
        compiler_params=_params("parallel"),
        name="matmul_residual",
    )(x, w_bf, h)


def _neumann_inverse(lmat, C):
    row = lax.broadcasted_iota(jnp.int32, (C, C), 0)
    col = lax.broadcasted_iota(jnp.int32, (C, C), 1)
    eye = jnp.where(row == col, 1.0, 0.0).astype(F32)
    p = -lmat
    t = eye + p
    for _ in range(int(math.log2(C)) - 1):
        p = _dot3(p, p)
        t = t + _dot3(t, p)
    return t


def _gdn_kernel(proj_ref, h_ref, conv0_ref, s0_ref, cw_ref, alog_ref, dtb_ref, onorm_ref, wout_ref,
                hout_ref, sout_ref, cout_ref, ext_ref, s_ref, *, C, nc, valid_last):
    H, DK = GDN_HEADS, GDN_DK
    QK = H * DK
    CH = 3 * QK
    c = pl.program_id(1)

    @pl.when(c == 0)
    def _():
        ext_ref[0:SUBLANES, :] = conv0_ref[0]
        s_ref[...] = s0_ref[0]

    xin = proj_ref[0, :, 0:CH]
    ext_ref[SUBLANES:SUBLANES + C, :] = xin
    base = SUBLANES - (CONV_W - 1)
    acc = ext_ref[base:base + C, :] * cw_ref[0:1, :]
    for i in range(1, CONV_W - 1):
        acc = acc + ext_ref[base + i:base + i + C, :] * cw_ref[i:i + 1, :]
    acc = acc + xin * cw_ref[CONV_W - 1:CONV_W, :]
    qkv = _silu(acc)

    @pl.when(c == nc - 1)
    def _():
        cout_ref[0] = ext_ref[valid_last:valid_last + SUBLANES, :]

    ext_ref[0:SUBLANES, :] = ext_ref[C:C + SUBLANES, :]

    ba = proj_ref[0, :, CH + QK:CH + QK + LANES]
    beta_t = _sigmoid(ba)
    g_t = -jnp.exp(alog_ref[...]) * _softplus(ba + dtb_ref[...])
    row1 = lax.broadcasted_iota(jnp.int32, (C, 1), 0)
    if valid_last < C:
        valid = jnp.logical_or(row1 < valid_last, c < nc - 1)
        beta_t = jnp.where(valid, beta_t, 0.0)
        g_t = jnp.where(valid, g_t, 0.0)

    row = lax.broadcasted_iota(jnp.int32, (C, C), 0)
    col = lax.broadcasted_iota(jnp.int32, (C, C), 1)
    incl = row >= col
    strict = row > col
    tril_bf = jnp.where(incl, 1.0, 0.0).astype(BF16)
    G = _dot_exact_lhs(tril_bf, g_t)
    GT = G.T
    expG = jnp.exp(G)
    g_last = G[C - 1:C, :]
    exp_last = jnp.exp(g_last)
    k_scale = jnp.exp(g_last - G)

    outs = []
    for hd in range(H):
        q = qkv[:, hd * DK:(hd + 1) * DK]
        k = qkv[:, QK + hd * DK:QK + (hd + 1) * DK]
        v = qkv[:, 2 * QK + hd * DK:2 * QK + (hd + 1) * DK]
        q = q * lax.rsqrt(jnp.sum(q * q, axis=-1, keepdims=True) + EPS) * (DK ** -0.5)
        k = k * lax.rsqrt(jnp.sum(k * k, axis=-1, keepdims=True) + EPS)
        if valid_last < C:
            k = jnp.where(valid, k, 0.0)
        bcol = beta_t[:, hd:hd + 1]
        gcol = G[:, H + hd:H + hd + 1]
        grow = GT[H + hd:H + hd + 1, :]
        eg = expG[:, H + hd:H + hd + 1]
        decay = jnp.exp(jnp.where(incl, gcol - grow, -jnp.inf))
        kk = _dot_nt(k, k)
        lmat = jnp.where(strict, bcol * kk * decay, 0.0)
        tinv = _neumann_inverse(lmat, C)
        rhs = jnp.concatenate([bcol * v, (bcol * eg) * k], axis=1)
        sol = _dot3(tinv, rhs)
        S = s_ref[hd]
        u = sol[:, :DK] - _dot(sol[:, DK:], S)
        qk = _dot_nt(q, k) * decay
        o = _dot(q * eg, S) + _dot(qk, u)
        s_ref[hd] = exp_last[:, H + hd:H + hd + 1] * S + _dot_tn(k * k_scale[:, H + hd:H + hd + 1], u)
        z = proj_ref[0, :, CH + hd * DK:CH + (hd + 1) * DK]
        outs.append(_rms(o, onorm_ref[...]) * _silu(z))

    o_all = jnp.concatenate(outs, axis=1)
    hout_ref[0] = h_ref[0] + jnp.dot(o_all.astype(BF16), wout_ref[...], preferred_element_type=F32)

    @pl.when(c == nc - 1)
    def _():
        sout_ref[0] = s_ref[...]


def _gdn_mixer(proj, h, conv0p, s0, conv_w, alog_l, dtb_l, onorm, wout_bf, C, valid_last):
    B, Lp, NP = proj.shape
    D = h.shape[-1]
    nc = Lp // C
    H, DK = GDN_HEADS, GDN_DK
    CH = 3 * H * DK
    kern = functools.partial(_gdn_kernel, C=C, nc=nc, valid_last=valid_last)
    return pl.pallas_call(
        kern,
        grid=(B, nc),
        in_specs=[pl.BlockSpec((1, C, NP), lambda b, c: (b, c, 0)),
                  pl.BlockSpec((1, C, D), lambda b, c: (b, c, 0)),
                  pl.BlockSpec((1, SUBLANES, CH), lambda b, c: (b, 0, 0)),
                  pl.BlockSpec((1, H, DK, DK), lambda b, c: (b, 0, 0, 0)),
                  pl.BlockSpec((CONV_W, CH), lambda b, c: (0, 0)),
                  pl.BlockSpec((1, LANES), lambda b, c: (0, 0)),
                  pl.BlockSpec((1, LANES), lambda b, c: (0, 0)),
                  pl.BlockSpec((1, DK), lambda b, c: (0, 0)),
                  pl.BlockSpec((D, D), lambda b, c: (0, 0))],
        out_specs=[pl.BlockSpec((1, C, D), lambda b, c: (b, c, 0)),
                   pl.BlockSpec((1, H, DK, DK), lambda b, c: (b, 0, 0, 0)),
                   pl.BlockSpec((1, SUBLANES, CH), lambda b, c: (b, 0, 0))],
        out_shape=[jax.ShapeDtypeStruct((B, Lp, D), F32),
                   jax.ShapeDtypeStruct((B, H, DK, DK), F32),
                   jax.ShapeDtypeStruct((B, SUBLANES, CH), F32)],
        scratch_shapes=[pltpu.VMEM((SUBLANES + C, CH), F32),
                        pltpu.VMEM((H, DK, DK), F32)],
        compiler_params=_params("parallel", "arbitrary"),
        name="gdn_mixer",
    )(proj, h, conv0p, s0, conv_w, alog_l, dtb_l, onorm.reshape(1, DK), wout_bf)


def _router_kernel(h_ref, g_ref, wr_ref, br_ref, xn_ref, gate_ref, eidx_ref):
    xn = _rms(h_ref[...], g_ref[...])
    xn_ref[...] = xn
    logits = _dot3(xn, wr_ref[...]) + br_ref[...]
    tm = logits.shape[0]
    lane = lax.broadcasted_iota(jnp.int32, (tm, LANES), 1).astype(F32)
    neg = -jnp.inf
    big = float(LANES)

    lg = jnp.where(lane < N_GROUPS, logits, neg)
    eg = jnp.exp(lg - jnp.max(lg, axis=-1, keepdims=True))
    pg = eg / jnp.sum(eg, axis=-1, keepdims=True)
    pmax = jnp.max(pg, axis=-1, keepdims=True)
    g_sel = jnp.min(jnp.where(pg == pmax, lane, big), axis=-1, keepdims=True)

    lo = ROUTE_OFF + EXP_PER_GROUP * g_sel
    in_grp = jnp.logical_and(lane >= lo, lane < lo + EXP_PER_GROUP)
    le = jnp.where(in_grp, logits, neg)
    ee = jnp.exp(le - jnp.max(le, axis=-1, keepdims=True))
    pe = ee / jnp.sum(ee, axis=-1, keepdims=True)
    pe = jnp.where(in_grp, pe, -1.0)
    p1 = jnp.max(pe, axis=-1, keepdims=True)
    i1 = jnp.min(jnp.where(pe == p1, lane, big), axis=-1, keepdims=True)
    pe2 = jnp.where(lane == i1, -1.0, pe)
    p2 = jnp.max(pe2, axis=-1, keepdims=True)
    i2 = jnp.min(jnp.where(pe2 == p2, lane, big), axis=-1, keepdims=True)
    den = p1 + p2
    gate_ref[...] = jnp.where(lane == 0, pmax * p1 / den, jnp.where(lane == 1, pmax * p2 / den, 0.0))
    eidx_ref[...] = jnp.where(lane == 0, i1 - ROUTE_OFF,
                              jnp.where(lane == 1, i2 - ROUTE_OFF, 0.0)).astype(jnp.int32)


def _router(h, g, wr, br, tm):
    T, D = h.shape
    return pl.pallas_call(
        _router_kernel,
        grid=(T // tm,),
        in_specs=[pl.BlockSpec((tm, D), lambda i: (i, 0)),
                  pl.BlockSpec((1, D), lambda i: (0, 0)),
                  pl.BlockSpec((D, LANES), lambda i: (0, 0)),
                  pl.BlockSpec((1, LANES), lambda i: (0, 0))],
        out_specs=[pl.BlockSpec((tm, D), lambda i: (i, 0)),
                   pl.BlockSpec((tm, LANES), lambda i: (i, 0)),
                   pl.BlockSpec((tm, LANES), lambda i: (i, 0))],
        out_shape=[jax.ShapeDtypeStruct((T, D), F32),
                   jax.ShapeDtypeStruct((T, LANES), F32),
                   jax.ShapeDtypeStruct((T, LANES), jnp.int32)],
        compiler_params=_params("parallel"),
        name="moe_router",
    )(h, g.reshape(1, D), wr, br)


def _row_copy(src_hbm, row, dst, slot_idx, sem):
    return pltpu.make_async_copy(src_hbm.at[pl.ds(row, 1)], dst.at[slot_idx], sem)


def _expert_kernel(be_ref, rt_ref, nu_ref, xn_hbm, wg_ref, wu_ref, wd_ref, ys_ref, xbuf, sem):
    del be_ref
    i = pl.program_id(0)
    n_used = nu_ref[0]

    def issue(blk, slot):
        def body(r, carry):
            tok = rt_ref[blk * MOE_BLOCK + r]
            _row_copy(xn_hbm, tok, xbuf, (slot, pl.ds(r, 1)), sem.at[slot]).start()
            return carry
        lax.fori_loop(0, MOE_BLOCK, body, 0)

    def wait(slot):
        def body(r, carry):
            _row_copy(xn_hbm, 0, xbuf, (slot, pl.ds(r, 1)), sem.at[slot]).wait()
            return carry
        lax.fori_loop(0, MOE_BLOCK, body, 0)

    @pl.when(i == 0)
    def _():
        issue(0, 0)

    @pl.when(i + 1 < n_used)
    def _():
        issue(i + 1, (i + 1) % 2)

    @pl.when(i < n_used)
    def _():
        slot = i % 2
        wait(slot)
        x = xbuf[slot].astype(BF16)
        gt = jnp.dot(x, wg_ref[0], preferred_element_type=F32)
        up = jnp.dot(x, wu_ref[0], preferred_element_type=F32)
        hid = _silu(gt) * up
        ys_ref[...] = jnp.dot(hid.astype(BF16), wd_ref[0], preferred_element_type=F32)

    @pl.when(i >= n_used)
    def _():
        ys_ref[...] = jnp.zeros(ys_ref.shape, F32)


def _experts(xn, block_e, row_tok, n_used, wg_bf, wu_bf, wd_bf):
    T, D = xn.shape
    FF = wg_bf.shape[-1]
    n_blocks = block_e.shape[0]
    P = n_blocks * MOE_BLOCK
    grid_spec = pltpu.PrefetchScalarGridSpec(
        num_scalar_prefetch=3,
        grid=(n_blocks,),
        in_specs=[pl.BlockSpec(memory_space=pl.ANY),
                  pl.BlockSpec((1, D, FF), lambda i, be, rt, nu: (be[i], 0, 0)),
                  pl.BlockSpec((1, D, FF), lambda i, be, rt, nu: (be[i], 0, 0)),
                  pl.BlockSpec((1, FF, D), lambda i, be, rt, nu: (be[i], 0, 0))],
        out_specs=pl.BlockSpec((MOE_BLOCK, D), lambda i, be, rt, nu: (i, 0)),
        scratch_shapes=[pltpu.VMEM((2, MOE_BLOCK, D), F32),
                        pltpu.SemaphoreType.DMA((2,))],
    )
    return pl.pallas_call(
        _expert_kernel,
        grid_spec=grid_spec,
        out_shape=jax.ShapeDtypeStruct((P, D), F32),
        compiler_params=_params("arbitrary"),
        name="moe_experts",
    )(block_e, row_tok, n_used, xn, wg_bf, wu_bf, wd_bf)


def _combine_kernel(dest_ref, ys_hbm, h_ref, gate_ref, fn_ref, o_ref, ybuf, sem, *, tm, final_norm):
    i = pl.program_id(0)
    n = pl.num_programs(0)

    def issue(step, slot):
        def body(r, carry):
            for kk in range(TOP_K):
                d = dest_ref[(step * tm + r) * TOP_K + kk]
                _row_copy(ys_hbm, d, ybuf, (slot, kk, pl.ds(r, 1)), sem.at[slot]).start()
            return carry
        lax.fori_loop(0, tm, body, 0)

    def wait(slot):
        def body(r, carry):
            for kk in range(TOP_K):
                _row_copy(ys_hbm, 0, ybuf, (slot, kk, pl.ds(r, 1)), sem.at[slot]).wait()
            return carry
        lax.fori_loop(0, tm, body, 0)

    @pl.when(i == 0)
    def _():
        issue(0, 0)

    @pl.when(i + 1 < n)
    def _():
        issue(i + 1, (i + 1) % 2)

    slot = i % 2
    wait(slot)
    gate = gate_ref[...]
    y = h_ref[...] + (ybuf[slot, 0] * gate[:, 0:1] + ybuf[slot, 1] * gate[:, 1:2])
    if final_norm:
        y = _rms(y, fn_ref[...])
    o_ref[...] = y


def _combine(ys, dest, h, gates, fnorm, tm, final_norm):
    T, D = h.shape
    kern = functools.partial(_combine_kernel, tm=tm, final_norm=final_norm)
    grid_spec = pltpu.PrefetchScalarGridSpec(
        num_scalar_prefetch=1,
        grid=(T // tm,),
        in_specs=[pl.BlockSpec(memory_space=pl.ANY),
                  pl.BlockSpec((tm, D), lambda i, d: (i, 0)),
                  pl.BlockSpec((tm, LANES), lambda i, d: (i, 0)),
                  pl.BlockSpec((1, D), lambda i, d: (0, 0))],
        out_specs=pl.BlockSpec((tm, D), lambda i, d: (i, 0)),
        scratch_shapes=[pltpu.VMEM((2, TOP_K, tm, D), F32),
                        pltpu.SemaphoreType.DMA((2,))],
    )
    return pl.pallas_call(
        kern,
        grid_spec=grid_spec,
        out_shape=jax.ShapeDtypeStruct((T, D), F32),
        compiler_params=_params("arbitrary"),
        name="moe_combine",
    )(dest, ys, h, gates, fnorm.reshape(1, D))


def _route_tables(eidx):
    T = eidx.shape[0]
    A = T * TOP_K
    flat_e = eidx.reshape(A)
    onehot = (flat_e[:, None] == jnp.arange(N_EXPERTS, dtype=jnp.int32)[None, :]).astype(jnp.int32)
    csum = jnp.cumsum(onehot, axis=0)
    rank = jnp.sum((csum - onehot) * onehot, axis=1)
    counts = csum[-1]
    padded = (counts + MOE_BLOCK - 1) // MOE_BLOCK * MOE_BLOCK
    pad_end = jnp.cumsum(padded)
    pad_start = pad_end - padded
    dest = (pad_start[flat_e] + rank).astype(jnp.int32)
    n_blocks = -(-A // MOE_BLOCK) + N_EXPERTS
    P = n_blocks * MOE_BLOCK
    row_tok = jnp.zeros((P,), jnp.int32).at[dest].set(jnp.arange(A, dtype=jnp.int32) // TOP_K)
    starts = jnp.arange(n_blocks, dtype=jnp.int32) * MOE_BLOCK
    block_e = jnp.minimum(jnp.searchsorted(pad_end, starts, side='right'), N_EXPERTS - 1).astype(jnp.int32)
    n_used = (pad_end[-1] // MOE_BLOCK).astype(jnp.int32).reshape(1)
    return dest, row_tok, block_e, n_used


def _hier_moe(h, norm_g, wr, br, wg_bf, wu_bf, wd_bf, fnorm, final_norm, tm):
    xn, gates, eidx = _router(h, norm_g, wr, br, tm)
    dest, row_tok, block_e, n_used = _route_tables(eidx[:, :TOP_K])
    ys = _experts(xn, block_e, row_tok, n_used, wg_bf, wu_bf, wd_bf)
    return _combine(ys, dest, h, gates, fnorm, min(tm, MOE_BLOCK), final_norm)


def _rope_tile(x, cos, sin_signed, lane):
    half = DIFF_DH // 2
    rot = jnp.where((lane & (DIFF_DH - 1)) < half,
                    pltpu.roll(x, LANES - half, axis=1),
                    pltpu.roll(x, half, axis=1))
    return x * cos + rot * sin_signed


def _qkv_kernel(h_ref, gq_ref, gkv_ref, wq_ref, wkv_ref, cos_ref, sin_ref, q_ref, k_ref, v_ref):
    x = h_ref[...]
    D = x.shape[1]
    xs = x * lax.rsqrt(jnp.mean(x * x, axis=-1, keepdims=True) + EPS)
    q = jnp.dot((xs * gq_ref[...]).astype(BF16), wq_ref[...], preferred_element_type=F32)
    kv = jnp.dot((xs * gkv_ref[...]).astype(BF16), wkv_ref[...], preferred_element_type=F32)
    cos = cos_ref[...]
    sin = sin_ref[...]
    lane = lax.broadcasted_iota(jnp.int32, cos.shape, 1)
    for hd in range(D // LANES):
        sl = slice(hd * LANES, (hd + 1) * LANES)
        q_ref[:, sl] = _rope_tile(q[:, sl], cos, sin, lane) * (DIFF_DH ** -0.5)
        k_ref[:, sl] = _rope_tile(kv[:, sl], cos, sin, lane)
    v_ref[...] = kv[:, D:]


def _qkv_proj(h, gq, gkv, wq_bf, wkv_bf, cos_t, sin_t, tm):
    T, D = h.shape
    nt = cos_t.shape[0] // tm
    row = pl.BlockSpec((tm, D), lambda i: (i, 0))
    vec = pl.BlockSpec((1, D), lambda i: (0, 0))
    tab = pl.BlockSpec((tm, LANES), lambda i: (i % nt, 0))
    return pl.pallas_call(
        _qkv_kernel,
        grid=(T // tm,),
        in_specs=[row, vec, vec,
                  pl.BlockSpec((D, D), lambda i: (0, 0)),
                  pl.BlockSpec((D, 2 * D), lambda i: (0, 0)),
                  tab, tab],
        out_specs=[row, row, row],
        out_shape=[jax.ShapeDtypeStruct((T, D), F32)] * 3,
        compiler_params=_params("parallel"),
        name="qkv_proj",
    )(h, gq.reshape(1, D), gkv.reshape(1, D), wq_bf, wkv_bf, cos_t, sin_t)


def _rope_tables(pos):
    half = DIFF_DH // 2
    inv = ROPE_THETA ** (-jnp.arange(half, dtype=F32) / half)
    ang = pos.astype(F32)[:, None] * inv[None, :]
    cos = jnp.cos(ang)
    sin = jnp.sin(ang)
    reps = LANES // DIFF_DH
    return (jnp.tile(jnp.concatenate([cos, cos], axis=1), (1, reps)),
            jnp.tile(jnp.concatenate([-sin, sin], axis=1), (1, reps)))


def _lambda(lam_ref, lam_init):
    lp = lam_ref[...]
    a = jnp.sum(lp[0:1, :] * lp[1:2, :], axis=-1, keepdims=True)
    b = jnp.sum(lp[2:3, :] * lp[3:4, :], axis=-1, keepdims=True)
    return jnp.exp(a) - jnp.exp(b) + lam_init


def _attn_prompt_kernel(lam_ref, q_ref, k_ref, v_ref, subln_ref, o_ref, m_ref, l_ref, acc_ref,
                        *, tq, tk, lam_init):
    qi = pl.program_id(2)
    ki = pl.program_id(3)
    nk = pl.num_programs(3)

    @pl.when(ki == 0)
    def _():
        m_ref[...] = jnp.full(m_ref.shape, -jnp.inf, F32)
        l_ref[...] = jnp.zeros(l_ref.shape, F32)
        acc_ref[...] = jnp.zeros(acc_ref.shape, F32)

    @pl.when(ki * tk <= qi * tq + (tq - 1))
    def _():
        q = q_ref[0]
        kb = k_ref[0].astype(BF16)
        vb = v_ref[0].astype(BF16)
        lane = lax.broadcasted_iota(jnp.int32, q.shape, 1)
        rowp = qi * tq + lax.broadcasted_iota(jnp.int32, (tq, tk), 0)
        colp = ki * tk + lax.broadcasted_iota(jnp.int32, (tq, tk), 1)
        mask = colp <= rowp
        for c in range(2):
            qc = jnp.where((lane < DIFF_DH) if c == 0 else (lane >= DIFF_DH), q, 0.0)
            s = jnp.where(mask, _dot_nt(qc, kb), -jnp.inf)
            m_prev = m_ref[c]
            m_new = jnp.maximum(m_prev, jnp.max(s, axis=-1, keepdims=True))
            alpha = jnp.exp(m_prev - m_new)
            p = jnp.exp(s - m_new)
            l_ref[c] = alpha * l_ref[c] + jnp.sum(p, axis=-1, keepdims=True)
            acc_ref[c] = alpha * acc_ref[c] + jnp.dot(p.astype(BF16), vb, preferred_element_type=F32)
            m_ref[c] = m_new

    @pl.when(ki == nk - 1)
    def _():
        lam = _lambda(lam_ref, lam_init)
        o = acc_ref[0] / l_ref[0] - lam * (acc_ref[1] / l_ref[1])
        o_ref[0] = _rms(o, subln_ref[...]) * (1.0 - lam_init)


def _attn_prompt(q, k, v, lam_p, subln, lam_init, tq, tk):
    B, L, D = q.shape
    H = DIFF_HEADS
    W = D // H
    kern = functools.partial(_attn_prompt_kernel, tq=tq, tk=tk, lam_init=lam_init)

    def kv_map(b, h, qi, ki):
        return (b, jnp.minimum(ki, (qi * tq + tq - 1) // tk), h)

    return pl.pallas_call(
        kern,
        grid=(B, H, L // tq, L // tk),
        in_specs=[pl.BlockSpec(lam_p.shape, lambda b, h, qi, ki: (0, 0)),
                  pl.BlockSpec((1, tq, W), lambda b, h, qi, ki: (b, qi, h)),
                  pl.BlockSpec((1, tk, W), kv_map),
                  pl.BlockSpec((1, tk, W), kv_map),
                  pl.BlockSpec((1, W), lambda b, h, qi, ki: (0, 0))],
        out_specs=pl.BlockSpec((1, tq, W), lambda b, h, qi, ki: (b, qi, h)),
        out_shape=jax.ShapeDtypeStruct((B, L, D), F32),
        scratch_shapes=[pltpu.VMEM((2, tq, 1), F32),
                        pltpu.VMEM((2, tq, 1), F32),
                        pltpu.VMEM((2, tq, W), F32)],
        compiler_params=_params("parallel", "parallel", "parallel", "arbitrary"),
        name="diff_attn_prompt",
    )(lam_p, q, k, v, subln.reshape(1, W))


def _attn_sample_kernel(pt_ref, lam_ref, q_ref, ck_ref, cv_ref, kn_ref, vn_ref, subln_ref, o_ref,
                        m_ref, l_ref, acc_ref, *, n_pages, n_new, lam_init):
    del pt_ref
    p = pl.program_id(1)
    H = DIFF_HEADS
    R = SUBLANES
    q = q_ref[0]
    lane = lax.broadcasted_iota(jnp.int32, (R, LANES), 1)

    @pl.when(p == 0)
    def _():
        m_ref[...] = jnp.full(m_ref.shape, -jnp.inf, F32)
        l_ref[...] = jnp.zeros(l_ref.shape, F32)
        acc_ref[...] = jnp.zeros(acc_ref.shape, F32)

    def qstack(hd):
        qh = q[:, hd * LANES:(hd + 1) * LANES]
        return jnp.concatenate([jnp.where(lane < DIFF_DH, qh, 0.0), jnp.where(lane >= DIFF_DH, qh, 0.0)], axis=0)

    @pl.when(p < n_pages)
    def _():
        for hd in range(H):
            sl = slice(hd * LANES, (hd + 1) * LANES)
            s = _dot_nt(qstack(hd), ck_ref[0, :, sl])
            m_prev = m_ref[hd]
            m_new = jnp.maximum(m_prev, jnp.max(s, axis=-1, keepdims=True))
            alpha = jnp.exp(m_prev - m_new)
            pr = jnp.exp(s - m_new)
            l_ref[hd] = alpha * l_ref[hd] + jnp.sum(pr, axis=-1, keepdims=True)
            acc_ref[hd] = alpha * acc_ref[hd] + _dot(pr, cv_ref[0, :, sl])
            m_ref[hd] = m_new

    @pl.when(p == n_pages)
    def _():
        lam = _lambda(lam_ref, lam_init)
        qrow = lax.broadcasted_iota(jnp.int32, (2 * R, 1), 0) & (R - 1)
        outs = []
        for hd in range(H):
            sl = slice(hd * LANES, (hd + 1) * LANES)
            qs = qstack(hd)
            kn = kn_ref[0, :, sl]
            vn = vn_ref[0, :, sl]
            ss = []
            for j in range(n_new):
                sj = jnp.sum(qs * kn[j:j + 1, :], axis=-1, keepdims=True)
                ss.append(jnp.where(qrow >= j, sj, -jnp.inf))
            m_new = m_ref[hd]
            for sj in ss:
                m_new = jnp.maximum(m_new, sj)
            alpha = jnp.exp(m_ref[hd] - m_new)
            l_new = alpha * l_ref[hd]
            acc = alpha * acc_ref[hd]
            for j, sj in enumerate(ss):
                pj = jnp.exp(sj - m_new)
                l_new = l_new + pj
                acc = acc + pj * vn[j:j + 1, :]
            o = acc[0:R] / l_new[0:R] - lam * (acc[R:2 * R] / l_new[R:2 * R])
            outs.append(_rms(o, subln_ref[...]) * (1.0 - lam_init))
        o_ref[0] = jnp.concatenate(outs, axis=1)


def _attn_sample(q, cache_k, cache_v, page_table, k_new, v_new, lam_p, subln, lam_init, n_new):
    B, R, D = q.shape
    H = DIFF_HEADS
    n_phys, page, _ = cache_k.shape
    n_pages = page_table.shape[1]
    kern = functools.partial(_attn_sample_kernel, n_pages=n_pages, n_new=n_new, lam_init=lam_init)

    def page_map(b, p, pt):
        return (pt[b, jnp.minimum(p, n_pages - 1)], 0, 0)

    seq = pl.BlockSpec((1, R, D), lambda b, p, pt: (b, 0, 0))
    grid_spec = pltpu.PrefetchScalarGridSpec(
        num_scalar_prefetch=1,
        grid=(B, n_pages + 1),
        in_specs=[pl.BlockSpec(lam_p.shape, lambda b, p, pt: (0, 0)),
                  seq,
                  pl.BlockSpec((1, page, D), page_map),
                  pl.BlockSpec((1, page, D), page_map),
                  seq, seq,
                  pl.BlockSpec((1, LANES), lambda b, p, pt: (0, 0))],
        out_specs=seq,
        scratch_shapes=[pltpu.VMEM((H, 2 * R, 1), F32),
                        pltpu.VMEM((H, 2 * R, 1), F32),
                        pltpu.VMEM((H, 2 * R, LANES), F32)],
    )
    return pl.pallas_call(
        kern,
        grid_spec=grid_spec,
        out_shape=jax.ShapeDtypeStruct((B, R, D), F32),
        compiler_params=_params("parallel", "arbitrary"),
        name="diff_attn_sample",
    )(page_table, lam_p, q, cache_k, cache_v, k_new, v_new, subln.reshape(1, LANES))


def _lambda_init(layer):
    return 0.8 - 0.6 * math.exp(-0.3 * layer)


def _pad_rows(x, rows):
    return jnp.pad(x, [(0, 0), (0, rows - x.shape[1])] + [(0, 0)] * (x.ndim - 2))


def kernel(x_prompt, x_sample, state_delta, state_conv, cache_k, cache_v, page_table, norm_a, w_in_a, conv_w_a, a_log, dt_bias, onorm_a, w_out_a, kv_norm, w_kv, norm_b, w_q_b, lam_b, subln_b, w_out_b, norm_m, w_rg, b_rg, w_re, b_re, w_gate_e, w_up_e, w_down_e, final_norm):
    D = x_prompt.shape[-1]
    H = GDN_HEADS
    CH = 3 * H * GDN_DK
    n_in = w_in_a.shape[-1]
    n_in_pad = -(-n_in // LANES) * LANES
    depth = norm_m.shape[0]
    n_a = norm_a.shape[0]
    assert depth == 2 and n_a == 1 and norm_b.shape[0] == 1

    w_in_bf = jnp.pad(w_in_a[0], ((0, 0), (0, n_in_pad - n_in))).astype(BF16)
    w_out_a_bf = w_out_a[0].astype(BF16)
    w_q_bf = w_q_b[0].astype(BF16)
    w_kv_bf = w_kv.astype(BF16)
    w_out_b_bf = w_out_b[0].astype(BF16)
    wg_bf = w_gate_e.astype(BF16)
    wu_bf = w_up_e.astype(BF16)
    wd_bf = w_down_e.astype(BF16)
    n_route = N_GROUPS + N_EXPERTS
    wr = jnp.pad(jnp.concatenate([w_rg, w_re], axis=-1), ((0, 0), (0, 0), (0, LANES - n_route)))
    br = jnp.pad(jnp.concatenate([b_rg, b_re], axis=-1), ((0, 0), (0, LANES - n_route))).reshape(depth, 1, LANES)
    alog_l = jnp.zeros((1, LANES), F32).at[0, H:2 * H].set(a_log[0])
    dtb_l = jnp.zeros((1, LANES), F32).at[0, H:2 * H].set(dt_bias[0])

    def trunk(x, conv0, s0, past_len, paged):
        B, L, _ = x.shape
        T = B * L
        tm = min(256, T)
        h = x.reshape(T, D)

        proj = _norm_matmul(h, norm_a[0], w_in_bf, tm)
        C = min(GDN_CHUNK, -(-L // SUBLANES) * SUBLANES)
        Lp = -(-L // C) * C
        valid_last = L - (Lp - C)
        proj3 = _pad_rows(proj.reshape(B, L, n_in_pad), Lp)
        h3 = _pad_rows(h.reshape(B, L, D), Lp)
        conv0p = jnp.pad(conv0, ((0, 0), (SUBLANES - (CONV_W - 1), 0), (0, 0)))
        h3, s_new, cbuf = _gdn_mixer(proj3, h3, conv0p, s0, conv_w_a[0], alog_l, dtb_l, onorm_a[0],
                                     w_out_a_bf, C, valid_last)
        h = h3[:, :L].reshape(T, D)
        conv_new = cbuf[:, SUBLANES - (CONV_W - 1):]
        h = _hier_moe(h, norm_m[0], wr[0], br[0], wg_bf[0], wu_bf[0], wd_bf[0], final_norm, False, tm)

        pos = past_len + jnp.arange(L, dtype=jnp.int32)
        cos_t, sin_t = _rope_tables(pos)
        if L % tm != 0:
            cos_t = jnp.tile(cos_t, (T // L, 1))
            sin_t = jnp.tile(sin_t, (T // L, 1))
        q, k, v = _qkv_proj(h, norm_b[0], kv_norm, w_q_bf, w_kv_bf, cos_t, sin_t, tm)
        lam_init = _lambda_init(n_a)
        if paged is None:
            o = _attn_prompt(q.reshape(B, L, D), k.reshape(B, L, D), v.reshape(B, L, D),
                             lam_b[0], subln_b[0], lam_init, min(256, L), min(256, L))
            o = o.reshape(T, D)
        else:
            ck, cv, pt = paged
            R = SUBLANES
            o = _attn_sample(_pad_rows(q.reshape(B, L, D), R), ck, cv, pt,
                             _pad_rows(k.reshape(B, L, D), R), _pad_rows(v.reshape(B, L, D), R),
                             lam_b[0], subln_b[0], lam_init, L)
            o = o[:, :L].reshape(T, D)
        h = _matmul_residual(o, w_out_b_bf, h, tm)
        y = _hier_moe(h, norm_m[1], wr[1], br[1], wg_bf[1], wu_bf[1], wd_bf[1], final_norm, True, tm)
        kshape = (B, L, DIFF_HEADS, 2 * DIFF_DH)
        return y.reshape(B, L, D), s_new[None], conv_new[None], k.reshape(kshape), v.reshape(kshape)

    bp = x_prompt.shape[0]
    conv0_p = jnp.zeros((bp, CONV_W - 1, CH), F32)
    s0_p = jnp.zeros((bp, H, GDN_DK, GDN_DK), F32)
    y_p, sd_p, sc_p, k_p, v_p = trunk(x_prompt, conv0_p, s0_p, 0, None)

    n_phys, page = cache_k.shape[0], cache_k.shape[1]
    past_len = page_table.shape[1] * page
    ck = cache_k.reshape(n_phys, page, D)
    cv = cache_v.reshape(n_phys, page, D)
    y_s, sd_s, sc_s, k_s, v_s = trunk(x_sample, state_conv[0], state_delta[0], past_len, (ck, cv, page_table))
    return (y_p, y_s, sd_p, sc_p, k_p, v_p, sd_s, sc_s, k_s, v_s)
```

```python
import functools
import math

import numpy as np
import jax
import jax.numpy as jnp
from jax import lax
from jax.experimental import pallas as pl
from jax.experimental.pallas import tpu as pltpu

F32 = jnp.float32
BF16 = jnp.bfloat16
EPS = 1e-6
LANES = 128
SUBLANES = 8
VMEM_LIMIT = 56 * 1024 * 1024

CONV_W = 4
GDN_HEADS = 8
GDN_DK = 128
GDN_CHUNK = 64
DIFF_HEADS = 8
DIFF_DH = 64
ROPE_THETA = 10000.0
N_GROUPS = 4
EXP_PER_GROUP = 8
N_EXPERTS = N_GROUPS * EXP_PER_GROUP
TOP_K = 2
MOE_BLOCK = 128
ROUTE_OFF = N_GROUPS
ATTN_TILE = 512
PAGES_PER_STEP = 8
ATTN_QSUB = LANES
ATTN_KSUB = 256
LOG2E = 1.4426950408889634
DMA_UNROLL = 8


def _params(*sem):
    return pltpu.CompilerParams(dimension_semantics=sem, vmem_limit_bytes=VMEM_LIMIT)


def _dot(a, b):
    return jnp.dot(a.astype(BF16), b.astype(BF16), preferred_element_type=F32)


def _dot_nt(a, b):
    return lax.dot_general(a.astype(BF16), b.astype(BF16), (((1,), (1,)), ((), ())),
                           preferred_element_type=F32)


def _dot_tn(a, b):
    return lax.dot_general(a.astype(BF16), b.astype(BF16), (((0,), (0,)), ((), ())),
                           preferred_element_type=F32)


def _split2(x):
    hi = x.astype(BF16)
    lo = (x - hi.astype(F32)).astype(BF16)
    return hi, lo


def _dot3(a, b):
    ah, al = _split2(a)
    bh, bl = _split2(b)
    d = functools.partial(jnp.dot, preferred_element_type=F32)
    return d(ah, bh) + (d(ah, bl) + d(al, bh))


def _dot_exact_lhs(a_bf, b):
    b1 = b.astype(BF16)
    r1 = b - b1.astype(F32)
    b2 = r1.astype(BF16)
    b3 = (r1 - b2.astype(F32)).astype(BF16)
    d = functools.partial(jnp.dot, preferred_element_type=F32)
    return d(a_bf, b1) + (d(a_bf, b2) + d(a_bf, b3))


def _sigmoid(x):
    return 1.0 / (1.0 + jnp.exp(-x))


def _silu(x):
    return x * _sigmoid(x)


def _softplus(x):
    return jnp.maximum(x, 0.0) + jnp.log1p(jnp.exp(-jnp.abs(x)))


def _rms(x, g):
    return x * lax.rsqrt(jnp.mean(x * x, axis=-1, keepdims=True) + EPS) * g


def _norm_mm_kernel(x_ref, g_ref, w_ref, o_ref):
    xn = _rms(x_ref[...], g_ref[...])
    o_ref[...] = jnp.dot(xn.astype(BF16), w_ref[...], preferred_element_type=F32)


def _norm_matmul(x, g, w_bf, tm):
    T, D = x.shape
    N = w_bf.shape[1]
    return pl.pallas_call(
        _norm_mm_kernel,
        grid=(T // tm,),
        in_specs=[pl.BlockSpec((tm, D), lambda i: (i, 0)),
                  pl.BlockSpec((1, D), lambda i: (0, 0)),
                  pl.BlockSpec((D, N), lambda i: (0, 0))],
        out_specs=pl.BlockSpec((tm, N), lambda i: (i, 0)),
        out_shape=jax.ShapeDtypeStruct((T, N), F32),
        compiler_params=_params("parallel"),
        name="norm_matmul",
    )(x, g.reshape(1, D), w_bf)


def _mm_res_kernel(x_ref, w_ref, h_ref, o_ref):
    o_ref[...] = h_ref[...] + jnp.dot(x_ref[...].astype(BF16), w_ref[...], preferred_element_type=F32)


def _matmul_residual(x, w_bf, h, tm):
    T, K = x.shape
    N = w_bf.shape[1]
    return pl.pallas_call(
        _mm_res_kernel,
        grid=(T // tm,),
        in_specs=[pl.BlockSpec((tm, K), lambda i: (i, 0)),
                  pl.BlockSpec((K, N), lambda i: (0, 0)),
                  pl.BlockSpec((tm, N), lambda i: (i, 0))],
        out_specs=pl.BlockSpec((tm, N), lambda i: (i, 0)),
        out_shape=jax.ShapeDtypeStruct((T, N), F32),
        compiler_params=_params("parallel"),
        name="matmul_residual",
    )(x, w_bf, h)


def _gdn_kernel(proj_ref, conv0_ref, s0_ref, cw_ref, alog_ref, dtb_ref, onorm_ref,
                o_ref, sout_ref, cout_ref, ext_ref, s_ref, *, C, nc, valid_last, bb):
    H, DK = GDN_HEADS, GDN_DK
    QK = H * DK
    CH = 3 * QK
    c = pl.program_id(1)
    masked = valid_last < C

    @pl.when(c == 0)
    def _():
        ext_ref[:, 0:SUBLANES, :] = conv0_ref[...]
        s_ref[...] = s0_ref[...]

    row1 = lax.broadcasted_iota(jnp.int32, (C, 1), 0)
    row = lax.broadcasted_iota(jnp.int32, (C, C), 0)
    col = lax.broadcasted_iota(jnp.int32, (C, C), 1)
    incl = row >= col
    strict = row > col
    eye = jnp.where(row == col, 1.0, 0.0).astype(F32)
    tril_bf = jnp.where(incl, 1.0, 0.0).astype(BF16)
    if masked:
        valid = jnp.logical_or(row1 < valid_last, c < nc - 1)

    seqs = []
    for b in range(bb):
        xin = proj_ref[b, :, 0:CH]
        ext_ref[b, SUBLANES:SUBLANES + C, :] = xin
        base = SUBLANES - (CONV_W - 1)
        acc = ext_ref[b, base:base + C, :] * cw_ref[0:1, :]
        for i in range(1, CONV_W - 1):
            acc = acc + ext_ref[b, base + i:base + i + C, :] * cw_ref[i:i + 1, :]
        acc = acc + xin * cw_ref[CONV_W - 1:CONV_W, :]
        qkv = _silu(acc)

        @pl.when(c == nc - 1)
        def _():
            cout_ref[b] = ext_ref[b, valid_last:valid_last + SUBLANES, :]

        ext_ref[b, 0:SUBLANES, :] = ext_ref[b, C:C + SUBLANES, :]

        ba = proj_ref[b, :, CH + QK:CH + QK + LANES]
        beta_t = _sigmoid(ba)
        g_t = -jnp.exp(alog_ref[...]) * _softplus(ba + dtb_ref[...])
        if masked:
            beta_t = jnp.where(valid, beta_t, 0.0)
            g_t = jnp.where(valid, g_t, 0.0)
        G = _dot_exact_lhs(tril_bf, g_t)
        g_last = G[C - 1:C, :]
        seqs.append(dict(qkv=qkv, beta=beta_t, G=G, GT=G.T, expG=jnp.exp(G),
                         exp_last=jnp.exp(g_last), k_scale=jnp.exp(g_last - G)))

    chains = [(b, hd) for b in range(bb) for hd in range(H)]
    st = []
    for b, hd in chains:
        sq = seqs[b]
        q = sq["qkv"][:, hd * DK:(hd + 1) * DK]
        k = sq["qkv"][:, QK + hd * DK:QK + (hd + 1) * DK]
        v = sq["qkv"][:, 2 * QK + hd * DK:2 * QK + (hd + 1) * DK]
        q = q * lax.rsqrt(jnp.sum(q * q, axis=-1, keepdims=True) + EPS) * (DK ** -0.5)
        k = k * lax.rsqrt(jnp.sum(k * k, axis=-1, keepdims=True) + EPS)
        if masked:
            k = jnp.where(valid, k, 0.0)
        bcol = sq["beta"][:, hd:hd + 1]
        gcol = sq["G"][:, H + hd:H + hd + 1]
        grow = sq["GT"][H + hd:H + hd + 1, :]
        eg = sq["expG"][:, H + hd:H + hd + 1]
        decay = jnp.exp(jnp.where(incl, gcol - grow, -jnp.inf))
        st.append(dict(q=q, k=k, decay=decay, eg=eg, bcol=bcol,
                       rhs=jnp.concatenate([bcol * v, (bcol * eg) * k], axis=1)))
    for d in st:
        kk = _dot_nt(d["k"], d["k"])
        d["p"] = -jnp.where(strict, d["bcol"] * kk * d["decay"], 0.0)
        d["t"] = eye + d["p"]
    for _ in range(int(math.log2(C)) - 1):
        for d in st:
            d["p"] = _dot3(d["p"], d["p"])
        for d in st:
            d["t"] = d["t"] + _dot3(d["t"], d["p"])
    for d in st:
        d["sol"] = _dot3(d["t"], d["rhs"])
    for d, (b, hd) in zip(st, chains):
        d["S"] = s_ref[b, hd]
        d["u"] = d["sol"][:, :DK] - _dot(d["sol"][:, DK:], d["S"])
    for d in st:
        d["qk"] = _dot_nt(d["q"], d["k"]) * d["decay"]
    for d in st:
        d["o"] = _dot(d["q"] * d["eg"], d["S"]) + _dot(d["qk"], d["u"])
    for d, (b, hd) in zip(st, chains):
        sq = seqs[b]
        s_ref[b, hd] = (sq["exp_last"][:, H + hd:H + hd + 1] * d["S"]
                        + _dot_tn(d["k"] * sq["k_scale"][:, H + hd:H + hd + 1], d["u"]))
    for d, (b, hd) in zip(st, chains):
        z = proj_ref[b, :, CH + hd * DK:CH + (hd + 1) * DK]
        o_ref[b, :, hd * DK:(hd + 1) * DK] = _rms(d["o"], onorm_ref[...]) * _silu(z)

    @pl.when(c == nc - 1)
    def _():
        sout_ref[...] = s_ref[...]


def _gdn_mixer(proj, conv0p, s0, conv_w, alog_l, dtb_l, onorm, C, valid_last, bb):
    B, Lp, NP = proj.shape
    nc = Lp // C
    H, DK = GDN_HEADS, GDN_DK
    D = H * DK
    CH = 3 * H * DK
    kern = functools.partial(_gdn_kernel, C=C, nc=nc, valid_last=valid_last, bb=bb)
    return pl.pallas_call(
        kern,
        grid=(B // bb, nc),
        in_specs=[pl.BlockSpec((bb, C, NP), lambda b, c: (b, c, 0)),
                  pl.BlockSpec((bb, SUBLANES, CH), lambda b, c: (b, 0, 0)),
                  pl.BlockSpec((bb, H, DK, DK), lambda b, c: (b, 0, 0, 0)),
                  pl.BlockSpec((CONV_W, CH), lambda b, c: (0, 0)),
                  pl.BlockSpec((1, LANES), lambda b, c: (0, 0)),
                  pl.BlockSpec((1, LANES), lambda b, c: (0, 0)),
                  pl.BlockSpec((1, DK), lambda b, c: (0, 0))],
        out_specs=[pl.BlockSpec((bb, C, D), lambda b, c: (b, c, 0)),
                   pl.BlockSpec((bb, H, DK, DK), lambda b, c: (b, 0, 0, 0)),
                   pl.BlockSpec((bb, SUBLANES, CH), lambda b, c: (b, 0, 0))],
        out_shape=[jax.ShapeDtypeStruct((B, Lp, D), F32),
                   jax.ShapeDtypeStruct((B, H, DK, DK), F32),
                   jax.ShapeDtypeStruct((B, SUBLANES, CH), F32)],
        scratch_shapes=[pltpu.VMEM((bb, SUBLANES + C, CH), F32),
                        pltpu.VMEM((bb, H, DK, DK), F32)],
        compiler_params=_params("parallel", "arbitrary"),
        name="gdn_mixer",
    )(proj, conv0p, s0, conv_w, alog_l, dtb_l, onorm.reshape(1, DK))


def _router_kernel(h_ref, g_ref, wr_ref, br_ref, xn_ref, gate_ref, eidx_ref):
    xn = _rms(h_ref[...], g_ref[...])
    xn_ref[...] = xn
    logits = _dot3(xn, wr_ref[...]) + br_ref[...]
    tm = logits.shape[0]
    lane = lax.broadcasted_iota(jnp.int32, (tm, LANES), 1).astype(F32)
    neg = -jnp.inf
    big = float(LANES)

    lg = jnp.where(lane < N_GROUPS, logits, neg)
    eg = jnp.exp(lg - jnp.max(lg, axis=-1, keepdims=True))
    pg = eg / jnp.sum(eg, axis=-1, keepdims=True)
    pmax = jnp.max(pg, axis=-1, keepdims=True)
    g_sel = jnp.min(jnp.where(pg == pmax, lane, big), axis=-1, keepdims=True)

    lo = ROUTE_OFF + EXP_PER_GROUP * g_sel
    in_grp = jnp.logical_and(lane >= lo, lane < lo + EXP_PER_GROUP)
    le = jnp.where(in_grp, logits, neg)
    ee = jnp.exp(le - jnp.max(le, axis=-1, keepdims=True))
    pe = ee / jnp.sum(ee, axis=-1, keepdims=True)
    pe = jnp.where(in_grp, pe, -1.0)
    p1 = jnp.max(pe, axis=-1, keepdims=True)
    i1 = jnp.min(jnp.where(pe == p1, lane, big), axis=-1, keepdims=True)
    pe2 = jnp.where(lane == i1, -1.0, pe)
    p2 = jnp.max(pe2, axis=-1, keepdims=True)
    i2 = jnp.min(jnp.where(pe2 == p2, lane, big), axis=-1, keepdims=True)
    den = p1 + p2
    gate_ref[...] = jnp.where(lane == 0, pmax * p1 / den, jnp.where(lane == 1, pmax * p2 / den, 0.0))
    eidx_ref[...] = jnp.where(lane == 0, i1 - ROUTE_OFF,
                              jnp.where(lane == 1, i2 - ROUTE_OFF, 0.0)).astype(jnp.int32)


def _router(h, g, wr, br, tm):
    T, D = h.shape
    return pl.pallas_call(
        _router_kernel,
        grid=(T // tm,),
        in_specs=[pl.BlockSpec((tm, D), lambda i: (i, 0)),
                  pl.BlockSpec((1, D), lambda i: (0, 0)),
                  pl.BlockSpec((D, LANES), lambda i: (0, 0)),
                  pl.BlockSpec((1, LANES), lambda i: (0, 0))],
        out_specs=[pl.BlockSpec((tm, D), lambda i: (i, 0)),
                   pl.BlockSpec((tm, LANES), lambda i: (i, 0)),
                   pl.BlockSpec((tm, LANES), lambda i: (i, 0))],
        out_shape=[jax.ShapeDtypeStruct((T, D), F32),
                   jax.ShapeDtypeStruct((T, LANES), F32),
                   jax.ShapeDtypeStruct((T, LANES), jnp.int32)],
        compiler_params=_params("parallel"),
        name="moe_router",
    )(h, g.reshape(1, D), wr, br)


def _row_copy(src_hbm, row, dst, slot_idx, sem):
    return pltpu.make_async_copy(src_hbm.at[pl.ds(row, 1)], dst.at[slot_idx], sem)


def _expert_kernel(be_ref, rt_ref, nu_ref, xn_hbm, wg_ref, wu_ref, wd_ref, ys_ref, xbuf, sem):
    del be_ref
    i = pl.program_id(0)
    n_used = nu_ref[0]

    def issue(blk, slot):
        def body(r, carry):
            tok = rt_ref[blk * MOE_BLOCK + r]
            _row_copy(xn_hbm, tok, xbuf, (slot, pl.ds(r, 1)), sem.at[slot]).start()
            return carry
        lax.fori_loop(0, MOE_BLOCK, body, 0, unroll=DMA_UNROLL)

    def wait(slot):
        for r in range(MOE_BLOCK):
            _row_copy(xn_hbm, 0, xbuf, (slot, pl.ds(r, 1)), sem.at[slot]).wait()

    @pl.when(i == 0)
    def _():
        issue(0, 0)

    @pl.when(i + 1 < n_used)
    def _():
        issue(i + 1, (i + 1) % 2)

    @pl.when(i < n_used)
    def _():
        slot = i % 2
        wait(slot)
        x = xbuf[slot].astype(BF16)
        gt = jnp.dot(x, wg_ref[0], preferred_element_type=F32)
        up = jnp.dot(x, wu_ref[0], preferred_element_type=F32)
        hid = _silu(gt) * up
        ys_ref[...] = jnp.dot(hid.astype(BF16), wd_ref[0], preferred_element_type=F32)

    @pl.when(i >= n_used)
    def _():
        ys_ref[...] = jnp.zeros(ys_ref.shape, F32)


def _experts(xn, block_e, row_tok, n_used, wg_bf, wu_bf, wd_bf):
    T, D = xn.shape
    FF = wg_bf.shape[-1]
    n_blocks = block_e.shape[0]
    P = n_blocks * MOE_BLOCK
    grid_spec = pltpu.PrefetchScalarGridSpec(
        num_scalar_prefetch=3,
        grid=(n_blocks,),
        in_specs=[pl.BlockSpec(memory_space=pl.ANY),
                  pl.BlockSpec((1, D, FF), lambda i, be, rt, nu: (be[i], 0, 0)),
                  pl.BlockSpec((1, D, FF), lambda i, be, rt, nu: (be[i], 0, 0)),
                  pl.BlockSpec((1, FF, D), lambda i, be, rt, nu: (be[i], 0, 0))],
        out_specs=pl.BlockSpec((MOE_BLOCK, D), lambda i, be, rt, nu: (i, 0)),
        scratch_shapes=[pltpu.VMEM((2, MOE_BLOCK, D), F32),
                        pltpu.SemaphoreType.DMA((2,))],
    )
    return pl.pallas_call(
        _expert_kernel,
        grid_spec=grid_spec,
        out_shape=jax.ShapeDtypeStruct((P, D), F32),
        compiler_params=_params("arbitrary"),
        name="moe_experts",
    )(block_e, row_tok, n_used, xn, wg_bf, wu_bf, wd_bf)


def _combine_kernel(dest_ref, ys_hbm, h_ref, gate_ref, fn_ref, o_ref, ybuf, sem, *, tm, final_norm):
    i = pl.program_id(0)
    n = pl.num_programs(0)

    def issue(step, slot):
        def body(r, carry):
            for kk in range(TOP_K):
                d = dest_ref[(step * tm + r) * TOP_K + kk]
                _row_copy(ys_hbm, d, ybuf, (slot, kk, pl.ds(r, 1)), sem.at[slot]).start()
            return carry
        lax.fori_loop(0, tm, body, 0, unroll=DMA_UNROLL // TOP_K)

    def wait(slot):
        for r in range(tm):
            for kk in range(TOP_K):
                _row_copy(ys_hbm, 0, ybuf, (slot, kk, pl.ds(r, 1)), sem.at[slot]).wait()

    @pl.when(i == 0)
    def _():
        issue(0, 0)

    @pl.when(i + 1 < n)
    def _():
        issue(i + 1, (i + 1) % 2)

    slot = i % 2
    wait(slot)
    gate = gate_ref[...]
    y = h_ref[...] + (ybuf[slot, 0] * gate[:, 0:1] + ybuf[slot, 1] * gate[:, 1:2])
    if final_norm:
        y = _rms(y, fn_ref[...])
    o_ref[...] = y


def _combine(ys, dest, h, gates, fnorm, tm, final_norm):
    T, D = h.shape
    kern = functools.partial(_combine_kernel, tm=tm, final_norm=final_norm)
    grid_spec = pltpu.PrefetchScalarGridSpec(
        num_scalar_prefetch=1,
        grid=(T // tm,),
        in_specs=[pl.BlockSpec(memory_space=pl.ANY),
                  pl.BlockSpec((tm, D), lambda i, d: (i, 0)),
                  pl.BlockSpec((tm, LANES), lambda i, d: (i, 0)),
                  pl.BlockSpec((1, D), lambda i, d: (0, 0))],
        out_specs=pl.BlockSpec((tm, D), lambda i, d: (i, 0)),
        scratch_shapes=[pltpu.VMEM((2, TOP_K, tm, D), F32),
                        pltpu.SemaphoreType.DMA((2,))],
    )
    return pl.pallas_call(
        kern,
        grid_spec=grid_spec,
        out_shape=jax.ShapeDtypeStruct((T, D), F32),
        compiler_params=_params("arbitrary"),
        name="moe_combine",
    )(dest, ys, h, gates, fnorm.reshape(1, D))


def _route_tables(eidx):
    T = eidx.shape[0]
    A = T * TOP_K
    flat_e = eidx.reshape(A)
    onehot = (flat_e[:, None] == jnp.arange(N_EXPERTS, dtype=jnp.int32)[None, :]).astype(jnp.int32)
    csum = jnp.cumsum(onehot, axis=0)
    rank = jnp.sum((csum - onehot) * onehot, axis=1)
    counts = csum[-1]
    padded = (counts + MOE_BLOCK - 1) // MOE_BLOCK * MOE_BLOCK
    pad_end = jnp.cumsum(padded)
    pad_start = pad_end - padded
    dest = (pad_start[flat_e] + rank).astype(jnp.int32)
    n_blocks = -(-A // MOE_BLOCK) + N_EXPERTS
    P = n_blocks * MOE_BLOCK
    row_tok = jnp.zeros((P,), jnp.int32).at[dest].set(jnp.arange(A, dtype=jnp.int32) // TOP_K)
    starts = jnp.arange(n_blocks, dtype=jnp.int32) * MOE_BLOCK
    block_e = jnp.minimum(jnp.searchsorted(pad_end, starts, side='right'), N_EXPERTS - 1).astype(jnp.int32)
    n_used = (pad_end[-1] // MOE_BLOCK).astype(jnp.int32).reshape(1)
    return dest, row_tok, block_e, n_used


def _hier_moe(h, norm_g, wr, br, wg_bf, wu_bf, wd_bf, fnorm, final_norm, tm):
    xn, gates, eidx = _router(h, norm_g, wr, br, tm)
    dest, row_tok, block_e, n_used = _route_tables(eidx[:, :TOP_K])
    ys = _experts(xn, block_e, row_tok, n_used, wg_bf, wu_bf, wd_bf)
    return _combine(ys, dest, h, gates, fnorm, min(tm, MOE_BLOCK), final_norm)


def _rope_tile(x, cos, sin_signed, lane):
    half = DIFF_DH // 2
    rot = jnp.where((lane & (DIFF_DH - 1)) < half,
                    pltpu.roll(x, LANES - half, axis=1),
                    pltpu.roll(x, half, axis=1))
    return x * cos + rot * sin_signed


def _qkv_kernel(h_ref, gq_ref, gkv_ref, wq_ref, wkv_ref, cos_ref, sin_ref, q_ref, k_ref, v_ref):
    x = h_ref[...]
    D = x.shape[1]
    xs = x * lax.rsqrt(jnp.mean(x * x, axis=-1, keepdims=True) + EPS)
    q = jnp.dot((xs * gq_ref[...]).astype(BF16), wq_ref[...], preferred_element_type=F32)
    kv = jnp.dot((xs * gkv_ref[...]).astype(BF16), wkv_ref[...], preferred_element_type=F32)
    cos = cos_ref[...]
    sin = sin_ref[...]
    lane = lax.broadcasted_iota(jnp.int32, cos.shape, 1)
    for hd in range(D // LANES):
        sl = slice(hd * LANES, (hd + 1) * LANES)
        q_ref[:, sl] = _rope_tile(q[:, sl], cos, sin, lane) * (DIFF_DH ** -0.5)
        k_ref[:, sl] = _rope_tile(kv[:, sl], cos, sin, lane)
    v_ref[...] = kv[:, D:]


def _qkv_proj(h, gq, gkv, wq_bf, wkv_bf, cos_t, sin_t, tm):
    T, D = h.shape
    nt = cos_t.shape[0] // tm
    row = pl.BlockSpec((tm, D), lambda i: (i, 0))
    vec = pl.BlockSpec((1, D), lambda i: (0, 0))
    tab = pl.BlockSpec((tm, LANES), lambda i: (i % nt, 0))
    return pl.pallas_call(
        _qkv_kernel,
        grid=(T // tm,),
        in_specs=[row, vec, vec,
                  pl.BlockSpec((D, D), lambda i: (0, 0)),
                  pl.BlockSpec((D, 2 * D), lambda i: (0, 0)),
                  tab, tab],
        out_specs=[row, row, row],
        out_shape=[jax.ShapeDtypeStruct((T, D), F32)] * 3,
        compiler_params=_params("parallel"),
        name="qkv_proj",
    )(h, gq.reshape(1, D), gkv.reshape(1, D), wq_bf, wkv_bf, cos_t, sin_t)


def _rope_tables(pos):
    half = DIFF_DH // 2
    inv = ROPE_THETA ** (-jnp.arange(half, dtype=F32) / half)
    ang = pos.astype(F32)[:, None] * inv[None, :]
    cos = jnp.cos(ang)
    sin = jnp.sin(ang)
    reps = LANES // DIFF_DH
    return (jnp.tile(jnp.concatenate([cos, cos], axis=1), (1, reps)),
            jnp.tile(jnp.concatenate([-sin, sin], axis=1), (1, reps)))


def _lambda(lam_ref, lam_init):
    lp = lam_ref[...]
    a = jnp.sum(lp[0:1, :] * lp[1:2, :], axis=-1, keepdims=True)
    b = jnp.sum(lp[2:3, :] * lp[3:4, :], axis=-1, keepdims=True)
    return jnp.exp(a) - jnp.exp(b) + lam_init


def _attn_prompt_kernel(qi_ref, ki_ref, lam_ref, q_ref, k_ref, v_ref, subln_ref, o_ref,
                        m_ref, l_ref, acc_ref, *, t, lam_init):
    j = pl.program_id(2)
    qi = qi_ref[j]
    ki = ki_ref[j]

    @pl.when(ki == 0)
    def _():
        m_ref[...] = jnp.full(m_ref.shape, -jnp.inf, F32)
        l_ref[...] = jnp.zeros(l_ref.shape, F32)
        acc_ref[...] = jnp.zeros(acc_ref.shape, F32)

    QS, KS = min(ATTN_QSUB, t), min(ATTN_KSUB, t)

    def update(diagonal):
        q = q_ref[0] * LOG2E
        kb = k_ref[0].astype(BF16)
        vT = v_ref[0].T.astype(BF16)
        lane = lax.broadcasted_iota(jnp.int32, q.shape, 1)
        for c in range(2):
            qc = jnp.where((lane < DIFF_DH) if c == 0 else (lane >= DIFF_DH), q, 0.0).astype(BF16)
            for qs in range(t // QS):
                qsub = qc[qs * QS:(qs + 1) * QS]
                m = m_ref[c, qs]
                l = l_ref[c, qs]
                acc = acc_ref[c, qs]
                for ks in range(t // KS):
                    if diagonal and ks * KS > qs * QS + QS - 1:
                        continue
                    sT = _dot_nt(kb[ks * KS:(ks + 1) * KS], qsub)
                    if diagonal and (ks + 1) * KS - 1 > qs * QS:
                        keyi = ks * KS + lax.broadcasted_iota(jnp.int32, (KS, QS), 0)
                        qidx = qs * QS + lax.broadcasted_iota(jnp.int32, (KS, QS), 1)
                        sT = jnp.where(keyi <= qidx, sT, -jnp.inf)
                    m_new = jnp.maximum(m, jnp.max(sT, axis=0, keepdims=True))
                    alpha = jnp.exp2(m - m_new)
                    p = jnp.exp2(sT - m_new)
                    l = alpha * l + jnp.sum(p, axis=0, keepdims=True)
                    acc = alpha * acc + jnp.dot(vT[:, ks * KS:(ks + 1) * KS], p.astype(BF16),
                                                preferred_element_type=F32)
                    m = m_new
                m_ref[c, qs] = m
                l_ref[c, qs] = l
                acc_ref[c, qs] = acc

    @pl.when(ki < qi)
    def _():
        update(False)

    @pl.when(ki == qi)
    def _():
        update(True)
        lam = _lambda(lam_ref, lam_init)
        for qs in range(t // QS):
            oT = acc_ref[0, qs] / l_ref[0, qs] - lam * (acc_ref[1, qs] / l_ref[1, qs])
            o_ref[0, qs * QS:(qs + 1) * QS, :] = _rms(oT.T, subln_ref[...]) * (1.0 - lam_init)


def _attn_prompt(q, k, v, lam_p, subln, lam_init, t):
    B, L, D = q.shape
    H = DIFF_HEADS
    W = D // H
    n = L // t
    pairs = [(qi, ki) for qi in range(n) for ki in range(qi + 1)]
    qi_tab = jnp.asarray(np.array([p[0] for p in pairs], np.int32))
    ki_tab = jnp.asarray(np.array([p[1] for p in pairs], np.int32))
    qs = min(ATTN_QSUB, t)
    assert t % qs == 0 and t % min(ATTN_KSUB, t) == 0
    kern = functools.partial(_attn_prompt_kernel, t=t, lam_init=lam_init)
    grid_spec = pltpu.PrefetchScalarGridSpec(
        num_scalar_prefetch=2,
        grid=(B, H, len(pairs)),
        in_specs=[pl.BlockSpec(lam_p.shape, lambda b, h, j, qt, kt: (0, 0)),
                  pl.BlockSpec((1, t, W), lambda b, h, j, qt, kt: (b, qt[j], h)),
                  pl.BlockSpec((1, t, W), lambda b, h, j, qt, kt: (b, kt[j], h)),
                  pl.BlockSpec((1, t, W), lambda b, h, j, qt, kt: (b, kt[j], h)),
                  pl.BlockSpec((1, W), lambda b, h, j, qt, kt: (0, 0))],
        out_specs=pl.BlockSpec((1, t, W), lambda b, h, j, qt, kt: (b, qt[j], h)),
        scratch_shapes=[pltpu.VMEM((2, t // qs, 1, qs), F32),
                        pltpu.VMEM((2, t // qs, 1, qs), F32),
                        pltpu.VMEM((2, t // qs, W, qs), F32)],
    )
    return pl.pallas_call(
        kern,
        grid_spec=grid_spec,
        out_shape=jax.ShapeDtypeStruct((B, L, D), F32),
        compiler_params=_params("parallel", "parallel", "arbitrary"),
        name="diff_attn_prompt",
    )(qi_tab, ki_tab, lam_p, q, k, v, subln.reshape(1, W))


def _attn_sample_kernel(pt_ref, lam_ref, qx_ref, *rest, G, n_groups, n_new, lam_init):
    del pt_ref
    ck_refs, cv_refs = rest[:G], rest[G:2 * G]
    kn_ref, vn_ref, subln_ref, o_ref, m_ref, l_ref, acc_ref = rest[2 * G:]
    H = DIFF_HEADS
    p = pl.program_id(1)
    R = qx_ref.shape[1]
    qb = qx_ref[0].astype(BF16)

    @pl.when(p == 0)
    def _():
        m_ref[...] = jnp.full(m_ref.shape, -jnp.inf, F32)
        l_ref[...] = jnp.zeros(l_ref.shape, F32)
        acc_ref[...] = jnp.zeros(acc_ref.shape, F32)

    def head_mask(width):
        rowi = lax.broadcasted_iota(jnp.int32, (R, width), 0)
        coli = lax.broadcasted_iota(jnp.int32, (R, width), 1)
        return rowi, coli, (coli & (H - 1)) == (rowi >> 3)

    def update(s_list, v_list):
        m_prev = m_ref[...]
        m_new = m_prev
        for s in s_list:
            m_new = jnp.maximum(m_new, jnp.max(s, axis=-1, keepdims=True))
        alpha = jnp.exp(m_prev - m_new)
        l_new = alpha * l_ref[...]
        acc = alpha * acc_ref[...]
        for s, v in zip(s_list, v_list):
            pr = jnp.exp(s - m_new)
            l_new = l_new + jnp.sum(pr, axis=-1, keepdims=True)
            acc = acc + _dot(pr, v)
        m_ref[...] = m_new
        l_ref[...] = l_new
        acc_ref[...] = acc

    @pl.when(p < n_groups)
    def _():
        _, _, hm = head_mask(ck_refs[0].shape[1])
        update([jnp.where(hm, _dot_nt(qb, r[0]), -jnp.inf) for r in ck_refs], [r[0] for r in cv_refs])

    @pl.when(p == n_groups)
    def _():
        rowi, coli, hm = head_mask(kn_ref.shape[1])
        causal = (coli >> 3) <= (rowi & (n_new - 1))
        s = jnp.where(jnp.logical_and(hm, causal), _dot_nt(qb, kn_ref[0]), -jnp.inf)
        update([s], [vn_ref[0]])
        lam = _lambda(lam_ref, lam_init)
        o8 = acc_ref[...] / l_ref[...]
        for hd in range(H):
            blk = o8[hd * SUBLANES:(hd + 1) * SUBLANES]
            diff = blk - lam * pltpu.roll(blk, SUBLANES // 2, axis=0)
            o_ref[0, hd * SUBLANES:(hd + 1) * SUBLANES, :] = _rms(diff, subln_ref[...]) * (1.0 - lam_init)


def _attn_sample(qx, cache_k, cache_v, page_table, k_new, v_new, lam_p, subln, lam_init, n_new):
    B, R, W = qx.shape
    rows = cache_k.shape[1]
    n_pages = page_table.shape[1]
    G = math.gcd(n_pages, PAGES_PER_STEP)
    n_groups = n_pages // G
    kern = functools.partial(_attn_sample_kernel, G=G, n_groups=n_groups, n_new=n_new, lam_init=lam_init)

    def page_spec(j):
        return pl.BlockSpec((1, rows, W), lambda b, p, pt: (pt[b, jnp.minimum(p, n_groups - 1) * G + j], 0, 0))

    def seq(r):
        return pl.BlockSpec((1, r, W), lambda b, p, pt: (b, 0, 0))

    grid_spec = pltpu.PrefetchScalarGridSpec(
        num_scalar_prefetch=1,
        grid=(B, n_groups + 1),
        in_specs=([pl.BlockSpec(lam_p.shape, lambda b, p, pt: (0, 0)), seq(R)]
                  + [page_spec(j) for j in range(G)] * 2
                  + [seq(k_new.shape[1]), seq(v_new.shape[1]),
                     pl.BlockSpec((1, W), lambda b, p, pt: (0, 0))]),
        out_specs=seq(R),
        scratch_shapes=[pltpu.VMEM((R, 1), F32),
                        pltpu.VMEM((R, 1), F32),
                        pltpu.VMEM((R, W), F32)],
    )
    return pl.pallas_call(
        kern,
        grid_spec=grid_spec,
        out_shape=jax.ShapeDtypeStruct((B, R, W), F32),
        compiler_params=_params("parallel", "arbitrary"),
        name="diff_attn_sample",
    )(page_table, lam_p, qx, *([cache_k] * G), *([cache_v] * G), k_new, v_new, subln.reshape(1, W))


def _lambda_init(layer):
    return 0.8 - 0.6 * math.exp(-0.3 * layer)


def _pad_rows(x, rows):
    return jnp.pad(x, [(0, 0), (0, rows - x.shape[1])] + [(0, 0)] * (x.ndim - 2))


def kernel(x_prompt, x_sample, state_delta, state_conv, cache_k, cache_v, page_table, norm_a, w_in_a, conv_w_a, a_log, dt_bias, onorm_a, w_out_a, kv_norm, w_kv, norm_b, w_q_b, lam_b, subln_b, w_out_b, norm_m, w_rg, b_rg, w_re, b_re, w_gate_e, w_up_e, w_down_e, final_norm):
    D = x_prompt.shape[-1]
    H = GDN_HEADS
    CH = 3 * H * GDN_DK
    n_in = w_in_a.shape[-1]
    n_in_pad = -(-n_in // LANES) * LANES
    depth = norm_m.shape[0]
    n_a = norm_a.shape[0]
    assert depth == 2 and n_a == 1 and norm_b.shape[0] == 1

    w_in_bf = jnp.pad(w_in_a[0], ((0, 0), (0, n_in_pad - n_in))).astype(BF16)
    w_out_a_bf = w_out_a[0].astype(BF16)
    w_q_bf = w_q_b[0].astype(BF16)
    w_kv_bf = w_kv.astype(BF16)
    w_out_b_bf = w_out_b[0].astype(BF16)
    wg_bf = w_gate_e.astype(BF16)
    wu_bf = w_up_e.astype(BF16)
    wd_bf = w_down_e.astype(BF16)
    n_route = N_GROUPS + N_EXPERTS
    wr = jnp.pad(jnp.concatenate([w_rg, w_re], axis=-1), ((0, 0), (0, 0), (0, LANES - n_route)))
    br = jnp.pad(jnp.concatenate([b_rg, b_re], axis=-1), ((0, 0), (0, LANES - n_route))).reshape(depth, 1, LANES)
    alog_l = jnp.zeros((1, LANES), F32).at[0, H:2 * H].set(a_log[0])
    dtb_l = jnp.zeros((1, LANES), F32).at[0, H:2 * H].set(dt_bias[0])

    def trunk(x, conv0, s0, past_len, paged):
        B, L, _ = x.shape
        T = B * L
        tm = min(256, T)
        h = x.reshape(T, D)

        proj = _norm_matmul(h, norm_a[0], w_in_bf, tm)
        C = min(GDN_CHUNK, -(-L // SUBLANES) * SUBLANES)
        Lp = -(-L // C) * C
        valid_last = L - (Lp - C)
        proj3 = _pad_rows(proj.reshape(B, L, n_in_pad), Lp)
        conv0p = jnp.pad(conv0, ((0, 0), (SUBLANES - (CONV_W - 1), 0), (0, 0)))
        bb = 1 if Lp > C else 2
        o3, s_new, cbuf = _gdn_mixer(proj3, conv0p, s0, conv_w_a[0], alog_l, dtb_l, onorm_a[0], C, valid_last, bb)
        h = _matmul_residual(o3[:, :L].reshape(T, D), w_out_a_bf, h, tm)
        conv_new = cbuf[:, SUBLANES - (CONV_W - 1):]
        h = _hier_moe(h, norm_m[0], wr[0], br[0], wg_bf[0], wu_bf[0], wd_bf[0], final_norm, False, tm)

        pos = past_len + jnp.arange(L, dtype=jnp.int32)
        cos_t, sin_t = _rope_tables(pos)
        if L % tm != 0:
            cos_t = jnp.tile(cos_t, (T // L, 1))
            sin_t = jnp.tile(sin_t, (T // L, 1))
        q, k, v = _qkv_proj(h, norm_b[0], kv_norm, w_q_bf, w_kv_bf, cos_t, sin_t, tm)
        lam_init = _lambda_init(n_a)
        NH, W = DIFF_HEADS, 2 * DIFF_DH
        if paged is None:
            o = _attn_prompt(q.reshape(B, L, D), k.reshape(B, L, D), v.reshape(B, L, D),
                             lam_b[0], subln_b[0], lam_init, min(ATTN_TILE, L))
            o = o.reshape(T, D)
        else:
            ck, cv, pt = paged
            assert 2 * L == SUBLANES and NH == SUBLANES
            q5 = q.reshape(B, L, NH, 2, DIFF_DH).transpose(0, 2, 1, 3, 4)
            sel = jnp.eye(2, dtype=F32)[None, None, :, None, :, None]
            qx = (q5[:, :, None] * sel).reshape(B, NH * 2 * L, W)
            o = _attn_sample(qx, ck, cv, pt, k.reshape(B, L * NH, W), v.reshape(B, L * NH, W),
                             lam_b[0], subln_b[0], lam_init, L)
            o = o.reshape(B, NH, 2 * L, W)[:, :, :L].transpose(0, 2, 1, 3).reshape(T, D)
        h = _matmul_residual(o, w_out_b_bf, h, tm)
        y = _hier_moe(h, norm_m[1], wr[1], br[1], wg_bf[1], wu_bf[1], wd_bf[1], final_norm, True, tm)
        kshape = (B, L, NH, W)
        return y.reshape(B, L, D), s_new[None], conv_new[None], k.reshape(kshape), v.reshape(kshape)

    bp = x_prompt.shape[0]
    conv0_p = jnp.zeros((bp, CONV_W - 1, CH), F32)
    s0_p = jnp.zeros((bp, H, GDN_DK, GDN_DK), F32)
    y_p, sd_p, sc_p, k_p, v_p = trunk(x_prompt, conv0_p, s0_p, 0, None)

    n_phys, page, nh, w = cache_k.shape
    past_len = page_table.shape[1] * page
    ck = cache_k.reshape(n_phys, page * nh, w)
    cv = cache_v.reshape(n_phys, page * nh, w)
    y_s, sd_s, sc_s, k_s, v_s = trunk(x_sample, state_conv[0], state_delta[0], past_len, (ck, cv, page_table))
    return (y_p, y_s, sd_p, sc_p, k_p, v_p, sd_s, sc_s, k_s, v_s)
```

```python
import functools
import math

import numpy as np
import jax
import jax.numpy as jnp
from jax import lax
from jax.experimental import pallas as pl
from jax.experimental.pallas import tpu as pltpu

F32 = jnp.float32
BF16 = jnp.bfloat16
EPS = 1e-6
LANES = 128
SUBLANES = 8
VMEM_LIMIT = 56 * 1024 * 1024

CONV_W = 4
GDN_HEADS = 8
GDN_DK = 128
GDN_CHUNK = 64
DIFF_HEADS = 8
DIFF_DH = 64
ROPE_THETA = 10000.0
N_GROUPS = 4
EXP_PER_GROUP = 8
N_EXPERTS = N_GROUPS * EXP_PER_GROUP
TOP_K = 2
MOE_BLOCK = 128
ROUTE_OFF = N_GROUPS
ATTN_TILE = 512
PAGES_PER_STEP = 8
ATTN_QSUB = LANES
ATTN_KSUB = 256
LOG2E = 1.4426950408889634
DMA_UNROLL = 8
ROUTER_TILE = 256


def _params(*sem):
    return pltpu.CompilerParams(dimension_semantics=sem, vmem_limit_bytes=VMEM_LIMIT)


def _dot(a, b):
    return jnp.dot(a.astype(BF16), b.astype(BF16), preferred_element_type=F32)


def _dot_nt(a, b):
    return lax.dot_general(a.astype(BF16), b.astype(BF16), (((1,), (1,)), ((), ())),
                           preferred_element_type=F32)


def _dot_tn(a, b):
    return lax.dot_general(a.astype(BF16), b.astype(BF16), (((0,), (0,)), ((), ())),
                           preferred_element_type=F32)


def _split2(x):
    hi = x.astype(BF16)
    lo = (x - hi.astype(F32)).astype(BF16)
    return hi, lo


def _dot3(a, b):
    ah, al = _split2(a)
    bh, bl = _split2(b)
    d = functools.partial(jnp.dot, preferred_element_type=F32)
    return d(ah, bh) + (d(ah, bl) + d(al, bh))


def _dot_exact_lhs(a_bf, b):
    b1 = b.astype(BF16)
    r1 = b - b1.astype(F32)
    b2 = r1.astype(BF16)
    b3 = (r1 - b2.astype(F32)).astype(BF16)
    d = functools.partial(jnp.dot, preferred_element_type=F32)
    return d(a_bf, b1) + (d(a_bf, b2) + d(a_bf, b3))


def _sigmoid(x):
    return 1.0 / (1.0 + jnp.exp(-x))


def _silu(x):
    return x * _sigmoid(x)


def _softplus(x):
    return jnp.maximum(x, 0.0) + jnp.log1p(jnp.exp(-jnp.abs(x)))


def _rms(x, g):
    return x * lax.rsqrt(jnp.mean(x * x, axis=-1, keepdims=True) + EPS) * g


def _norm_mm_kernel(x_ref, g_ref, w_ref, o_ref):
    xn = _rms(x_ref[...], g_ref[...])
    o_ref[...] = jnp.dot(xn.astype(BF16), w_ref[...], preferred_element_type=F32)


def _norm_matmul(x, g, w_bf, tm):
    T, D = x.shape
    N = w_bf.shape[1]
    return pl.pallas_call(
        _norm_mm_kernel,
        grid=(T // tm,),
        in_specs=[pl.BlockSpec((tm, D), lambda i: (i, 0)),
                  pl.BlockSpec((1, D), lambda i: (0, 0)),
                  pl.BlockSpec((D, N), lambda i: (0, 0))],
        out_specs=pl.BlockSpec((tm, N), lambda i: (i, 0)),
        out_shape=jax.ShapeDtypeStruct((T, N), F32),
        compiler_params=_params("parallel"),
        name="norm_matmul",
    )(x, g.reshape(1, D), w_bf)


def _mm_res_kernel(x_ref, w_ref, h_ref, o_ref):
    o_ref[...] = h_ref[...] + jnp.dot(x_ref[...].astype(BF16), w_ref[...], preferred_element_type=F32)


def _matmul_residual(x, w_bf, h, tm):
    T, K = x.shape
    N = w_bf.shape[1]
    return pl.pallas_call(
        _mm_res_kernel,
        grid=(T // tm,),
        in_specs=[pl.BlockSpec((tm, K), lambda i: (i, 0)),
                  pl.BlockSpec((K, N), lambda i: (0, 0)),
                  pl.BlockSpec((tm, N), lambda i: (i, 0))],
        out_specs=pl.BlockSpec((tm, N), lambda i: (i, 0)),
        out_shape=jax.ShapeDtypeStruct((T, N), F32),
        compiler_params=_params("parallel"),
        name="matmul_residual",
    )(x, w_bf, h)


def _gdn_kernel(proj_ref, conv0_ref, s0_ref, cw_ref, alog_ref, dtb_ref, onorm_ref,
                o_ref, sout_ref, cout_ref, ext_ref, s_ref, *, C, nc, valid_last, bb):
    H, DK = GDN_HEADS, GDN_DK
    QK = H * DK
    CH = 3 * QK
    c = pl.program_id(1)
    masked = valid_last < C

    @pl.when(c == 0)
    def _():
        ext_ref[:, 0:SUBLANES, :] = conv0_ref[...]
        s_ref[...] = s0_ref[...]

    row1 = lax.broadcasted_iota(jnp.int32, (C, 1), 0)
    row = lax.broadcasted_iota(jnp.int32, (C, C), 0)
    col = lax.broadcasted_iota(jnp.int32, (C, C), 1)
    incl = row >= col
    strict = row > col
    eye = jnp.where(row == col, 1.0, 0.0).astype(F32)
    tril_bf = jnp.where(incl, 1.0, 0.0).astype(BF16)
    if masked:
        valid = jnp.logical_or(row1 < valid_last, c < nc - 1)

    seqs = []
    for b in range(bb):
        xin = proj_ref[b, :, 0:CH]
        ext_ref[b, SUBLANES:SUBLANES + C, :] = xin
        base = SUBLANES - (CONV_W - 1)
        acc = ext_ref[b, base:base + C, :] * cw_ref[0:1, :]
        for i in range(1, CONV_W - 1):
            acc = acc + ext_ref[b, base + i:base + i + C, :] * cw_ref[i:i + 1, :]
        acc = acc + xin * cw_ref[CONV_W - 1:CONV_W, :]
        qkv = _silu(acc)

        @pl.when(c == nc - 1)
        def _():
            cout_ref[b] = ext_ref[b, valid_last:valid_last + SUBLANES, :]

        ext_ref[b, 0:SUBLANES, :] = ext_ref[b, C:C + SUBLANES, :]

        ba = proj_ref[b, :, CH + QK:CH + QK + LANES]
        beta_t = _sigmoid(ba)
        g_t = -jnp.exp(alog_ref[...]) * _softplus(ba + dtb_ref[...])
        if masked:
            beta_t = jnp.where(valid, beta_t, 0.0)
            g_t = jnp.where(valid, g_t, 0.0)
        G = _dot_exact_lhs(tril_bf, g_t)
        g_last = G[C - 1:C, :]
        seqs.append(dict(qkv=qkv, beta=beta_t, G=G, GT=G.T, expG=jnp.exp(G),
                         exp_last=jnp.exp(g_last), k_scale=jnp.exp(g_last - G)))

    chains = [(b, hd) for b in range(bb) for hd in range(H)]
    st = []
    for b, hd in chains:
        sq = seqs[b]
        q = sq["qkv"][:, hd * DK:(hd + 1) * DK]
        k = sq["qkv"][:, QK + hd * DK:QK + (hd + 1) * DK]
        v = sq["qkv"][:, 2 * QK + hd * DK:2 * QK + (hd + 1) * DK]
        q = q * lax.rsqrt(jnp.sum(q * q, axis=-1, keepdims=True) + EPS) * (DK ** -0.5)
        k = k * lax.rsqrt(jnp.sum(k * k, axis=-1, keepdims=True) + EPS)
        if masked:
            k = jnp.where(valid, k, 0.0)
        bcol = sq["beta"][:, hd:hd + 1]
        gcol = sq["G"][:, H + hd:H + hd + 1]
        grow = sq["GT"][H + hd:H + hd + 1, :]
        eg = sq["expG"][:, H + hd:H + hd + 1]
        decay = jnp.exp(jnp.where(incl, gcol - grow, -jnp.inf))
        st.append(dict(q=q, k=k, decay=decay, eg=eg, bcol=bcol,
                       rhs=jnp.concatenate([bcol * v, (bcol * eg) * k], axis=1)))
    for d in st:
        kk = _dot_nt(d["k"], d["k"])
        d["p"] = -jnp.where(strict, d["bcol"] * kk * d["decay"], 0.0)
        d["t"] = eye + d["p"]
    for _ in range(int(math.log2(C)) - 1):
        for d in st:
            d["p"] = _dot3(d["p"], d["p"])
        for d in st:
            d["t"] = d["t"] + _dot3(d["t"], d["p"])
    for d in st:
        d["sol"] = _dot3(d["t"], d["rhs"])
    for d, (b, hd) in zip(st, chains):
        d["S"] = s_ref[b, hd]
        d["u"] = d["sol"][:, :DK] - _dot(d["sol"][:, DK:], d["S"])
    for d in st:
        d["qk"] = _dot_nt(d["q"], d["k"]) * d["decay"]
    for d in st:
        d["o"] = _dot(d["q"] * d["eg"], d["S"]) + _dot(d["qk"], d["u"])
    for d, (b, hd) in zip(st, chains):
        sq = seqs[b]
        s_ref[b, hd] = (sq["exp_last"][:, H + hd:H + hd + 1] * d["S"]
                        + _dot_tn(d["k"] * sq["k_scale"][:, H + hd:H + hd + 1], d["u"]))
    for d, (b, hd) in zip(st, chains):
        z = proj_ref[b, :, CH + hd * DK:CH + (hd + 1) * DK]
        o_ref[b, :, hd * DK:(hd + 1) * DK] = _rms(d["o"], onorm_ref[...]) * _silu(z)

    @pl.when(c == nc - 1)
    def _():
        sout_ref[...] = s_ref[...]


def _gdn_mixer(proj, conv0p, s0, conv_w, alog_l, dtb_l, onorm, C, valid_last, bb):
    B, Lp, NP = proj.shape
    nc = Lp // C
    H, DK = GDN_HEADS, GDN_DK
    D = H * DK
    CH = 3 * H * DK
    kern = functools.partial(_gdn_kernel, C=C, nc=nc, valid_last=valid_last, bb=bb)
    return pl.pallas_call(
        kern,
        grid=(B // bb, nc),
        in_specs=[pl.BlockSpec((bb, C, NP), lambda b, c: (b, c, 0)),
                  pl.BlockSpec((bb, SUBLANES, CH), lambda b, c: (b, 0, 0)),
                  pl.BlockSpec((bb, H, DK, DK), lambda b, c: (b, 0, 0, 0)),
                  pl.BlockSpec((CONV_W, CH), lambda b, c: (0, 0)),
                  pl.BlockSpec((1, LANES), lambda b, c: (0, 0)),
                  pl.BlockSpec((1, LANES), lambda b, c: (0, 0)),
                  pl.BlockSpec((1, DK), lambda b, c: (0, 0))],
        out_specs=[pl.BlockSpec((bb, C, D), lambda b, c: (b, c, 0)),
                   pl.BlockSpec((bb, H, DK, DK), lambda b, c: (b, 0, 0, 0)),
                   pl.BlockSpec((bb, SUBLANES, CH), lambda b, c: (b, 0, 0))],
        out_shape=[jax.ShapeDtypeStruct((B, Lp, D), F32),
                   jax.ShapeDtypeStruct((B, H, DK, DK), F32),
                   jax.ShapeDtypeStruct((B, SUBLANES, CH), F32)],
        scratch_shapes=[pltpu.VMEM((bb, SUBLANES + C, CH), F32),
                        pltpu.VMEM((bb, H, DK, DK), F32)],
        compiler_params=_params("parallel", "arbitrary"),
        name="gdn_mixer",
    )(proj, conv0p, s0, conv_w, alog_l, dtb_l, onorm.reshape(1, DK))


def _router_kernel(hp_ref, hs_ref, g_ref, wr_ref, br_ref, xn_ref, gate_ref, route_ref, cnt_ref, carry_ref,
                   *, n_p):
    i = pl.program_id(0)

    @pl.when(i == 0)
    def _():
        carry_ref[...] = jnp.zeros(carry_ref.shape, F32)

    x = jnp.where(i < n_p, hp_ref[...], hs_ref[...])
    xn = _rms(x, g_ref[...])
    xn_ref[...] = xn
    logits = _dot3(xn, wr_ref[...]) + br_ref[...]
    tm = logits.shape[0]
    lane = lax.broadcasted_iota(jnp.int32, (tm, LANES), 1).astype(F32)
    neg = -jnp.inf
    big = float(LANES)

    lg = jnp.where(lane < N_GROUPS, logits, neg)
    eg = jnp.exp(lg - jnp.max(lg, axis=-1, keepdims=True))
    pg = eg / jnp.sum(eg, axis=-1, keepdims=True)
    pmax = jnp.max(pg, axis=-1, keepdims=True)
    g_sel = jnp.min(jnp.where(pg == pmax, lane, big), axis=-1, keepdims=True)

    lo = ROUTE_OFF + EXP_PER_GROUP * g_sel
    in_grp = jnp.logical_and(lane >= lo, lane < lo + EXP_PER_GROUP)
    le = jnp.where(in_grp, logits, neg)
    ee = jnp.exp(le - jnp.max(le, axis=-1, keepdims=True))
    pe = ee / jnp.sum(ee, axis=-1, keepdims=True)
    pe = jnp.where(in_grp, pe, -1.0)
    p1 = jnp.max(pe, axis=-1, keepdims=True)
    i1 = jnp.min(jnp.where(pe == p1, lane, big), axis=-1, keepdims=True)
    pe2 = jnp.where(lane == i1, -1.0, pe)
    p2 = jnp.max(pe2, axis=-1, keepdims=True)
    i2 = jnp.min(jnp.where(pe2 == p2, lane, big), axis=-1, keepdims=True)
    den = p1 + p2
    gate_ref[...] = jnp.where(lane == 0, pmax * p1 / den, jnp.where(lane == 1, pmax * p2 / den, 0.0))

    e1 = i1 - ROUTE_OFF
    e2 = i2 - ROUTE_OFF
    oh1 = jnp.where(lane == e1, 1.0, 0.0)
    oh2 = jnp.where(lane == e2, 1.0, 0.0)
    both = oh1 + oh2
    earlier = (lax.broadcasted_iota(jnp.int32, (tm, tm), 0) > lax.broadcasted_iota(jnp.int32, (tm, tm), 1))
    before = (jnp.dot(jnp.where(earlier, 1.0, 0.0).astype(BF16), both.astype(BF16), preferred_element_type=F32)
              + carry_ref[0:1, :])
    r1 = jnp.sum(oh1 * before, axis=-1, keepdims=True)
    r2 = jnp.sum(oh2 * before, axis=-1, keepdims=True)
    carry_ref[...] = carry_ref[...] + jnp.sum(both, axis=0, keepdims=True)
    cnt_ref[...] = carry_ref[...]
    route_ref[...] = jnp.where(lane == 0, e1, jnp.where(lane == 1, e2, jnp.where(
        lane == 2, r1, jnp.where(lane == 3, r2, 0.0)))).astype(jnp.int32)


def _router(hp, hs, g, wr, br, tm):
    Tp, D = hp.shape
    Ts = hs.shape[0]
    n_p, n_s = Tp // tm, Ts // tm
    T = Tp + Ts
    kern = functools.partial(_router_kernel, n_p=n_p)
    row = pl.BlockSpec((tm, D), lambda i: (i, 0))
    lanes = pl.BlockSpec((tm, LANES), lambda i: (i, 0))
    return pl.pallas_call(
        kern,
        grid=(n_p + n_s,),
        in_specs=[pl.BlockSpec((tm, D), lambda i: (jnp.minimum(i, n_p - 1), 0)),
                  pl.BlockSpec((tm, D), lambda i: (jnp.maximum(i - n_p, 0), 0)),
                  pl.BlockSpec((1, D), lambda i: (0, 0)),
                  pl.BlockSpec((D, LANES), lambda i: (0, 0)),
                  pl.BlockSpec((1, LANES), lambda i: (0, 0))],
        out_specs=[row, lanes, lanes, pl.BlockSpec((SUBLANES, LANES), lambda i: (0, 0))],
        out_shape=[jax.ShapeDtypeStruct((T, D), F32),
                   jax.ShapeDtypeStruct((T, LANES), F32),
                   jax.ShapeDtypeStruct((T, LANES), jnp.int32),
                   jax.ShapeDtypeStruct((SUBLANES, LANES), F32)],
        scratch_shapes=[pltpu.VMEM((SUBLANES, LANES), F32)],
        compiler_params=_params("arbitrary"),
        name="moe_router",
    )(hp, hs, g.reshape(1, D), wr, br)


def _row_copy(src_hbm, row, dst, slot_idx, sem):
    return pltpu.make_async_copy(src_hbm.at[pl.ds(row, 1)], dst.at[slot_idx], sem)


def _expert_kernel(be_ref, rt_ref, nu_ref, xn_hbm, wg_ref, wu_ref, wd_ref, ys_ref,
                   xbuf, wg_bf, wu_bf, wd_bf, sem):
    i = pl.program_id(0)
    n_used = nu_ref[0]

    def issue(blk, slot):
        for r in range(MOE_BLOCK):
            _row_copy(xn_hbm, rt_ref[blk * MOE_BLOCK + r], xbuf, (slot, pl.ds(r, 1)), sem.at[slot]).start()

    def wait(slot):
        for r in range(MOE_BLOCK):
            _row_copy(xn_hbm, 0, xbuf, (slot, pl.ds(r, 1)), sem.at[slot]).wait()

    @pl.when(i == 0)
    def _():
        issue(0, 0)

    @pl.when(jnp.logical_or(i == 0, be_ref[i] != be_ref[jnp.maximum(i - 1, 0)]))
    def _():
        wg_bf[...] = wg_ref[0, 0].astype(BF16)
        wu_bf[...] = wu_ref[0, 0].astype(BF16)
        wd_bf[...] = wd_ref[0, 0].astype(BF16)

    @pl.when(i < n_used)
    def _():
        slot = i % 2
        wait(slot)
        issue(jnp.minimum(i + 1, n_used - 1), 1 - slot)
        x = xbuf[slot].astype(BF16)
        gt = jnp.dot(x, wg_bf[...], preferred_element_type=F32)
        up = jnp.dot(x, wu_bf[...], preferred_element_type=F32)
        hid = _silu(gt) * up
        ys_ref[...] = jnp.dot(hid.astype(BF16), wd_bf[...], preferred_element_type=F32)

    @pl.when(i == n_used - 1)
    def _():
        wait(1 - i % 2)

    @pl.when(i >= n_used)
    def _():
        ys_ref[...] = jnp.zeros(ys_ref.shape, F32)


def _experts(xn, block_e, row_tok, n_used, w_gate, w_up, w_down, layer):
    T, D = xn.shape
    FF = w_gate.shape[-1]
    n_blocks = block_e.shape[0]
    P = n_blocks * MOE_BLOCK
    grid_spec = pltpu.PrefetchScalarGridSpec(
        num_scalar_prefetch=3,
        grid=(n_blocks,),
        in_specs=[pl.BlockSpec(memory_space=pl.ANY),
                  pl.BlockSpec((1, 1, D, FF), lambda i, be, rt, nu: (layer, be[i], 0, 0)),
                  pl.BlockSpec((1, 1, D, FF), lambda i, be, rt, nu: (layer, be[i], 0, 0)),
                  pl.BlockSpec((1, 1, FF, D), lambda i, be, rt, nu: (layer, be[i], 0, 0))],
        out_specs=pl.BlockSpec((MOE_BLOCK, D), lambda i, be, rt, nu: (i, 0)),
        scratch_shapes=[pltpu.VMEM((2, MOE_BLOCK, D), F32),
                        pltpu.VMEM((D, FF), BF16),
                        pltpu.VMEM((D, FF), BF16),
                        pltpu.VMEM((FF, D), BF16),
                        pltpu.SemaphoreType.DMA((2,))],
    )
    return pl.pallas_call(
        _expert_kernel,
        grid_spec=grid_spec,
        out_shape=jax.ShapeDtypeStruct((P, D), F32),
        compiler_params=_params("arbitrary"),
        name="moe_experts",
    )(block_e, row_tok, n_used, xn, w_gate, w_up, w_down)


def _combine_kernel(dest_ref, ys_hbm, hp_ref, hs_ref, gate_ref, fn_ref, op_ref, os_ref, ybuf, sem,
                    *, tm, n_p, final_norm):
    i = pl.program_id(0)
    n = pl.num_programs(0)

    def issue(step, slot):
        def body(r, carry):
            for kk in range(TOP_K):
                d = dest_ref[(step * tm + r) * TOP_K + kk]
                _row_copy(ys_hbm, d, ybuf, (slot, kk, pl.ds(r, 1)), sem.at[slot]).start()
            return carry
        lax.fori_loop(0, tm, body, 0, unroll=DMA_UNROLL // TOP_K)

    def wait(slot):
        for r in range(tm):
            for kk in range(TOP_K):
                _row_copy(ys_hbm, 0, ybuf, (slot, kk, pl.ds(r, 1)), sem.at[slot]).wait()

    @pl.when(i == 0)
    def _():
        issue(0, 0)

    @pl.when(i + 1 < n)
    def _():
        issue(i + 1, (i + 1) % 2)

    slot = i % 2
    wait(slot)
    gate = gate_ref[...]
    moe = ybuf[slot, 0] * gate[:, 0:1] + ybuf[slot, 1] * gate[:, 1:2]

    def finish(h_ref, o_ref):
        y = h_ref[...] + moe
        if final_norm:
            y = _rms(y, fn_ref[...])
        o_ref[...] = y

    @pl.when(i < n_p)
    def _():
        finish(hp_ref, op_ref)

    @pl.when(i >= n_p)
    def _():
        finish(hs_ref, os_ref)


def _combine(ys, dest, hp, hs, gates, fnorm, tm, final_norm):
    Tp, D = hp.shape
    Ts = hs.shape[0]
    n_p, n_s = Tp // tm, Ts // tm
    kern = functools.partial(_combine_kernel, tm=tm, n_p=n_p, final_norm=final_norm)
    p_blk = pl.BlockSpec((tm, D), lambda i, d: (jnp.minimum(i, n_p - 1), 0))
    s_blk = pl.BlockSpec((tm, D), lambda i, d: (jnp.maximum(i - n_p, 0), 0))
    grid_spec = pltpu.PrefetchScalarGridSpec(
        num_scalar_prefetch=1,
        grid=(n_p + n_s,),
        in_specs=[pl.BlockSpec(memory_space=pl.ANY), p_blk, s_blk,
                  pl.BlockSpec((tm, LANES), lambda i, d: (i, 0)),
                  pl.BlockSpec((1, D), lambda i, d: (0, 0))],
        out_specs=[p_blk, s_blk],
        scratch_shapes=[pltpu.VMEM((2, TOP_K, tm, D), F32),
                        pltpu.SemaphoreType.DMA((2,))],
    )
    return pl.pallas_call(
        kern,
        grid_spec=grid_spec,
        out_shape=[jax.ShapeDtypeStruct((Tp, D), F32), jax.ShapeDtypeStruct((Ts, D), F32)],
        compiler_params=_params("arbitrary"),
        name="moe_combine",
    )(dest, ys, hp, hs, gates, fnorm.reshape(1, D))


def _route_tables(route, counts):
    T = route.shape[0]
    A = T * TOP_K
    flat_e = route[:, 0:TOP_K].reshape(A)
    rank = route[:, TOP_K:2 * TOP_K].reshape(A)
    padded = (counts + MOE_BLOCK - 1) // MOE_BLOCK * MOE_BLOCK
    pad_end = jnp.cumsum(padded)
    pad_start = pad_end - padded
    dest = (pad_start[flat_e] + rank).astype(jnp.int32)
    n_blocks = -(-A // MOE_BLOCK) + N_EXPERTS
    P = n_blocks * MOE_BLOCK
    row_tok = jnp.zeros((P,), jnp.int32).at[dest].set(jnp.arange(A, dtype=jnp.int32) // TOP_K)
    starts = jnp.arange(n_blocks, dtype=jnp.int32) * MOE_BLOCK
    block_e = jnp.minimum(jnp.searchsorted(pad_end, starts, side='right'), N_EXPERTS - 1).astype(jnp.int32)
    n_used = (pad_end[-1] // MOE_BLOCK).astype(jnp.int32).reshape(1)
    return dest, row_tok, block_e, n_used


def _hier_moe(hp, hs, norm_g, wr, br, w_gate, w_up, w_down, layer, fnorm, final_norm):
    t_all = math.gcd(hp.shape[0], hs.shape[0])
    xn, gates, route, cnt = _router(hp, hs, norm_g, wr, br, math.gcd(ROUTER_TILE, t_all))
    counts = cnt[0, :N_EXPERTS].astype(jnp.int32)
    dest, row_tok, block_e, n_used = _route_tables(route, counts)
    ys = _experts(xn, block_e, row_tok, n_used, w_gate, w_up, w_down, layer)
    return _combine(ys, dest, hp, hs, gates, fnorm, math.gcd(MOE_BLOCK, t_all), final_norm)


def _rope_tile(x, cos, sin_signed, lane):
    half = DIFF_DH // 2
    rot = jnp.where((lane & (DIFF_DH - 1)) < half,
                    pltpu.roll(x, LANES - half, axis=1),
                    pltpu.roll(x, half, axis=1))
    return x * cos + rot * sin_signed


def _qkv_kernel(h_ref, gq_ref, gkv_ref, wq_ref, wkv_ref, cos_ref, sin_ref, q_ref, k_ref, v_ref):
    x = h_ref[...]
    D = x.shape[1]
    xs = x * lax.rsqrt(jnp.mean(x * x, axis=-1, keepdims=True) + EPS)
    q = jnp.dot((xs * gq_ref[...]).astype(BF16), wq_ref[...], preferred_element_type=F32)
    kv = jnp.dot((xs * gkv_ref[...]).astype(BF16), wkv_ref[...], preferred_element_type=F32)
    cos = cos_ref[...]
    sin = sin_ref[...]
    lane = lax.broadcasted_iota(jnp.int32, cos.shape, 1)
    for hd in range(D // LANES):
        sl = slice(hd * LANES, (hd + 1) * LANES)
        q_ref[:, sl] = _rope_tile(q[:, sl], cos, sin, lane) * (DIFF_DH ** -0.5)
        k_ref[:, sl] = _rope_tile(kv[:, sl], cos, sin, lane)
    v_ref[...] = kv[:, D:]


def _qkv_proj(h, gq, gkv, wq_bf, wkv_bf, cos_t, sin_t, tm):
    T, D = h.shape
    nt = cos_t.shape[0] // tm
    row = pl.BlockSpec((tm, D), lambda i: (i, 0))
    vec = pl.BlockSpec((1, D), lambda i: (0, 0))
    tab = pl.BlockSpec((tm, LANES), lambda i: (i % nt, 0))
    return pl.pallas_call(
        _qkv_kernel,
        grid=(T // tm,),
        in_specs=[row, vec, vec,
                  pl.BlockSpec((D, D), lambda i: (0, 0)),
                  pl.BlockSpec((D, 2 * D), lambda i: (0, 0)),
                  tab, tab],
        out_specs=[row, row, row],
        out_shape=[jax.ShapeDtypeStruct((T, D), F32)] * 3,
        compiler_params=_params("parallel"),
        name="qkv_proj",
    )(h, gq.reshape(1, D), gkv.reshape(1, D), wq_bf, wkv_bf, cos_t, sin_t)


def _rope_tables(pos):
    half = DIFF_DH // 2
    inv = ROPE_THETA ** (-jnp.arange(half, dtype=F32) / half)
    ang = pos.astype(F32)[:, None] * inv[None, :]
    cos = jnp.cos(ang)
    sin = jnp.sin(ang)
    reps = LANES // DIFF_DH
    return (jnp.tile(jnp.concatenate([cos, cos], axis=1), (1, reps)),
            jnp.tile(jnp.concatenate([-sin, sin], axis=1), (1, reps)))


def _lambda(lam_ref, lam_init):
    lp = lam_ref[...]
    a = jnp.sum(lp[0:1, :] * lp[1:2, :], axis=-1, keepdims=True)
    b = jnp.sum(lp[2:3, :] * lp[3:4, :], axis=-1, keepdims=True)
    return jnp.exp(a) - jnp.exp(b) + lam_init


def _attn_prompt_kernel(qi_ref, ki_ref, lam_ref, q_ref, k_ref, v_ref, subln_ref, o_ref,
                        m_ref, l_ref, acc_ref, *, t, lam_init):
    j = pl.program_id(2)
    qi = qi_ref[j]
    ki = ki_ref[j]

    @pl.when(ki == 0)
    def _():
        m_ref[...] = jnp.full(m_ref.shape, -jnp.inf, F32)
        l_ref[...] = jnp.zeros(l_ref.shape, F32)
        acc_ref[...] = jnp.zeros(acc_ref.shape, F32)

    QS, KS = min(ATTN_QSUB, t), min(ATTN_KSUB, t)

    def update(diagonal):
        q = q_ref[0] * LOG2E
        kb = k_ref[0].astype(BF16)
        vT = v_ref[0].T.astype(BF16)
        lane = lax.broadcasted_iota(jnp.int32, q.shape, 1)
        for c in range(2):
            qc = jnp.where((lane < DIFF_DH) if c == 0 else (lane >= DIFF_DH), q, 0.0).astype(BF16)
            for qs in range(t // QS):
                qsub = qc[qs * QS:(qs + 1) * QS]
                m = m_ref[c, qs]
                l = l_ref[c, qs]
                acc = acc_ref[c, qs]
                for ks in range(t // KS):
                    if diagonal and ks * KS > qs * QS + QS - 1:
                        continue
                    sT = _dot_nt(kb[ks * KS:(ks + 1) * KS], qsub)
                    if diagonal and (ks + 1) * KS - 1 > qs * QS:
                        keyi = ks * KS + lax.broadcasted_iota(jnp.int32, (KS, QS), 0)
                        qidx = qs * QS + lax.broadcasted_iota(jnp.int32, (KS, QS), 1)
                        sT = jnp.where(keyi <= qidx, sT, -jnp.inf)
                    m_new = jnp.maximum(m, jnp.max(sT, axis=0, keepdims=True))
                    alpha = jnp.exp2(m - m_new)
                    p = jnp.exp2(sT - m_new)
                    l = alpha * l + jnp.sum(p, axis=0, keepdims=True)
                    acc = alpha * acc + jnp.dot(vT[:, ks * KS:(ks + 1) * KS], p.astype(BF16),
                                                preferred_element_type=F32)
                    m = m_new
                m_ref[c, qs] = m
                l_ref[c, qs] = l
                acc_ref[c, qs] = acc

    @pl.when(ki < qi)
    def _():
        update(False)

    @pl.when(ki == qi)
    def _():
        update(True)
        lam = _lambda(lam_ref, lam_init)
        for qs in range(t // QS):
            oT = acc_ref[0, qs] / l_ref[0, qs] - lam * (acc_ref[1, qs] / l_ref[1, qs])
            o_ref[0, qs * QS:(qs + 1) * QS, :] = _rms(oT.T, subln_ref[...]) * (1.0 - lam_init)


def _attn_prompt(q, k, v, lam_p, subln, lam_init, t):
    B, L, D = q.shape
    H = DIFF_HEADS
    W = D // H
    n = L // t
    pairs = [(qi, ki) for qi in range(n) for ki in range(qi + 1)]
    qi_tab = jnp.asarray(np.array([p[0] for p in pairs], np.int32))
    ki_tab = jnp.asarray(np.array([p[1] for p in pairs], np.int32))
    qs = min(ATTN_QSUB, t)
    assert t % qs == 0 and t % min(ATTN_KSUB, t) == 0
    kern = functools.partial(_attn_prompt_kernel, t=t, lam_init=lam_init)
    grid_spec = pltpu.PrefetchScalarGridSpec(
        num_scalar_prefetch=2,
        grid=(B, H, len(pairs)),
        in_specs=[pl.BlockSpec(lam_p.shape, lambda b, h, j, qt, kt: (0, 0)),
                  pl.BlockSpec((1, t, W), lambda b, h, j, qt, kt: (b, qt[j], h)),
                  pl.BlockSpec((1, t, W), lambda b, h, j, qt, kt: (b, kt[j], h)),
                  pl.BlockSpec((1, t, W), lambda b, h, j, qt, kt: (b, kt[j], h)),
                  pl.BlockSpec((1, W), lambda b, h, j, qt, kt: (0, 0))],
        out_specs=pl.BlockSpec((1, t, W), lambda b, h, j, qt, kt: (b, qt[j], h)),
        scratch_shapes=[pltpu.VMEM((2, t // qs, 1, qs), F32),
                        pltpu.VMEM((2, t // qs, 1, qs), F32),
                        pltpu.VMEM((2, t // qs, W, qs), F32)],
    )
    return pl.pallas_call(
        kern,
        grid_spec=grid_spec,
        out_shape=jax.ShapeDtypeStruct((B, L, D), F32),
        compiler_params=_params("parallel", "parallel", "arbitrary"),
        name="diff_attn_prompt",
    )(qi_tab, ki_tab, lam_p, q, k, v, subln.reshape(1, W))


def _attn_sample_kernel(pt_ref, lam_ref, qx_ref, *rest, G, n_groups, n_new, lam_init):
    del pt_ref
    ck_refs, cv_refs = rest[:G], rest[G:2 * G]
    kn_ref, vn_ref, subln_ref, o_ref, m_ref, l_ref, acc_ref = rest[2 * G:]
    H = DIFF_HEADS
    p = pl.program_id(1)
    R = qx_ref.shape[1]
    qb = qx_ref[0].astype(BF16)

    @pl.when(p == 0)
    def _():
        m_ref[...] = jnp.full(m_ref.shape, -jnp.inf, F32)
        l_ref[...] = jnp.zeros(l_ref.shape, F32)
        acc_ref[...] = jnp.zeros(acc_ref.shape, F32)

    def head_mask(width):
        rowi = lax.broadcasted_iota(jnp.int32, (R, width), 0)
        coli = lax.broadcasted_iota(jnp.int32, (R, width), 1)
        return rowi, coli, (coli & (H - 1)) == (rowi >> 3)

    def update(s_list, v_list):
        m_prev = m_ref[...]
        m_new = m_prev
        for s in s_list:
            m_new = jnp.maximum(m_new, jnp.max(s, axis=-1, keepdims=True))
        alpha = jnp.exp(m_prev - m_new)
        l_new = alpha * l_ref[...]
        acc = alpha * acc_ref[...]
        for s, v in zip(s_list, v_list):
            pr = jnp.exp(s - m_new)
            l_new = l_new + jnp.sum(pr, axis=-1, keepdims=True)
            acc = acc + _dot(pr, v)
        m_ref[...] = m_new
        l_ref[...] = l_new
        acc_ref[...] = acc

    @pl.when(p < n_groups)
    def _():
        _, _, hm = head_mask(ck_refs[0].shape[1])
        update([jnp.where(hm, _dot_nt(qb, r[0]), -jnp.inf) for r in ck_refs], [r[0] for r in cv_refs])

    @pl.when(p == n_groups)
    def _():
        rowi, coli, hm = head_mask(kn_ref.shape[1])
        causal = (coli >> 3) <= (rowi & (n_new - 1))
        s = jnp.where(jnp.logical_and(hm, causal), _dot_nt(qb, kn_ref[0]), -jnp.inf)
        update([s], [vn_ref[0]])
        lam = _lambda(lam_ref, lam_init)
        o8 = acc_ref[...] / l_ref[...]
        for hd in range(H):
            blk = o8[hd * SUBLANES:(hd + 1) * SUBLANES]
            diff = blk - lam * pltpu.roll(blk, SUBLANES // 2, axis=0)
            o_ref[0, hd * SUBLANES:(hd + 1) * SUBLANES, :] = _rms(diff, subln_ref[...]) * (1.0 - lam_init)


def _attn_sample(qx, cache_k, cache_v, page_table, k_new, v_new, lam_p, subln, lam_init, n_new):
    B, R, W = qx.shape
    rows = cache_k.shape[1]
    n_pages = page_table.shape[1]
    G = math.gcd(n_pages, PAGES_PER_STEP)
    n_groups = n_pages // G
    kern = functools.partial(_attn_sample_kernel, G=G, n_groups=n_groups, n_new=n_new, lam_init=lam_init)

    def page_spec(j):
        return pl.BlockSpec((1, rows, W), lambda b, p, pt: (pt[b, jnp.minimum(p, n_groups - 1) * G + j], 0, 0))

    def seq(r):
        return pl.BlockSpec((1, r, W), lambda b, p, pt: (b, 0, 0))

    grid_spec = pltpu.PrefetchScalarGridSpec(
        num_scalar_prefetch=1,
        grid=(B, n_groups + 1),
        in_specs=([pl.BlockSpec(lam_p.shape, lambda b, p, pt: (0, 0)), seq(R)]
                  + [page_spec(j) for j in range(G)] * 2
                  + [seq(k_new.shape[1]), seq(v_new.shape[1]),
                     pl.BlockSpec((1, W), lambda b, p, pt: (0, 0))]),
        out_specs=seq(R),
        scratch_shapes=[pltpu.VMEM((R, 1), F32),
                        pltpu.VMEM((R, 1), F32),
                        pltpu.VMEM((R, W), F32)],
    )
    return pl.pallas_call(
        kern,
        grid_spec=grid_spec,
        out_shape=jax.ShapeDtypeStruct((B, R, W), F32),
        compiler_params=_params("parallel", "arbitrary"),
        name="diff_attn_sample",
    )(page_table, lam_p, qx, *([cache_k] * G), *([cache_v] * G), k_new, v_new, subln.reshape(1, W))


def _lambda_init(layer):
    return 0.8 - 0.6 * math.exp(-0.3 * layer)


def _pad_rows(x, rows):
    return jnp.pad(x, [(0, 0), (0, rows - x.shape[1])] + [(0, 0)] * (x.ndim - 2))


def kernel(x_prompt, x_sample, state_delta, state_conv, cache_k, cache_v, page_table, norm_a, w_in_a, conv_w_a, a_log, dt_bias, onorm_a, w_out_a, kv_norm, w_kv, norm_b, w_q_b, lam_b, subln_b, w_out_b, norm_m, w_rg, b_rg, w_re, b_re, w_gate_e, w_up_e, w_down_e, final_norm):
    D = x_prompt.shape[-1]
    H = GDN_HEADS
    CH = 3 * H * GDN_DK
    NH, W = DIFF_HEADS, 2 * DIFF_DH
    n_in = w_in_a.shape[-1]
    n_in_pad = -(-n_in // LANES) * LANES
    depth = norm_m.shape[0]
    n_a = norm_a.shape[0]
    assert depth == 2 and n_a == 1 and norm_b.shape[0] == 1

    w_in_bf = jnp.pad(w_in_a[0], ((0, 0), (0, n_in_pad - n_in))).astype(BF16)
    w_out_a_bf = w_out_a[0].astype(BF16)
    w_q_bf = w_q_b[0].astype(BF16)
    w_kv_bf = w_kv.astype(BF16)
    w_out_b_bf = w_out_b[0].astype(BF16)
    n_route = N_GROUPS + N_EXPERTS
    wr = jnp.pad(jnp.concatenate([w_rg, w_re], axis=-1), ((0, 0), (0, 0), (0, LANES - n_route)))
    br = jnp.pad(jnp.concatenate([b_rg, b_re], axis=-1), ((0, 0), (0, LANES - n_route))).reshape(depth, 1, LANES)
    alog_l = jnp.zeros((1, LANES), F32).at[0, H:2 * H].set(a_log[0])
    dtb_l = jnp.zeros((1, LANES), F32).at[0, H:2 * H].set(dt_bias[0])

    def gdn_layer(h, B, L, conv0, s0):
        T = B * L
        tm = min(256, T)
        proj = _norm_matmul(h, norm_a[0], w_in_bf, tm)
        C = min(GDN_CHUNK, -(-L // SUBLANES) * SUBLANES)
        Lp = -(-L // C) * C
        valid_last = L - (Lp - C)
        proj3 = _pad_rows(proj.reshape(B, L, n_in_pad), Lp)
        conv0p = jnp.pad(conv0, ((0, 0), (SUBLANES - (CONV_W - 1), 0), (0, 0)))
        bb = 1 if Lp > C else 2
        o3, s_new, cbuf = _gdn_mixer(proj3, conv0p, s0, conv_w_a[0], alog_l, dtb_l, onorm_a[0], C, valid_last, bb)
        h = _matmul_residual(o3[:, :L].reshape(T, D), w_out_a_bf, h, tm)
        return h, s_new[None], cbuf[:, SUBLANES - (CONV_W - 1):][None]

    def attn_layer(h, B, L, past_len, paged):
        T = B * L
        tm = min(256, T)
        pos = past_len + jnp.arange(L, dtype=jnp.int32)
        cos_t, sin_t = _rope_tables(pos)
        if L % tm != 0:
            cos_t = jnp.tile(cos_t, (T // L, 1))
            sin_t = jnp.tile(sin_t, (T // L, 1))
        q, k, v = _qkv_proj(h, norm_b[0], kv_norm, w_q_bf, w_kv_bf, cos_t, sin_t, tm)
        lam_init = _lambda_init(n_a)
        if paged is None:
            o = _attn_prompt(q.reshape(B, L, D), k.reshape(B, L, D), v.reshape(B, L, D),
                             lam_b[0], subln_b[0], lam_init, min(ATTN_TILE, L))
            o = o.reshape(T, D)
        else:
            ck, cv, pt = paged
            assert 2 * L == SUBLANES and NH == SUBLANES
            q5 = q.reshape(B, L, NH, 2, DIFF_DH).transpose(0, 2, 1, 3, 4)
            sel = jnp.eye(2, dtype=F32)[None, None, :, None, :, None]
            qx = (q5[:, :, None] * sel).reshape(B, NH * 2 * L, W)
            o = _attn_sample(qx, ck, cv, pt, k.reshape(B, L * NH, W), v.reshape(B, L * NH, W),
                             lam_b[0], subln_b[0], lam_init, L)
            o = o.reshape(B, NH, 2 * L, W)[:, :, :L].transpose(0, 2, 1, 3).reshape(T, D)
        h = _matmul_residual(o, w_out_b_bf, h, tm)
        kshape = (B, L, NH, W)
        return h, k.reshape(kshape), v.reshape(kshape)

    def moe(hp, hs, layer, last):
        return _hier_moe(hp, hs, norm_m[layer], wr[layer], br[layer], w_gate_e, w_up_e, w_down_e, layer,
                         final_norm, last)

    Bp, Lq, _ = x_prompt.shape
    Bs, Ls, _ = x_sample.shape
    hp = x_prompt.reshape(Bp * Lq, D)
    hs = x_sample.reshape(Bs * Ls, D)

    hp, sd_p, sc_p = gdn_layer(hp, Bp, Lq, jnp.zeros((Bp, CONV_W - 1, CH), F32),
                               jnp.zeros((Bp, H, GDN_DK, GDN_DK), F32))
    hs, sd_s, sc_s = gdn_layer(hs, Bs, Ls, state_conv[0], state_delta[0])
    hp, hs = moe(hp, hs, 0, False)

    n_phys, page, nh, w = cache_k.shape
    past_len = page_table.shape[1] * page
    ck = cache_k.reshape(n_phys, page * nh, w)
    cv = cache_v.reshape(n_phys, page * nh, w)
    hp, k_p, v_p = attn_layer(hp, Bp, Lq, 0, None)
    hs, k_s, v_s = attn_layer(hs, Bs, Ls, past_len, (ck, cv, page_table))
    y_p, y_s = moe(hp, hs, 1, True)
    return (y_p.reshape(Bp, Lq, D), y_s.reshape(Bs, Ls, D), sd_p, sc_p, k_p, v_p, sd_s, sc_s, k_s, v_s)
```

```python
import functools
import math

import numpy as np
import jax
import jax.numpy as jnp
from jax import lax
from jax.experimental import pallas as pl
from jax.experimental.pallas import tpu as pltpu

F32 = jnp.float32
BF16 = jnp.bfloat16
EPS = 1e-6
LANES = 128
SUBLANES = 8
VMEM_LIMIT = 56 * 1024 * 1024

CONV_W = 4
GDN_HEADS = 8
GDN_DK = 128
GDN_CHUNK = 64
DIFF_HEADS = 8
DIFF_DH = 64
ROPE_THETA = 10000.0
N_GROUPS = 4
EXP_PER_GROUP = 8
N_EXPERTS = N_GROUPS * EXP_PER_GROUP
TOP_K = 2
MOE_BLOCK = 128
ROUTE_OFF = N_GROUPS
ATTN_TILE = 512
PAGES_PER_STEP = 16
ATTN_QSUB = LANES
ATTN_KSUB = 256
LOG2E = 1.4426950408889634
DMA_UNROLL = 8
ROUTER_TILE = 256


def _params(*sem):
    return pltpu.CompilerParams(dimension_semantics=sem, vmem_limit_bytes=VMEM_LIMIT)


def _dot(a, b):
    return jnp.dot(a.astype(BF16), b.astype(BF16), preferred_element_type=F32)


def _dot_nt(a, b):
    return lax.dot_general(a.astype(BF16), b.astype(BF16), (((1,), (1,)), ((), ())),
                           preferred_element_type=F32)


def _dot_tn(a, b):
    return lax.dot_general(a.astype(BF16), b.astype(BF16), (((0,), (0,)), ((), ())),
                           preferred_element_type=F32)


def _split2(x):
    hi = x.astype(BF16)
    lo = (x - hi.astype(F32)).astype(BF16)
    return hi, lo


def _dot3(a, b):
    ah, al = _split2(a)
    bh, bl = _split2(b)
    d = functools.partial(jnp.dot, preferred_element_type=F32)
    return d(ah, bh) + (d(ah, bl) + d(al, bh))


def _dot_exact_lhs(a_bf, b):
    b1 = b.astype(BF16)
    r1 = b - b1.astype(F32)
    b2 = r1.astype(BF16)
    b3 = (r1 - b2.astype(F32)).astype(BF16)
    d = functools.partial(jnp.dot, preferred_element_type=F32)
    return d(a_bf, b1) + (d(a_bf, b2) + d(a_bf, b3))


def _sigmoid(x):
    return 1.0 / (1.0 + jnp.exp(-x))


def _silu(x):
    return x * _sigmoid(x)


def _softplus(x):
    return jnp.maximum(x, 0.0) + jnp.log1p(jnp.exp(-jnp.abs(x)))


def _rms(x, g):
    return x * lax.rsqrt(jnp.mean(x * x, axis=-1, keepdims=True) + EPS) * g


def _norm_mm_kernel(x_ref, g_ref, w_ref, o_ref):
    xn = _rms(x_ref[...], g_ref[...])
    o_ref[...] = jnp.dot(xn.astype(BF16), w_ref[...], preferred_element_type=F32)


def _norm_matmul(x, g, w_bf, tm):
    T, D = x.shape
    N = w_bf.shape[1]
    return pl.pallas_call(
        _norm_mm_kernel,
        grid=(T // tm,),
        in_specs=[pl.BlockSpec((tm, D), lambda i: (i, 0)),
                  pl.BlockSpec((1, D), lambda i: (0, 0)),
                  pl.BlockSpec((D, N), lambda i: (0, 0))],
        out_specs=pl.BlockSpec((tm, N), lambda i: (i, 0)),
        out_shape=jax.ShapeDtypeStruct((T, N), F32),
        compiler_params=_params("parallel"),
        name="norm_matmul",
    )(x, g.reshape(1, D), w_bf)


def _mm_res_kernel(x_ref, w_ref, h_ref, o_ref):
    o_ref[...] = h_ref[...] + jnp.dot(x_ref[...].astype(BF16), w_ref[...], preferred_element_type=F32)


def _matmul_residual(x, w_bf, h, tm):
    T, K = x.shape
    N = w_bf.shape[1]
    return pl.pallas_call(
        _mm_res_kernel,
        grid=(T // tm,),
        in_specs=[pl.BlockSpec((tm, K), lambda i: (i, 0)),
                  pl.BlockSpec((K, N), lambda i: (0, 0)),
                  pl.BlockSpec((tm, N), lambda i: (i, 0))],
        out_specs=pl.BlockSpec((tm, N), lambda i: (i, 0)),
        out_shape=jax.ShapeDtypeStruct((T, N), F32),
        compiler_params=_params("parallel"),
        name="matmul_residual",
    )(x, w_bf, h)


def _gdn_kernel(proj_ref, conv0_ref, s0_ref, cw_ref, alog_ref, dtb_ref, onorm_ref,
                o_ref, sout_ref, cout_ref, ext_ref, s_ref, *, C, nc, valid_last, bb):
    H, DK = GDN_HEADS, GDN_DK
    QK = H * DK
    CH = 3 * QK
    c = pl.program_id(1)
    masked = valid_last < C

    @pl.when(c == 0)
    def _():
        ext_ref[:, 0:SUBLANES, :] = conv0_ref[...]
        s_ref[...] = s0_ref[...]

    row1 = lax.broadcasted_iota(jnp.int32, (C, 1), 0)
    row = lax.broadcasted_iota(jnp.int32, (C, C), 0)
    col = lax.broadcasted_iota(jnp.int32, (C, C), 1)
    incl = row >= col
    strict = row > col
    eye = jnp.where(row == col, 1.0, 0.0).astype(F32)
    tril_bf = jnp.where(incl, 1.0, 0.0).astype(BF16)
    if masked:
        valid = jnp.logical_or(row1 < valid_last, c < nc - 1)

    seqs = []
    for b in range(bb):
        xin = proj_ref[b, :, 0:CH]
        ext_ref[b, SUBLANES:SUBLANES + C, :] = xin
        base = SUBLANES - (CONV_W - 1)
        acc = ext_ref[b, base:base + C, :] * cw_ref[0:1, :]
        for i in range(1, CONV_W - 1):
            acc = acc + ext_ref[b, base + i:base + i + C, :] * cw_ref[i:i + 1, :]
        acc = acc + xin * cw_ref[CONV_W - 1:CONV_W, :]
        qkv = _silu(acc)

        @pl.when(c == nc - 1)
        def _():
            cout_ref[b] = ext_ref[b, valid_last:valid_last + SUBLANES, :]

        ext_ref[b, 0:SUBLANES, :] = ext_ref[b, C:C + SUBLANES, :]

        ba = proj_ref[b, :, CH + QK:CH + QK + LANES]
        beta_t = _sigmoid(ba)
        g_t = -jnp.exp(alog_ref[...]) * _softplus(ba + dtb_ref[...])
        if masked:
            beta_t = jnp.where(valid, beta_t, 0.0)
            g_t = jnp.where(valid, g_t, 0.0)
        G = _dot_exact_lhs(tril_bf, g_t)
        g_last = G[C - 1:C, :]
        seqs.append(dict(qkv=qkv, beta=beta_t, G=G, GT=G.T, expG=jnp.exp(G),
                         exp_last=jnp.exp(g_last), k_scale=jnp.exp(g_last - G)))

    chains = [(b, hd) for b in range(bb) for hd in range(H)]
    st = []
    for b, hd in chains:
        sq = seqs[b]
        q = sq["qkv"][:, hd * DK:(hd + 1) * DK]
        k = sq["qkv"][:, QK + hd * DK:QK + (hd + 1) * DK]
        v = sq["qkv"][:, 2 * QK + hd * DK:2 * QK + (hd + 1) * DK]
        q = q * lax.rsqrt(jnp.sum(q * q, axis=-1, keepdims=True) + EPS) * (DK ** -0.5)
        k = k * lax.rsqrt(jnp.sum(k * k, axis=-1, keepdims=True) + EPS)
        if masked:
            k = jnp.where(valid, k, 0.0)
        bcol = sq["beta"][:, hd:hd + 1]
        gcol = sq["G"][:, H + hd:H + hd + 1]
        grow = sq["GT"][H + hd:H + hd + 1, :]
        eg = sq["expG"][:, H + hd:H + hd + 1]
        decay = jnp.exp(jnp.where(incl, gcol - grow, -jnp.inf))
        st.append(dict(q=q, k=k, decay=decay, eg=eg, bcol=bcol,
                       rhs=jnp.concatenate([bcol * v, (bcol * eg) * k], axis=1)))
    for d in st:
        kk = _dot_nt(d["k"], d["k"])
        d["p"] = -jnp.where(strict, d["bcol"] * kk * d["decay"], 0.0)
        d["t"] = eye + d["p"]
    for _ in range(int(math.log2(C)) - 1):
        for d in st:
            d["p"] = _dot3(d["p"], d["p"])
        for d in st:
            d["t"] = d["t"] + _dot3(d["t"], d["p"])
    for d in st:
        d["sol"] = _dot3(d["t"], d["rhs"])
    for d, (b, hd) in zip(st, chains):
        d["S"] = s_ref[b, hd]
        d["u"] = d["sol"][:, :DK] - _dot(d["sol"][:, DK:], d["S"])
    for d in st:
        d["qk"] = _dot_nt(d["q"], d["k"]) * d["decay"]
    for d in st:
        d["o"] = _dot(d["q"] * d["eg"], d["S"]) + _dot(d["qk"], d["u"])
    for d, (b, hd) in zip(st, chains):
        sq = seqs[b]
        s_ref[b, hd] = (sq["exp_last"][:, H + hd:H + hd + 1] * d["S"]
                        + _dot_tn(d["k"] * sq["k_scale"][:, H + hd:H + hd + 1], d["u"]))
    for d, (b, hd) in zip(st, chains):
        z = proj_ref[b, :, CH + hd * DK:CH + (hd + 1) * DK]
        o_ref[b, :, hd * DK:(hd + 1) * DK] = _rms(d["o"], onorm_ref[...]) * _silu(z)

    @pl.when(c == nc - 1)
    def _():
        sout_ref[...] = s_ref[...]


def _gdn_mixer(proj, conv0p, s0, conv_w, alog_l, dtb_l, onorm, C, valid_last, bb):
    B, Lp, NP = proj.shape
    nc = Lp // C
    H, DK = GDN_HEADS, GDN_DK
    D = H * DK
    CH = 3 * H * DK
    kern = functools.partial(_gdn_kernel, C=C, nc=nc, valid_last=valid_last, bb=bb)
    return pl.pallas_call(
        kern,
        grid=(B // bb, nc),
        in_specs=[pl.BlockSpec((bb, C, NP), lambda b, c: (b, c, 0)),
                  pl.BlockSpec((bb, SUBLANES, CH), lambda b, c: (b, 0, 0)),
                  pl.BlockSpec((bb, H, DK, DK), lambda b, c: (b, 0, 0, 0)),
                  pl.BlockSpec((CONV_W, CH), lambda b, c: (0, 0)),
                  pl.BlockSpec((1, LANES), lambda b, c: (0, 0)),
                  pl.BlockSpec((1, LANES), lambda b, c: (0, 0)),
                  pl.BlockSpec((1, DK), lambda b, c: (0, 0))],
        out_specs=[pl.BlockSpec((bb, C, D), lambda b, c: (b, c, 0)),
                   pl.BlockSpec((bb, H, DK, DK), lambda b, c: (b, 0, 0, 0)),
                   pl.BlockSpec((bb, SUBLANES, CH), lambda b, c: (b, 0, 0))],
        out_shape=[jax.ShapeDtypeStruct((B, Lp, D), F32),
                   jax.ShapeDtypeStruct((B, H, DK, DK), F32),
                   jax.ShapeDtypeStruct((B, SUBLANES, CH), F32)],
        scratch_shapes=[pltpu.VMEM((bb, SUBLANES + C, CH), F32),
                        pltpu.VMEM((bb, H, DK, DK), F32)],
        compiler_params=_params("parallel", "arbitrary"),
        name="gdn_mixer",
    )(proj, conv0p, s0, conv_w, alog_l, dtb_l, onorm.reshape(1, DK))


def _router_kernel(hp_ref, hs_ref, g_ref, wr_ref, br_ref, xn_ref, gate_ref, route_ref, cnt_ref, carry_ref,
                   *, n_p):
    i = pl.program_id(0)

    @pl.when(i == 0)
    def _():
        carry_ref[...] = jnp.zeros(carry_ref.shape, F32)

    x = jnp.where(i < n_p, hp_ref[...], hs_ref[...])
    xn = _rms(x, g_ref[...])
    xn_ref[...] = xn
    logits = _dot3(xn, wr_ref[...]) + br_ref[...]
    tm = logits.shape[0]
    lane = lax.broadcasted_iota(jnp.int32, (tm, LANES), 1).astype(F32)
    neg = -jnp.inf
    big = float(LANES)

    lg = jnp.where(lane < N_GROUPS, logits, neg)
    eg = jnp.exp(lg - jnp.max(lg, axis=-1, keepdims=True))
    pg = eg / jnp.sum(eg, axis=-1, keepdims=True)
    pmax = jnp.max(pg, axis=-1, keepdims=True)
    g_sel = jnp.min(jnp.where(pg == pmax, lane, big), axis=-1, keepdims=True)

    lo = ROUTE_OFF + EXP_PER_GROUP * g_sel
    in_grp = jnp.logical_and(lane >= lo, lane < lo + EXP_PER_GROUP)
    le = jnp.where(in_grp, logits, neg)
    ee = jnp.exp(le - jnp.max(le, axis=-1, keepdims=True))
    pe = ee / jnp.sum(ee, axis=-1, keepdims=True)
    pe = jnp.where(in_grp, pe, -1.0)
    p1 = jnp.max(pe, axis=-1, keepdims=True)
    i1 = jnp.min(jnp.where(pe == p1, lane, big), axis=-1, keepdims=True)
    pe2 = jnp.where(lane == i1, -1.0, pe)
    p2 = jnp.max(pe2, axis=-1, keepdims=True)
    i2 = jnp.min(jnp.where(pe2 == p2, lane, big), axis=-1, keepdims=True)
    den = p1 + p2
    gate_ref[...] = jnp.where(lane == 0, pmax * p1 / den, jnp.where(lane == 1, pmax * p2 / den, 0.0))

    e1 = i1 - ROUTE_OFF
    e2 = i2 - ROUTE_OFF
    oh1 = jnp.where(lane == e1, 1.0, 0.0)
    oh2 = jnp.where(lane == e2, 1.0, 0.0)
    both = oh1 + oh2
    earlier = (lax.broadcasted_iota(jnp.int32, (tm, tm), 0) > lax.broadcasted_iota(jnp.int32, (tm, tm), 1))
    before = (jnp.dot(jnp.where(earlier, 1.0, 0.0).astype(BF16), both.astype(BF16), preferred_element_type=F32)
              + carry_ref[0:1, :])
    r1 = jnp.sum(oh1 * before, axis=-1, keepdims=True)
    r2 = jnp.sum(oh2 * before, axis=-1, keepdims=True)
    carry_ref[...] = carry_ref[...] + jnp.sum(both, axis=0, keepdims=True)
    cnt_ref[...] = carry_ref[...]
    route_ref[...] = jnp.where(lane == 0, e1, jnp.where(lane == 1, e2, jnp.where(
        lane == 2, r1, jnp.where(lane == 3, r2, 0.0)))).astype(jnp.int32)


def _router(hp, hs, g, wr, br, tm):
    Tp, D = hp.shape
    Ts = hs.shape[0]
    n_p, n_s = Tp // tm, Ts // tm
    T = Tp + Ts
    kern = functools.partial(_router_kernel, n_p=n_p)
    row = pl.BlockSpec((tm, D), lambda i: (i, 0))
    lanes = pl.BlockSpec((tm, LANES), lambda i: (i, 0))
    return pl.pallas_call(
        kern,
        grid=(n_p + n_s,),
        in_specs=[pl.BlockSpec((tm, D), lambda i: (jnp.minimum(i, n_p - 1), 0)),
                  pl.BlockSpec((tm, D), lambda i: (jnp.maximum(i - n_p, 0), 0)),
                  pl.BlockSpec((1, D), lambda i: (0, 0)),
                  pl.BlockSpec((D, LANES), lambda i: (0, 0)),
                  pl.BlockSpec((1, LANES), lambda i: (0, 0))],
        out_specs=[row, lanes, lanes, pl.BlockSpec((SUBLANES, LANES), lambda i: (0, 0))],
        out_shape=[jax.ShapeDtypeStruct((T, D), F32),
                   jax.ShapeDtypeStruct((T, LANES), F32),
                   jax.ShapeDtypeStruct((T, LANES), jnp.int32),
                   jax.ShapeDtypeStruct((SUBLANES, LANES), F32)],
        scratch_shapes=[pltpu.VMEM((SUBLANES, LANES), F32)],
        compiler_params=_params("arbitrary"),
        name="moe_router",
    )(hp, hs, g.reshape(1, D), wr, br)


def _row_copy(src_hbm, row, dst, slot_idx, sem):
    return pltpu.make_async_copy(src_hbm.at[pl.ds(row, 1)], dst.at[slot_idx], sem)


def _expert_kernel(be_ref, rt_ref, nu_ref, xn_hbm, wg_ref, wu_ref, wd_ref, ys_ref,
                   xbuf, wg_bf, wu_bf, wd_bf, sem):
    i = pl.program_id(0)
    n_used = nu_ref[0]

    def issue(blk, slot):
        for r in range(MOE_BLOCK):
            _row_copy(xn_hbm, rt_ref[blk * MOE_BLOCK + r], xbuf, (slot, pl.ds(r, 1)), sem.at[slot]).start()

    def wait(slot):
        for r in range(MOE_BLOCK):
            _row_copy(xn_hbm, 0, xbuf, (slot, pl.ds(r, 1)), sem.at[slot]).wait()

    @pl.when(i == 0)
    def _():
        issue(0, 0)

    @pl.when(jnp.logical_or(i == 0, be_ref[i] != be_ref[jnp.maximum(i - 1, 0)]))
    def _():
        wg_bf[...] = wg_ref[0, 0].astype(BF16)
        wu_bf[...] = wu_ref[0, 0].astype(BF16)
        wd_bf[...] = wd_ref[0, 0].astype(BF16)

    @pl.when(i < n_used)
    def _():
        slot = i % 2
        wait(slot)
        issue(jnp.minimum(i + 1, n_used - 1), 1 - slot)
        x = xbuf[slot].astype(BF16)
        gt = jnp.dot(x, wg_bf[...], preferred_element_type=F32)
        up = jnp.dot(x, wu_bf[...], preferred_element_type=F32)
        hid = _silu(gt) * up
        ys_ref[...] = jnp.dot(hid.astype(BF16), wd_bf[...], preferred_element_type=F32)

    @pl.when(i == n_used - 1)
    def _():
        wait(1 - i % 2)

    @pl.when(i >= n_used)
    def _():
        ys_ref[...] = jnp.zeros(ys_ref.shape, F32)


def _experts(xn, block_e, row_tok, n_used, w_gate, w_up, w_down, layer):
    T, D = xn.shape
    FF = w_gate.shape[-1]
    n_blocks = block_e.shape[0]
    P = n_blocks * MOE_BLOCK
    grid_spec = pltpu.PrefetchScalarGridSpec(
        num_scalar_prefetch=3,
        grid=(n_blocks,),
        in_specs=[pl.BlockSpec(memory_space=pl.ANY),
                  pl.BlockSpec((1, 1, D, FF), lambda i, be, rt, nu: (layer, be[i], 0, 0)),
                  pl.BlockSpec((1, 1, D, FF), lambda i, be, rt, nu: (layer, be[i], 0, 0)),
                  pl.BlockSpec((1, 1, FF, D), lambda i, be, rt, nu: (layer, be[i], 0, 0))],
        out_specs=pl.BlockSpec((MOE_BLOCK, D), lambda i, be, rt, nu: (i, 0)),
        scratch_shapes=[pltpu.VMEM((2, MOE_BLOCK, D), F32),
                        pltpu.VMEM((D, FF), BF16),
                        pltpu.VMEM((D, FF), BF16),
                        pltpu.VMEM((FF, D), BF16),
                        pltpu.SemaphoreType.DMA((2,))],
    )
    return pl.pallas_call(
        _expert_kernel,
        grid_spec=grid_spec,
        out_shape=jax.ShapeDtypeStruct((P, D), F32),
        compiler_params=_params("arbitrary"),
        name="moe_experts",
    )(block_e, row_tok, n_used, xn, w_gate, w_up, w_down)


def _combine_kernel(dest_ref, ys_hbm, hp_ref, hs_ref, gate_ref, fn_ref, op_ref, os_ref, ybuf, sem,
                    *, tm, n_p, final_norm):
    i = pl.program_id(0)
    n = pl.num_programs(0)

    def issue(step, slot):
        def body(r, carry):
            for kk in range(TOP_K):
                d = dest_ref[(step * tm + r) * TOP_K + kk]
                _row_copy(ys_hbm, d, ybuf, (slot, kk, pl.ds(r, 1)), sem.at[slot]).start()
            return carry
        lax.fori_loop(0, tm, body, 0, unroll=DMA_UNROLL // TOP_K)

    def wait(slot):
        for r in range(tm):
            for kk in range(TOP_K):
                _row_copy(ys_hbm, 0, ybuf, (slot, kk, pl.ds(r, 1)), sem.at[slot]).wait()

    @pl.when(i == 0)
    def _():
        issue(0, 0)

    @pl.when(i + 1 < n)
    def _():
        issue(i + 1, (i + 1) % 2)

    slot = i % 2
    wait(slot)
    gate = gate_ref[...]
    moe = ybuf[slot, 0] * gate[:, 0:1] + ybuf[slot, 1] * gate[:, 1:2]

    def finish(h_ref, o_ref):
        y = h_ref[...] + moe
        if final_norm:
            y = _rms(y, fn_ref[...])
        o_ref[...] = y

    @pl.when(i < n_p)
    def _():
        finish(hp_ref, op_ref)

    @pl.when(i >= n_p)
    def _():
        finish(hs_ref, os_ref)


def _combine(ys, dest, hp, hs, gates, fnorm, tm, final_norm):
    Tp, D = hp.shape
    Ts = hs.shape[0]
    n_p, n_s = Tp // tm, Ts // tm
    kern = functools.partial(_combine_kernel, tm=tm, n_p=n_p, final_norm=final_norm)
    p_blk = pl.BlockSpec((tm, D), lambda i, d: (jnp.minimum(i, n_p - 1), 0))
    s_blk = pl.BlockSpec((tm, D), lambda i, d: (jnp.maximum(i - n_p, 0), 0))
    grid_spec = pltpu.PrefetchScalarGridSpec(
        num_scalar_prefetch=1,
        grid=(n_p + n_s,),
        in_specs=[pl.BlockSpec(memory_space=pl.ANY), p_blk, s_blk,
                  pl.BlockSpec((tm, LANES), lambda i, d: (i, 0)),
                  pl.BlockSpec((1, D), lambda i, d: (0, 0))],
        out_specs=[p_blk, s_blk],
        scratch_shapes=[pltpu.VMEM((2, TOP_K, tm, D), F32),
                        pltpu.SemaphoreType.DMA((2,))],
    )
    return pl.pallas_call(
        kern,
        grid_spec=grid_spec,
        out_shape=[jax.ShapeDtypeStruct((Tp, D), F32), jax.ShapeDtypeStruct((Ts, D), F32)],
        compiler_params=_params("arbitrary"),
        name="moe_combine",
    )(dest, ys, hp, hs, gates, fnorm.reshape(1, D))


def _route_tables(route, counts):
    T = route.shape[0]
    A = T * TOP_K
    flat_e = route[:, 0:TOP_K].reshape(A)
    rank = route[:, TOP_K:2 * TOP_K].reshape(A)
    padded = (counts + MOE_BLOCK - 1) // MOE_BLOCK * MOE_BLOCK
    pad_end = jnp.cumsum(padded)
    pad_start = pad_end - padded
    dest = (pad_start[flat_e] + rank).astype(jnp.int32)
    n_blocks = -(-A // MOE_BLOCK) + N_EXPERTS
    P = n_blocks * MOE_BLOCK
    row_tok = jnp.zeros((P,), jnp.int32).at[dest].set(jnp.arange(A, dtype=jnp.int32) // TOP_K)
    starts = jnp.arange(n_blocks, dtype=jnp.int32) * MOE_BLOCK
    block_e = jnp.minimum(jnp.sum((pad_end[None, :] <= starts[:, None]).astype(jnp.int32), axis=1),
                          N_EXPERTS - 1)
    n_used = (pad_end[-1] // MOE_BLOCK).astype(jnp.int32).reshape(1)
    return dest, row_tok, block_e, n_used


def _hier_moe(hp, hs, norm_g, wr, br, w_gate, w_up, w_down, layer, fnorm, final_norm):
    t_all = math.gcd(hp.shape[0], hs.shape[0])
    xn, gates, route, cnt = _router(hp, hs, norm_g, wr, br, math.gcd(ROUTER_TILE, t_all))
    counts = cnt[0, :N_EXPERTS].astype(jnp.int32)
    dest, row_tok, block_e, n_used = _route_tables(route, counts)
    ys = _experts(xn, block_e, row_tok, n_used, w_gate, w_up, w_down, layer)
    return _combine(ys, dest, hp, hs, gates, fnorm, math.gcd(MOE_BLOCK, t_all), final_norm)


def _rope_tile(x, cos, sin_signed, lane):
    half = DIFF_DH // 2
    rot = jnp.where((lane & (DIFF_DH - 1)) < half,
                    pltpu.roll(x, LANES - half, axis=1),
                    pltpu.roll(x, half, axis=1))
    return x * cos + rot * sin_signed


def _qkv_kernel(h_ref, gq_ref, gkv_ref, wq_ref, wkv_ref, cos_ref, sin_ref, q_ref, k_ref, v_ref):
    x = h_ref[...]
    D = x.shape[1]
    xs = x * lax.rsqrt(jnp.mean(x * x, axis=-1, keepdims=True) + EPS)
    q = jnp.dot((xs * gq_ref[...]).astype(BF16), wq_ref[...], preferred_element_type=F32)
    kv = jnp.dot((xs * gkv_ref[...]).astype(BF16), wkv_ref[...], preferred_element_type=F32)
    cos = cos_ref[...]
    sin = sin_ref[...]
    lane = lax.broadcasted_iota(jnp.int32, cos.shape, 1)
    for hd in range(D // LANES):
        sl = slice(hd * LANES, (hd + 1) * LANES)
        q_ref[:, sl] = _rope_tile(q[:, sl], cos, sin, lane) * (DIFF_DH ** -0.5)
        k_ref[:, sl] = _rope_tile(kv[:, sl], cos, sin, lane)
    v_ref[...] = kv[:, D:]


def _qkv_proj(h, gq, gkv, wq_bf, wkv_bf, cos_t, sin_t, tm):
    T, D = h.shape
    nt = cos_t.shape[0] // tm
    row = pl.BlockSpec((tm, D), lambda i: (i, 0))
    vec = pl.BlockSpec((1, D), lambda i: (0, 0))
    tab = pl.BlockSpec((tm, LANES), lambda i: (i % nt, 0))
    return pl.pallas_call(
        _qkv_kernel,
        grid=(T // tm,),
        in_specs=[row, vec, vec,
                  pl.BlockSpec((D, D), lambda i: (0, 0)),
                  pl.BlockSpec((D, 2 * D), lambda i: (0, 0)),
                  tab, tab],
        out_specs=[row, row, row],
        out_shape=[jax.ShapeDtypeStruct((T, D), F32)] * 3,
        compiler_params=_params("parallel"),
        name="qkv_proj",
    )(h, gq.reshape(1, D), gkv.reshape(1, D), wq_bf, wkv_bf, cos_t, sin_t)


def _rope_tables(pos):
    half = DIFF_DH // 2
    inv = ROPE_THETA ** (-jnp.arange(half, dtype=F32) / half)
    ang = pos.astype(F32)[:, None] * inv[None, :]
    cos = jnp.cos(ang)
    sin = jnp.sin(ang)
    reps = LANES // DIFF_DH
    return (jnp.tile(jnp.concatenate([cos, cos], axis=1), (1, reps)),
            jnp.tile(jnp.concatenate([-sin, sin], axis=1), (1, reps)))


def _lambda(lam_ref, lam_init):
    lp = lam_ref[...]
    a = jnp.sum(lp[0:1, :] * lp[1:2, :], axis=-1, keepdims=True)
    b = jnp.sum(lp[2:3, :] * lp[3:4, :], axis=-1, keepdims=True)
    return jnp.exp(a) - jnp.exp(b) + lam_init


def _attn_prompt_kernel(qi_ref, ki_ref, lam_ref, q_ref, k_ref, v_ref, subln_ref, o_ref,
                        m_ref, l_ref, acc_ref, *, t, lam_init):
    j = pl.program_id(2)
    qi = qi_ref[j]
    ki = ki_ref[j]

    @pl.when(ki == 0)
    def _():
        m_ref[...] = jnp.full(m_ref.shape, -jnp.inf, F32)
        l_ref[...] = jnp.zeros(l_ref.shape, F32)
        acc_ref[...] = jnp.zeros(acc_ref.shape, F32)

    QS, KS = min(ATTN_QSUB, t), min(ATTN_KSUB, t)

    def update(diagonal):
        q = q_ref[0] * LOG2E
        kb = k_ref[0].astype(BF16)
        vT = v_ref[0].T.astype(BF16)
        lane = lax.broadcasted_iota(jnp.int32, q.shape, 1)
        for c in range(2):
            qc = jnp.where((lane < DIFF_DH) if c == 0 else (lane >= DIFF_DH), q, 0.0).astype(BF16)
            for qs in range(t // QS):
                qsub = qc[qs * QS:(qs + 1) * QS]
                m = m_ref[c, qs]
                l = l_ref[c, qs]
                acc = acc_ref[c, qs]
                for ks in range(t // KS):
                    if diagonal and ks * KS > qs * QS + QS - 1:
                        continue
                    sT = _dot_nt(kb[ks * KS:(ks + 1) * KS], qsub)
                    if diagonal and (ks + 1) * KS - 1 > qs * QS:
                        keyi = ks * KS + lax.broadcasted_iota(jnp.int32, (KS, QS), 0)
                        qidx = qs * QS + lax.broadcasted_iota(jnp.int32, (KS, QS), 1)
                        sT = jnp.where(keyi <= qidx, sT, -jnp.inf)
                    m_new = jnp.maximum(m, jnp.max(sT, axis=0, keepdims=True))
                    alpha = jnp.exp2(m - m_new)
                    p = jnp.exp2(sT - m_new)
                    l = alpha * l + jnp.sum(p, axis=0, keepdims=True)
                    acc = alpha * acc + jnp.dot(vT[:, ks * KS:(ks + 1) * KS], p.astype(BF16),
                                                preferred_element_type=F32)
                    m = m_new
                m_ref[c, qs] = m
                l_ref[c, qs] = l
                acc_ref[c, qs] = acc

    @pl.when(ki < qi)
    def _():
        update(False)

    @pl.when(ki == qi)
    def _():
        update(True)
        lam = _lambda(lam_ref, lam_init)
        for qs in range(t // QS):
            oT = acc_ref[0, qs] / l_ref[0, qs] - lam * (acc_ref[1, qs] / l_ref[1, qs])
            o_ref[0, qs * QS:(qs + 1) * QS, :] = _rms(oT.T, subln_ref[...]) * (1.0 - lam_init)


def _attn_prompt(q, k, v, lam_p, subln, lam_init, t):
    B, L, D = q.shape
    H = DIFF_HEADS
    W = D // H
    n = L // t
    pairs = [(qi, ki) for qi in range(n) for ki in range(qi + 1)]
    qi_tab = jnp.asarray(np.array([p[0] for p in pairs], np.int32))
    ki_tab = jnp.asarray(np.array([p[1] for p in pairs], np.int32))
    qs = min(ATTN_QSUB, t)
    assert t % qs == 0 and t % min(ATTN_KSUB, t) == 0
    kern = functools.partial(_attn_prompt_kernel, t=t, lam_init=lam_init)
    grid_spec = pltpu.PrefetchScalarGridSpec(
        num_scalar_prefetch=2,
        grid=(B, H, len(pairs)),
        in_specs=[pl.BlockSpec(lam_p.shape, lambda b, h, j, qt, kt: (0, 0)),
                  pl.BlockSpec((1, t, W), lambda b, h, j, qt, kt: (b, qt[j], h)),
                  pl.BlockSpec((1, t, W), lambda b, h, j, qt, kt: (b, kt[j], h)),
                  pl.BlockSpec((1, t, W), lambda b, h, j, qt, kt: (b, kt[j], h)),
                  pl.BlockSpec((1, W), lambda b, h, j, qt, kt: (0, 0))],
        out_specs=pl.BlockSpec((1, t, W), lambda b, h, j, qt, kt: (b, qt[j], h)),
        scratch_shapes=[pltpu.VMEM((2, t // qs, 1, qs), F32),
                        pltpu.VMEM((2, t // qs, 1, qs), F32),
                        pltpu.VMEM((2, t // qs, W, qs), F32)],
    )
    return pl.pallas_call(
        kern,
        grid_spec=grid_spec,
        out_shape=jax.ShapeDtypeStruct((B, L, D), F32),
        compiler_params=_params("parallel", "parallel", "arbitrary"),
        name="diff_attn_prompt",
    )(qi_tab, ki_tab, lam_p, q, k, v, subln.reshape(1, W))


def _attn_sample_kernel(pt_ref, lam_ref, qx_ref, *rest, G, n_groups, n_new, lam_init):
    del pt_ref
    ck_refs, cv_refs = rest[:G], rest[G:2 * G]
    kn_ref, vn_ref, subln_ref, o_ref, m_ref, l_ref, acc_ref = rest[2 * G:]
    H = DIFF_HEADS
    p = pl.program_id(1)
    R = qx_ref.shape[1]
    qb = qx_ref[0].astype(BF16)

    @pl.when(p == 0)
    def _():
        m_ref[...] = jnp.full(m_ref.shape, -jnp.inf, F32)
        l_ref[...] = jnp.zeros(l_ref.shape, F32)
        acc_ref[...] = jnp.zeros(acc_ref.shape, F32)

    def head_mask(width):
        rowi = lax.broadcasted_iota(jnp.int32, (R, width), 0)
        coli = lax.broadcasted_iota(jnp.int32, (R, width), 1)
        return rowi, coli, (coli & (H - 1)) == (rowi >> 3)

    def update(s_list, v_list):
        m_prev = m_ref[...]
        m_new = m_prev
        for s in s_list:
            m_new = jnp.maximum(m_new, jnp.max(s, axis=-1, keepdims=True))
        alpha = jnp.exp(m_prev - m_new)
        l_new = alpha * l_ref[...]
        acc = alpha * acc_ref[...]
        for s, v in zip(s_list, v_list):
            pr = jnp.exp(s - m_new)
            l_new = l_new + jnp.sum(pr, axis=-1, keepdims=True)
            acc = acc + _dot(pr, v)
        m_ref[...] = m_new
        l_ref[...] = l_new
        acc_ref[...] = acc

    def page_scores():
        _, _, hm = head_mask(ck_refs[0].shape[1])
        return [jnp.where(hm, _dot_nt(qb, r[0]), -jnp.inf) for r in ck_refs], [r[0] for r in cv_refs]

    if n_groups > 1:
        @pl.when(p < n_groups - 1)
        def _():
            update(*page_scores())

    @pl.when(p == n_groups - 1)
    def _():
        s_list, v_list = page_scores()
        rowi, coli, hm = head_mask(kn_ref.shape[1])
        causal = (coli >> 3) <= (rowi & (n_new - 1))
        s_new = jnp.where(jnp.logical_and(hm, causal), _dot_nt(qb, kn_ref[0]), -jnp.inf)
        update(s_list + [s_new], v_list + [vn_ref[0]])
        lam = _lambda(lam_ref, lam_init)
        o8 = acc_ref[...] / l_ref[...]
        for hd in range(H):
            blk = o8[hd * SUBLANES:(hd + 1) * SUBLANES]
            diff = blk - lam * pltpu.roll(blk, SUBLANES // 2, axis=0)
            o_ref[0, hd * SUBLANES:(hd + 1) * SUBLANES, :] = _rms(diff, subln_ref[...]) * (1.0 - lam_init)


def _attn_sample(qx, cache_k, cache_v, page_table, k_new, v_new, lam_p, subln, lam_init, n_new):
    B, R, W = qx.shape
    rows = cache_k.shape[1]
    n_pages = page_table.shape[1]
    G = math.gcd(n_pages, PAGES_PER_STEP)
    n_groups = n_pages // G
    kern = functools.partial(_attn_sample_kernel, G=G, n_groups=n_groups, n_new=n_new, lam_init=lam_init)

    def page_spec(j):
        return pl.BlockSpec((1, rows, W), lambda b, p, pt: (pt[b, p * G + j], 0, 0))

    def seq(r):
        return pl.BlockSpec((1, r, W), lambda b, p, pt: (b, 0, 0))

    grid_spec = pltpu.PrefetchScalarGridSpec(
        num_scalar_prefetch=1,
        grid=(B, n_groups),
        in_specs=([pl.BlockSpec(lam_p.shape, lambda b, p, pt: (0, 0)), seq(R)]
                  + [page_spec(j) for j in range(G)] * 2
                  + [seq(k_new.shape[1]), seq(v_new.shape[1]),
                     pl.BlockSpec((1, W), lambda b, p, pt: (0, 0))]),
        out_specs=seq(R),
        scratch_shapes=[pltpu.VMEM((R, 1), F32),
                        pltpu.VMEM((R, 1), F32),
                        pltpu.VMEM((R, W), F32)],
    )
    return pl.pallas_call(
        kern,
        grid_spec=grid_spec,
        out_shape=jax.ShapeDtypeStruct((B, R, W), F32),
        compiler_params=_params("parallel", "arbitrary"),
        name="diff_attn_sample",
    )(page_table, lam_p, qx, *([cache_k] * G), *([cache_v] * G), k_new, v_new, subln.reshape(1, W))


def _lambda_init(layer):
    return 0.8 - 0.6 * math.exp(-0.3 * layer)


def _pad_rows(x, rows):
    return jnp.pad(x, [(0, 0), (0, rows - x.shape[1])] + [(0, 0)] * (x.ndim - 2))


def kernel(x_prompt, x_sample, state_delta, state_conv, cache_k, cache_v, page_table, norm_a, w_in_a, conv_w_a, a_log, dt_bias, onorm_a, w_out_a, kv_norm, w_kv, norm_b, w_q_b, lam_b, subln_b, w_out_b, norm_m, w_rg, b_rg, w_re, b_re, w_gate_e, w_up_e, w_down_e, final_norm):
    D = x_prompt.shape[-1]
    H = GDN_HEADS
    CH = 3 * H * GDN_DK
    NH, W = DIFF_HEADS, 2 * DIFF_DH
    n_in = w_in_a.shape[-1]
    n_in_pad = -(-n_in // LANES) * LANES
    depth = norm_m.shape[0]
    n_a = norm_a.shape[0]
    assert depth == 2 and n_a == 1 and norm_b.shape[0] == 1

    w_in_bf = jnp.pad(w_in_a[0], ((0, 0), (0, n_in_pad - n_in))).astype(BF16)
    w_out_a_bf = w_out_a[0].astype(BF16)
    w_q_bf = w_q_b[0].astype(BF16)
    w_kv_bf = w_kv.astype(BF16)
    w_out_b_bf = w_out_b[0].astype(BF16)
    n_route = N_GROUPS + N_EXPERTS
    wr = jnp.pad(jnp.concatenate([w_rg, w_re], axis=-1), ((0, 0), (0, 0), (0, LANES - n_route)))
    br = jnp.pad(jnp.concatenate([b_rg, b_re], axis=-1), ((0, 0), (0, LANES - n_route))).reshape(depth, 1, LANES)
    alog_l = jnp.zeros((1, LANES), F32).at[0, H:2 * H].set(a_log[0])
    dtb_l = jnp.zeros((1, LANES), F32).at[0, H:2 * H].set(dt_bias[0])

    def gdn_layer(h, B, L, conv0, s0):
        T = B * L
        tm = min(256, T)
        proj = _norm_matmul(h, norm_a[0], w_in_bf, tm)
        C = min(GDN_CHUNK, -(-L // SUBLANES) * SUBLANES)
        Lp = -(-L // C) * C
        valid_last = L - (Lp - C)
        proj3 = _pad_rows(proj.reshape(B, L, n_in_pad), Lp)
        conv0p = jnp.pad(conv0, ((0, 0), (SUBLANES - (CONV_W - 1), 0), (0, 0)))
        bb = 2
        o3, s_new, cbuf = _gdn_mixer(proj3, conv0p, s0, conv_w_a[0], alog_l, dtb_l, onorm_a[0], C, valid_last, bb)
        h = _matmul_residual(o3[:, :L].reshape(T, D), w_out_a_bf, h, tm)
        return h, s_new[None], cbuf[:, SUBLANES - (CONV_W - 1):][None]

    def attn_layer(h, B, L, past_len, paged):
        T = B * L
        tm = min(256, T)
        pos = past_len + jnp.arange(L, dtype=jnp.int32)
        cos_t, sin_t = _rope_tables(pos)
        if L % tm != 0:
            cos_t = jnp.tile(cos_t, (T // L, 1))
            sin_t = jnp.tile(sin_t, (T // L, 1))
        q, k, v = _qkv_proj(h, norm_b[0], kv_norm, w_q_bf, w_kv_bf, cos_t, sin_t, tm)
        lam_init = _lambda_init(n_a)
        if paged is None:
            o = _attn_prompt(q.reshape(B, L, D), k.reshape(B, L, D), v.reshape(B, L, D),
                             lam_b[0], subln_b[0], lam_init, min(ATTN_TILE, L))
            o = o.reshape(T, D)
        else:
            ck, cv, pt = paged
            assert 2 * L == SUBLANES and NH == SUBLANES
            q5 = q.reshape(B, L, NH, 2, DIFF_DH).transpose(0, 2, 1, 3, 4)
            sel = jnp.eye(2, dtype=F32)[None, None, :, None, :, None]
            qx = (q5[:, :, None] * sel).reshape(B, NH * 2 * L, W)
            o = _attn_sample(qx, ck, cv, pt, k.reshape(B, L * NH, W), v.reshape(B, L * NH, W),
                             lam_b[0], subln_b[0], lam_init, L)
            o = o.reshape(B, NH, 2 * L, W)[:, :, :L].transpose(0, 2, 1, 3).reshape(T, D)
        h = _matmul_residual(o, w_out_b_bf, h, tm)
        kshape = (B, L, NH, W)
        return h, k.reshape(kshape), v.reshape(kshape)

    def moe(hp, hs, layer, last):
        return _hier_moe(hp, hs, norm_m[layer], wr[layer], br[layer], w_gate_e, w_up_e, w_down_e, layer,
                         final_norm, last)

    Bp, Lq, _ = x_prompt.shape
    Bs, Ls, _ = x_sample.shape
    hp = x_prompt.reshape(Bp * Lq, D)
    hs = x_sample.reshape(Bs * Ls, D)

    hp, sd_p, sc_p = gdn_layer(hp, Bp, Lq, jnp.zeros((Bp, CONV_W - 1, CH), F32),
                               jnp.zeros((Bp, H, GDN_DK, GDN_DK), F32))
    hs, sd_s, sc_s = gdn_layer(hs, Bs, Ls, state_conv[0], state_delta[0])
    hp, hs = moe(hp, hs, 0, False)

    n_phys, page, nh, w = cache_k.shape
    past_len = page_table.shape[1] * page
    ck = cache_k.reshape(n_phys, page * nh, w)
    cv = cache_v.reshape(n_phys, page * nh, w)
    hp, k_p, v_p = attn_layer(hp, Bp, Lq, 0, None)
    hs, k_s, v_s = attn_layer(hs, Bs, Ls, past_len, (ck, cv, page_table))
    y_p, y_s = moe(hp, hs, 1, True)
    return (y_p.reshape(Bp, Lq, D), y_s.reshape(Bs, Ls, D), sd_p, sc_p, k_p, v_p, sd_s, sc_s, k_s, v_s)
```

```python
import functools
import math

import numpy as np
import jax
import jax.numpy as jnp
from jax import lax
from jax.experimental import pallas as pl
from jax.experimental.pallas import tpu as pltpu

F32 = jnp.float32
BF16 = jnp.bfloat16
EPS = 1e-6
LANES = 128
SUBLANES = 8
VMEM_LIMIT = 56 * 1024 * 1024

CONV_W = 4
GDN_HEADS = 8
GDN_DK = 128
GDN_CHUNK = 64
DIFF_HEADS = 8
DIFF_DH = 64
ROPE_THETA = 10000.0
N_GROUPS = 4
EXP_PER_GROUP = 8
N_EXPERTS = N_GROUPS * EXP_PER_GROUP
TOP_K = 2
MOE_BLOCK = 128
ROUTE_OFF = N_GROUPS
ATTN_TILE = 512
PAGES_PER_STEP = 16
ATTN_QSUB = LANES
ATTN_KSUB = 256
LOG2E = 1.4426950408889634
DMA_UNROLL = 8
ROUTER_TILE = 256
GATHER_AHEAD = 3
GATHER_SLOTS = GATHER_AHEAD + 1


def _params(*sem):
    return pltpu.CompilerParams(dimension_semantics=sem, vmem_limit_bytes=VMEM_LIMIT)


def _dot(a, b):
    return jnp.dot(a.astype(BF16), b.astype(BF16), preferred_element_type=F32)


def _dot_nt(a, b):
    return lax.dot_general(a.astype(BF16), b.astype(BF16), (((1,), (1,)), ((), ())),
                           preferred_element_type=F32)


def _dot_tn(a, b):
    return lax.dot_general(a.astype(BF16), b.astype(BF16), (((0,), (0,)), ((), ())),
                           preferred_element_type=F32)


def _split2(x):
    hi = x.astype(BF16)
    lo = (x - hi.astype(F32)).astype(BF16)
    return hi, lo


def _dot3(a, b):
    ah, al = _split2(a)
    bh, bl = _split2(b)
    d = functools.partial(jnp.dot, preferred_element_type=F32)
    return d(ah, bh) + (d(ah, bl) + d(al, bh))


def _dot_exact_lhs(a_bf, b):
    b1 = b.astype(BF16)
    r1 = b - b1.astype(F32)
    b2 = r1.astype(BF16)
    b3 = (r1 - b2.astype(F32)).astype(BF16)
    d = functools.partial(jnp.dot, preferred_element_type=F32)
    return d(a_bf, b1) + (d(a_bf, b2) + d(a_bf, b3))


def _sigmoid(x):
    return 1.0 / (1.0 + jnp.exp(-x))


def _silu(x):
    return x * _sigmoid(x)


def _softplus(x):
    return jnp.maximum(x, 0.0) + jnp.log1p(jnp.exp(-jnp.abs(x)))


def _rms(x, g):
    return x * lax.rsqrt(jnp.mean(x * x, axis=-1, keepdims=True) + EPS) * g


def _norm_mm_kernel(x_ref, g_ref, w_ref, o_ref):
    xn = _rms(x_ref[...], g_ref[...])
    o_ref[...] = jnp.dot(xn.astype(BF16), w_ref[...], preferred_element_type=F32)


def _norm_matmul(x, g, w_bf, tm):
    T, D = x.shape
    N = w_bf.shape[1]
    return pl.pallas_call(
        _norm_mm_kernel,
        grid=(T // tm,),
        in_specs=[pl.BlockSpec((tm, D), lambda i: (i, 0)),
                  pl.BlockSpec((1, D), lambda i: (0, 0)),
                  pl.BlockSpec((D, N), lambda i: (0, 0))],
        out_specs=pl.BlockSpec((tm, N), lambda i: (i, 0)),
        out_shape=jax.ShapeDtypeStruct((T, N), F32),
        compiler_params=_params("parallel"),
        name="norm_matmul",
    )(x, g.reshape(1, D), w_bf)


def _mm_res_kernel(x_ref, w_ref, h_ref, o_ref):
    o_ref[...] = h_ref[...] + jnp.dot(x_ref[...].astype(BF16), w_ref[...], preferred_element_type=F32)


def _matmul_residual(x, w_bf, h, tm):
    T, K = x.shape
    N = w_bf.shape[1]
    return pl.pallas_call(
        _mm_res_kernel,
        grid=(T // tm,),
        in_specs=[pl.BlockSpec((tm, K), lambda i: (i, 0)),
                  pl.BlockSpec((K, N), lambda i: (0, 0)),
                  pl.BlockSpec((tm, N), lambda i: (i, 0))],
        out_specs=pl.BlockSpec((tm, N), lambda i: (i, 0)),
        out_shape=jax.ShapeDtypeStruct((T, N), F32),
        compiler_params=_params("parallel"),
        name="matmul_residual",
    )(x, w_bf, h)


def _gdn_kernel(proj_ref, conv0_ref, s0_ref, cw_ref, alog_ref, dtb_ref, onorm_ref,
                o_ref, sout_ref, cout_ref, ext_ref, s_ref, *, C, nc, valid_last, bb):
    H, DK = GDN_HEADS, GDN_DK
    QK = H * DK
    CH = 3 * QK
    c = pl.program_id(1)
    masked = valid_last < C

    @pl.when(c == 0)
    def _():
        ext_ref[:, 0:SUBLANES, :] = conv0_ref[...]
        s_ref[...] = s0_ref[...]

    row1 = lax.broadcasted_iota(jnp.int32, (C, 1), 0)
    row = lax.broadcasted_iota(jnp.int32, (C, C), 0)
    col = lax.broadcasted_iota(jnp.int32, (C, C), 1)
    incl = row >= col
    strict = row > col
    eye = jnp.where(row == col, 1.0, 0.0).astype(F32)
    tril_bf = jnp.where(incl, 1.0, 0.0).astype(BF16)
    if masked:
        valid = jnp.logical_or(row1 < valid_last, c < nc - 1)

    seqs = []
    for b in range(bb):
        xin = proj_ref[b, :, 0:CH]
        ext_ref[b, SUBLANES:SUBLANES + C, :] = xin
        base = SUBLANES - (CONV_W - 1)
        acc = ext_ref[b, base:base + C, :] * cw_ref[0:1, :]
        for i in range(1, CONV_W - 1):
            acc = acc + ext_ref[b, base + i:base + i + C, :] * cw_ref[i:i + 1, :]
        acc = acc + xin * cw_ref[CONV_W - 1:CONV_W, :]
        qkv = _silu(acc)

        @pl.when(c == nc - 1)
        def _():
            cout_ref[b] = ext_ref[b, valid_last:valid_last + SUBLANES, :]

        ext_ref[b, 0:SUBLANES, :] = ext_ref[b, C:C + SUBLANES, :]

        ba = proj_ref[b, :, CH + QK:CH + QK + LANES]
        beta_t = _sigmoid(ba)
        g_t = -jnp.exp(alog_ref[...]) * _softplus(ba + dtb_ref[...])
        if masked:
            beta_t = jnp.where(valid, beta_t, 0.0)
            g_t = jnp.where(valid, g_t, 0.0)
        G = _dot_exact_lhs(tril_bf, g_t)
        g_last = G[C - 1:C, :]
        seqs.append(dict(qkv=qkv, beta=beta_t, G=G, GT=G.T, expG=jnp.exp(G),
                         exp_last=jnp.exp(g_last), k_scale=jnp.exp(g_last - G)))

    chains = [(b, hd) for b in range(bb) for hd in range(H)]
    st = []
    for b, hd in chains:
        sq = seqs[b]
        q = sq["qkv"][:, hd * DK:(hd + 1) * DK]
        k = sq["qkv"][:, QK + hd * DK:QK + (hd + 1) * DK]
        v = sq["qkv"][:, 2 * QK + hd * DK:2 * QK + (hd + 1) * DK]
        q = q * lax.rsqrt(jnp.sum(q * q, axis=-1, keepdims=True) + EPS) * (DK ** -0.5)
        k = k * lax.rsqrt(jnp.sum(k * k, axis=-1, keepdims=True) + EPS)
        if masked:
            k = jnp.where(valid, k, 0.0)
        bcol = sq["beta"][:, hd:hd + 1]
        gcol = sq["G"][:, H + hd:H + hd + 1]
        grow = sq["GT"][H + hd:H + hd + 1, :]
        eg = sq["expG"][:, H + hd:H + hd + 1]
        decay = jnp.exp(jnp.where(incl, gcol - grow, -jnp.inf))
        st.append(dict(q=q, k=k, decay=decay, eg=eg, bcol=bcol,
                       rhs=jnp.concatenate([bcol * v, (bcol * eg) * k], axis=1)))
    for d in st:
        kk = _dot_nt(d["k"], d["k"])
        d["p"] = -jnp.where(strict, d["bcol"] * kk * d["decay"], 0.0)
        d["t"] = eye + d["p"]
    for _ in range(int(math.log2(C)) - 1):
        for d in st:
            d["p"] = _dot3(d["p"], d["p"])
        for d in st:
            d["t"] = d["t"] + _dot3(d["t"], d["p"])
    for d in st:
        d["sol"] = _dot3(d["t"], d["rhs"])
    for d, (b, hd) in zip(st, chains):
        d["S"] = s_ref[b, hd]
        d["u"] = d["sol"][:, :DK] - _dot(d["sol"][:, DK:], d["S"])
    for d in st:
        d["qk"] = _dot_nt(d["q"], d["k"]) * d["decay"]
    for d in st:
        d["o"] = _dot(d["q"] * d["eg"], d["S"]) + _dot(d["qk"], d["u"])
    for d, (b, hd) in zip(st, chains):
        sq = seqs[b]
        s_ref[b, hd] = (sq["exp_last"][:, H + hd:H + hd + 1] * d["S"]
                        + _dot_tn(d["k"] * sq["k_scale"][:, H + hd:H + hd + 1], d["u"]))
    for d, (b, hd) in zip(st, chains):
        z = proj_ref[b, :, CH + hd * DK:CH + (hd + 1) * DK]
        o_ref[b, :, hd * DK:(hd + 1) * DK] = _rms(d["o"], onorm_ref[...]) * _silu(z)

    @pl.when(c == nc - 1)
    def _():
        sout_ref[...] = s_ref[...]


def _gdn_mixer(proj, conv0p, s0, conv_w, alog_l, dtb_l, onorm, C, valid_last, bb):
    B, Lp, NP = proj.shape
    nc = Lp // C
    H, DK = GDN_HEADS, GDN_DK
    D = H * DK
    CH = 3 * H * DK
    kern = functools.partial(_gdn_kernel, C=C, nc=nc, valid_last=valid_last, bb=bb)
    return pl.pallas_call(
        kern,
        grid=(B // bb, nc),
        in_specs=[pl.BlockSpec((bb, C, NP), lambda b, c: (b, c, 0)),
                  pl.BlockSpec((bb, SUBLANES, CH), lambda b, c: (b, 0, 0)),
                  pl.BlockSpec((bb, H, DK, DK), lambda b, c: (b, 0, 0, 0)),
                  pl.BlockSpec((CONV_W, CH), lambda b, c: (0, 0)),
                  pl.BlockSpec((1, LANES), lambda b, c: (0, 0)),
                  pl.BlockSpec((1, LANES), lambda b, c: (0, 0)),
                  pl.BlockSpec((1, DK), lambda b, c: (0, 0))],
        out_specs=[pl.BlockSpec((bb, C, D), lambda b, c: (b, c, 0)),
                   pl.BlockSpec((bb, H, DK, DK), lambda b, c: (b, 0, 0, 0)),
                   pl.BlockSpec((bb, SUBLANES, CH), lambda b, c: (b, 0, 0))],
        out_shape=[jax.ShapeDtypeStruct((B, Lp, D), F32),
                   jax.ShapeDtypeStruct((B, H, DK, DK), F32),
                   jax.ShapeDtypeStruct((B, SUBLANES, CH), F32)],
        scratch_shapes=[pltpu.VMEM((bb, SUBLANES + C, CH), F32),
                        pltpu.VMEM((bb, H, DK, DK), F32)],
        compiler_params=_params("parallel", "arbitrary"),
        name="gdn_mixer",
    )(proj, conv0p, s0, conv_w, alog_l, dtb_l, onorm.reshape(1, DK))


def _router_kernel(hp_ref, hs_ref, g_ref, wr_ref, br_ref, xn_ref, gate_ref, route_ref, cnt_ref, carry_ref,
                   *, n_p):
    i = pl.program_id(0)

    @pl.when(i == 0)
    def _():
        carry_ref[...] = jnp.zeros(carry_ref.shape, F32)

    x = jnp.where(i < n_p, hp_ref[...], hs_ref[...])
    xn = _rms(x, g_ref[...])
    xn_ref[...] = xn
    logits = _dot3(xn, wr_ref[...]) + br_ref[...]
    tm = logits.shape[0]
    lane = lax.broadcasted_iota(jnp.int32, (tm, LANES), 1).astype(F32)
    neg = -jnp.inf
    big = float(LANES)

    lg = jnp.where(lane < N_GROUPS, logits, neg)
    eg = jnp.exp(lg - jnp.max(lg, axis=-1, keepdims=True))
    pg = eg / jnp.sum(eg, axis=-1, keepdims=True)
    pmax = jnp.max(pg, axis=-1, keepdims=True)
    g_sel = jnp.min(jnp.where(pg == pmax, lane, big), axis=-1, keepdims=True)

    lo = ROUTE_OFF + EXP_PER_GROUP * g_sel
    in_grp = jnp.logical_and(lane >= lo, lane < lo + EXP_PER_GROUP)
    le = jnp.where(in_grp, logits, neg)
    ee = jnp.exp(le - jnp.max(le, axis=-1, keepdims=True))
    pe = ee / jnp.sum(ee, axis=-1, keepdims=True)
    pe = jnp.where(in_grp, pe, -1.0)
    p1 = jnp.max(pe, axis=-1, keepdims=True)
    i1 = jnp.min(jnp.where(pe == p1, lane, big), axis=-1, keepdims=True)
    pe2 = jnp.where(lane == i1, -1.0, pe)
    p2 = jnp.max(pe2, axis=-1, keepdims=True)
    i2 = jnp.min(jnp.where(pe2 == p2, lane, big), axis=-1, keepdims=True)
    den = p1 + p2
    gate_ref[...] = jnp.where(lane == 0, pmax * p1 / den, jnp.where(lane == 1, pmax * p2 / den, 0.0))

    e1 = i1 - ROUTE_OFF
    e2 = i2 - ROUTE_OFF
    oh1 = jnp.where(lane == e1, 1.0, 0.0)
    oh2 = jnp.where(lane == e2, 1.0, 0.0)
    both = oh1 + oh2
    earlier = (lax.broadcasted_iota(jnp.int32, (tm, tm), 0) > lax.broadcasted_iota(jnp.int32, (tm, tm), 1))
    before = (jnp.dot(jnp.where(earlier, 1.0, 0.0).astype(BF16), both.astype(BF16), preferred_element_type=F32)
              + carry_ref[0:1, :])
    r1 = jnp.sum(oh1 * before, axis=-1, keepdims=True)
    r2 = jnp.sum(oh2 * before, axis=-1, keepdims=True)
    carry_ref[...] = carry_ref[...] + jnp.sum(both, axis=0, keepdims=True)
    cnt_ref[...] = carry_ref[...]
    route_ref[...] = jnp.where(lane == 0, e1, jnp.where(lane == 1, e2, jnp.where(
        lane == 2, r1, jnp.where(lane == 3, r2, 0.0)))).astype(jnp.int32)


def _router(hp, hs, g, wr, br, tm):
    Tp, D = hp.shape
    Ts = hs.shape[0]
    n_p, n_s = Tp // tm, Ts // tm
    T = Tp + Ts
    kern = functools.partial(_router_kernel, n_p=n_p)
    row = pl.BlockSpec((tm, D), lambda i: (i, 0))
    lanes = pl.BlockSpec((tm, LANES), lambda i: (i, 0))
    return pl.pallas_call(
        kern,
        grid=(n_p + n_s,),
        in_specs=[pl.BlockSpec((tm, D), lambda i: (jnp.minimum(i, n_p - 1), 0)),
                  pl.BlockSpec((tm, D), lambda i: (jnp.maximum(i - n_p, 0), 0)),
                  pl.BlockSpec((1, D), lambda i: (0, 0)),
                  pl.BlockSpec((D, LANES), lambda i: (0, 0)),
                  pl.BlockSpec((1, LANES), lambda i: (0, 0))],
        out_specs=[row, lanes, lanes, pl.BlockSpec((SUBLANES, LANES), lambda i: (0, 0))],
        out_shape=[jax.ShapeDtypeStruct((T, D), F32),
                   jax.ShapeDtypeStruct((T, LANES), F32),
                   jax.ShapeDtypeStruct((T, LANES), jnp.int32),
                   jax.ShapeDtypeStruct((SUBLANES, LANES), F32)],
        scratch_shapes=[pltpu.VMEM((SUBLANES, LANES), F32)],
        compiler_params=_params("arbitrary"),
        name="moe_router",
    )(hp, hs, g.reshape(1, D), wr, br)


def _row_copy(src_hbm, row, dst, slot_idx, sem):
    return pltpu.make_async_copy(src_hbm.at[pl.ds(row, 1)], dst.at[slot_idx], sem)


def _expert_kernel(be_ref, rt_ref, nu_ref, xn_hbm, wg_ref, wu_ref, wd_ref, ys_ref,
                   xbuf, wg_bf, wu_bf, wd_bf, sem):
    i = pl.program_id(0)
    n_used = nu_ref[0]

    def issue(blk, slot):
        for r in range(MOE_BLOCK):
            _row_copy(xn_hbm, rt_ref[blk * MOE_BLOCK + r], xbuf, (slot, pl.ds(r, 1)), sem.at[slot]).start()

    def wait(slot):
        for r in range(MOE_BLOCK):
            _row_copy(xn_hbm, 0, xbuf, (slot, pl.ds(r, 1)), sem.at[slot]).wait()

    def issue_pos(pos):
        issue(jnp.minimum(pos, n_used - 1), pos % GATHER_SLOTS)

    @pl.when(i == 0)
    def _():
        for k in range(GATHER_AHEAD):
            issue_pos(jnp.int32(k))

    @pl.when(jnp.logical_or(i == 0, be_ref[i] != be_ref[jnp.maximum(i - 1, 0)]))
    def _():
        wg_bf[...] = wg_ref[0, 0].astype(BF16)
        wu_bf[...] = wu_ref[0, 0].astype(BF16)
        wd_bf[...] = wd_ref[0, 0].astype(BF16)

    @pl.when(i < n_used)
    def _():
        slot = i % GATHER_SLOTS
        wait(slot)
        issue_pos(i + GATHER_AHEAD)
        x = xbuf[slot].astype(BF16)
        gt = jnp.dot(x, wg_bf[...], preferred_element_type=F32)
        up = jnp.dot(x, wu_bf[...], preferred_element_type=F32)
        hid = _silu(gt) * up
        ys_ref[...] = jnp.dot(hid.astype(BF16), wd_bf[...], preferred_element_type=F32)

    @pl.when(i == n_used - 1)
    def _():
        for k in range(GATHER_AHEAD):
            wait((n_used + k) % GATHER_SLOTS)

    @pl.when(i >= n_used)
    def _():
        ys_ref[...] = jnp.zeros(ys_ref.shape, F32)


def _experts(xn, block_e, row_tok, n_used, w_gate, w_up, w_down, layer):
    T, D = xn.shape
    FF = w_gate.shape[-1]
    n_blocks = block_e.shape[0]
    P = n_blocks * MOE_BLOCK
    grid_spec = pltpu.PrefetchScalarGridSpec(
        num_scalar_prefetch=3,
        grid=(n_blocks,),
        in_specs=[pl.BlockSpec(memory_space=pl.ANY),
                  pl.BlockSpec((1, 1, D, FF), lambda i, be, rt, nu: (layer, be[i], 0, 0)),
                  pl.BlockSpec((1, 1, D, FF), lambda i, be, rt, nu: (layer, be[i], 0, 0)),
                  pl.BlockSpec((1, 1, FF, D), lambda i, be, rt, nu: (layer, be[i], 0, 0))],
        out_specs=pl.BlockSpec((MOE_BLOCK, D), lambda i, be, rt, nu: (i, 0)),
        scratch_shapes=[pltpu.VMEM((GATHER_SLOTS, MOE_BLOCK, D), F32),
                        pltpu.VMEM((D, FF), BF16),
                        pltpu.VMEM((D, FF), BF16),
                        pltpu.VMEM((FF, D), BF16),
                        pltpu.SemaphoreType.DMA((GATHER_SLOTS,))],
    )
    return pl.pallas_call(
        _expert_kernel,
        grid_spec=grid_spec,
        out_shape=jax.ShapeDtypeStruct((P, D), F32),
        compiler_params=_params("arbitrary"),
        name="moe_experts",
    )(block_e, row_tok, n_used, xn, w_gate, w_up, w_down)


def _combine_kernel(dest_ref, ys_hbm, hp_ref, hs_ref, gate_ref, fn_ref, op_ref, os_ref, ybuf, sem,
                    *, tm, n_p, n_steps, final_norm):
    i = pl.program_id(0)
    n = pl.num_programs(0)

    def issue(step, slot):
        def body(r, carry):
            for kk in range(TOP_K):
                d = dest_ref[(step * tm + r) * TOP_K + kk]
                _row_copy(ys_hbm, d, ybuf, (slot, kk, pl.ds(r, 1)), sem.at[slot]).start()
            return carry
        lax.fori_loop(0, tm, body, 0, unroll=DMA_UNROLL // TOP_K)

    def wait(slot):
        for r in range(tm):
            for kk in range(TOP_K):
                _row_copy(ys_hbm, 0, ybuf, (slot, kk, pl.ds(r, 1)), sem.at[slot]).wait()

    @pl.when(i == 0)
    def _():
        for k in range(min(GATHER_AHEAD, n_steps)):
            issue(k, k)

    @pl.when(i + GATHER_AHEAD < n)
    def _():
        issue(i + GATHER_AHEAD, (i + GATHER_AHEAD) % GATHER_SLOTS)

    slot = i % GATHER_SLOTS
    wait(slot)
    gate = gate_ref[...]
    moe = ybuf[slot, 0] * gate[:, 0:1] + ybuf[slot, 1] * gate[:, 1:2]

    def finish(h_ref, o_ref):
        y = h_ref[...] + moe
        if final_norm:
            y = _rms(y, fn_ref[...])
        o_ref[...] = y

    @pl.when(i < n_p)
    def _():
        finish(hp_ref, op_ref)

    @pl.when(i >= n_p)
    def _():
        finish(hs_ref, os_ref)


def _combine(ys, dest, hp, hs, gates, fnorm, tm, final_norm):
    Tp, D = hp.shape
    Ts = hs.shape[0]
    n_p, n_s = Tp // tm, Ts // tm
    kern = functools.partial(_combine_kernel, tm=tm, n_p=n_p, n_steps=n_p + n_s, final_norm=final_norm)
    p_blk = pl.BlockSpec((tm, D), lambda i, d: (jnp.minimum(i, n_p - 1), 0))
    s_blk = pl.BlockSpec((tm, D), lambda i, d: (jnp.maximum(i - n_p, 0), 0))
    grid_spec = pltpu.PrefetchScalarGridSpec(
        num_scalar_prefetch=1,
        grid=(n_p + n_s,),
        in_specs=[pl.BlockSpec(memory_space=pl.ANY), p_blk, s_blk,
                  pl.BlockSpec((tm, LANES), lambda i, d: (i, 0)),
                  pl.BlockSpec((1, D), lambda i, d: (0, 0))],
        out_specs=[p_blk, s_blk],
        scratch_shapes=[pltpu.VMEM((GATHER_SLOTS, TOP_K, tm, D), F32),
                        pltpu.SemaphoreType.DMA((GATHER_SLOTS,))],
    )
    return pl.pallas_call(
        kern,
        grid_spec=grid_spec,
        out_shape=[jax.ShapeDtypeStruct((Tp, D), F32), jax.ShapeDtypeStruct((Ts, D), F32)],
        compiler_params=_params("arbitrary"),
        name="moe_combine",
    )(dest, ys, hp, hs, gates, fnorm.reshape(1, D))


def _route_tables(route, counts):
    T = route.shape[0]
    A = T * TOP_K
    flat_e = route[:, 0:TOP_K].reshape(A)
    rank = route[:, TOP_K:2 * TOP_K].reshape(A)
    padded = (counts + MOE_BLOCK - 1) // MOE_BLOCK * MOE_BLOCK
    pad_end = jnp.cumsum(padded)
    pad_start = pad_end - padded
    dest = (pad_start[flat_e] + rank).astype(jnp.int32)
    n_blocks = -(-A // MOE_BLOCK) + N_EXPERTS
    P = n_blocks * MOE_BLOCK
    row_tok = jnp.zeros((P,), jnp.int32).at[dest].set(jnp.arange(A, dtype=jnp.int32) // TOP_K)
    starts = jnp.arange(n_blocks, dtype=jnp.int32) * MOE_BLOCK
    block_e = jnp.minimum(jnp.sum((pad_end[None, :] <= starts[:, None]).astype(jnp.int32), axis=1),
                          N_EXPERTS - 1)
    n_used = (pad_end[-1] // MOE_BLOCK).astype(jnp.int32).reshape(1)
    return dest, row_tok, block_e, n_used


def _hier_moe(hp, hs, norm_g, wr, br, w_gate, w_up, w_down, layer, fnorm, final_norm):
    t_all = math.gcd(hp.shape[0], hs.shape[0])
    xn, gates, route, cnt = _router(hp, hs, norm_g, wr, br, math.gcd(ROUTER_TILE, t_all))
    counts = cnt[0, :N_EXPERTS].astype(jnp.int32)
    dest, row_tok, block_e, n_used = _route_tables(route, counts)
    ys = _experts(xn, block_e, row_tok, n_used, w_gate, w_up, w_down, layer)
    return _combine(ys, dest, hp, hs, gates, fnorm, math.gcd(MOE_BLOCK, t_all), final_norm)


def _rope_tile(x, cos, sin_signed, lane):
    half = DIFF_DH // 2
    rot = jnp.where((lane & (DIFF_DH - 1)) < half,
                    pltpu.roll(x, LANES - half, axis=1),
                    pltpu.roll(x, half, axis=1))
    return x * cos + rot * sin_signed


def _qkv_kernel(h_ref, gq_ref, gkv_ref, wq_ref, wkv_ref, cos_ref, sin_ref, q_ref, k_ref, v_ref):
    x = h_ref[...]
    D = x.shape[1]
    xs = x * lax.rsqrt(jnp.mean(x * x, axis=-1, keepdims=True) + EPS)
    q = jnp.dot((xs * gq_ref[...]).astype(BF16), wq_ref[...], preferred_element_type=F32)
    kv = jnp.dot((xs * gkv_ref[...]).astype(BF16), wkv_ref[...], preferred_element_type=F32)
    cos = cos_ref[...]
    sin = sin_ref[...]
    lane = lax.broadcasted_iota(jnp.int32, cos.shape, 1)
    for hd in range(D // LANES):
        sl = slice(hd * LANES, (hd + 1) * LANES)
        q_ref[:, sl] = _rope_tile(q[:, sl], cos, sin, lane) * (DIFF_DH ** -0.5)
        k_ref[:, sl] = _rope_tile(kv[:, sl], cos, sin, lane)
    v_ref[...] = kv[:, D:]


def _qkv_proj(h, gq, gkv, wq_bf, wkv_bf, cos_t, sin_t, tm):
    T, D = h.shape
    nt = cos_t.shape[0] // tm
    row = pl.BlockSpec((tm, D), lambda i: (i, 0))
    vec = pl.BlockSpec((1, D), lambda i: (0, 0))
    tab = pl.BlockSpec((tm, LANES), lambda i: (i % nt, 0))
    return pl.pallas_call(
        _qkv_kernel,
        grid=(T // tm,),
        in_specs=[row, vec, vec,
                  pl.BlockSpec((D, D), lambda i: (0, 0)),
                  pl.BlockSpec((D, 2 * D), lambda i: (0, 0)),
                  tab, tab],
        out_specs=[row, row, row],
        out_shape=[jax.ShapeDtypeStruct((T, D), F32)] * 3,
        compiler_params=_params("parallel"),
        name="qkv_proj",
    )(h, gq.reshape(1, D), gkv.reshape(1, D), wq_bf, wkv_bf, cos_t, sin_t)


def _rope_tables(pos):
    half = DIFF_DH // 2
    inv = ROPE_THETA ** (-jnp.arange(half, dtype=F32) / half)
    ang = pos.astype(F32)[:, None] * inv[None, :]
    cos = jnp.cos(ang)
    sin = jnp.sin(ang)
    reps = LANES // DIFF_DH
    return (jnp.tile(jnp.concatenate([cos, cos], axis=1), (1, reps)),
            jnp.tile(jnp.concatenate([-sin, sin], axis=1), (1, reps)))


def _lambda(lam_ref, lam_init):
    lp = lam_ref[...]
    a = jnp.sum(lp[0:1, :] * lp[1:2, :], axis=-1, keepdims=True)
    b = jnp.sum(lp[2:3, :] * lp[3:4, :], axis=-1, keepdims=True)
    return jnp.exp(a) - jnp.exp(b) + lam_init


def _attn_prompt_kernel(qi_ref, ki_ref, lam_ref, q_ref, k_ref, v_ref, subln_ref, o_ref,
                        m_ref, l_ref, acc_ref, *, t, lam_init):
    j = pl.program_id(2)
    qi = qi_ref[j]
    ki = ki_ref[j]

    @pl.when(ki == 0)
    def _():
        m_ref[...] = jnp.full(m_ref.shape, -jnp.inf, F32)
        l_ref[...] = jnp.zeros(l_ref.shape, F32)
        acc_ref[...] = jnp.zeros(acc_ref.shape, F32)

    QS, KS = min(ATTN_QSUB, t), min(ATTN_KSUB, t)

    def update(diagonal):
        q = q_ref[0] * LOG2E
        kb = k_ref[0].astype(BF16)
        vT = v_ref[0].T.astype(BF16)
        lane = lax.broadcasted_iota(jnp.int32, q.shape, 1)
        for c in range(2):
            qc = jnp.where((lane < DIFF_DH) if c == 0 else (lane >= DIFF_DH), q, 0.0).astype(BF16)
            for qs in range(t // QS):
                qsub = qc[qs * QS:(qs + 1) * QS]
                m = m_ref[c, qs]
                l = l_ref[c, qs]
                acc = acc_ref[c, qs]
                for ks in range(t // KS):
                    if diagonal and ks * KS > qs * QS + QS - 1:
                        continue
                    sT = _dot_nt(kb[ks * KS:(ks + 1) * KS], qsub)
                    if diagonal and (ks + 1) * KS - 1 > qs * QS:
                        keyi = ks * KS + lax.broadcasted_iota(jnp.int32, (KS, QS), 0)
                        qidx = qs * QS + lax.broadcasted_iota(jnp.int32, (KS, QS), 1)
                        sT = jnp.where(keyi <= qidx, sT, -jnp.inf)
                    m_new = jnp.maximum(m, jnp.max(sT, axis=0, keepdims=True))
                    alpha = jnp.exp2(m - m_new)
                    p = jnp.exp2(sT - m_new)
                    l = alpha * l + jnp.sum(p, axis=0, keepdims=True)
                    acc = alpha * acc + jnp.dot(vT[:, ks * KS:(ks + 1) * KS], p.astype(BF16),
                                                preferred_element_type=F32)
                    m = m_new
                m_ref[c, qs] = m
                l_ref[c, qs] = l
                acc_ref[c, qs] = acc

    @pl.when(ki < qi)
    def _():
        update(False)

    @pl.when(ki == qi)
    def _():
        update(True)
        lam = _lambda(lam_ref, lam_init)
        for qs in range(t // QS):
            oT = acc_ref[0, qs] / l_ref[0, qs] - lam * (acc_ref[1, qs] / l_ref[1, qs])
            o_ref[0, qs * QS:(qs + 1) * QS, :] = _rms(oT.T, subln_ref[...]) * (1.0 - lam_init)


def _attn_prompt(q, k, v, lam_p, subln, lam_init, t):
    B, L, D = q.shape
    H = DIFF_HEADS
    W = D // H
    n = L // t
    pairs = [(qi, ki) for qi in range(n) for ki in range(qi + 1)]
    qi_tab = jnp.asarray(np.array([p[0] for p in pairs], np.int32))
    ki_tab = jnp.asarray(np.array([p[1] for p in pairs], np.int32))
    qs = min(ATTN_QSUB, t)
    assert t % qs == 0 and t % min(ATTN_KSUB, t) == 0
    kern = functools.partial(_attn_prompt_kernel, t=t, lam_init=lam_init)
    grid_spec = pltpu.PrefetchScalarGridSpec(
        num_scalar_prefetch=2,
        grid=(B, H, len(pairs)),
        in_specs=[pl.BlockSpec(lam_p.shape, lambda b, h, j, qt, kt: (0, 0)),
                  pl.BlockSpec((1, t, W), lambda b, h, j, qt, kt: (b, qt[j], h)),
                  pl.BlockSpec((1, t, W), lambda b, h, j, qt, kt: (b, kt[j], h)),
                  pl.BlockSpec((1, t, W), lambda b, h, j, qt, kt: (b, kt[j], h)),
                  pl.BlockSpec((1, W), lambda b, h, j, qt, kt: (0, 0))],
        out_specs=pl.BlockSpec((1, t, W), lambda b, h, j, qt, kt: (b, qt[j], h)),
        scratch_shapes=[pltpu.VMEM((2, t // qs, 1, qs), F32),
                        pltpu.VMEM((2, t // qs, 1, qs), F32),
                        pltpu.VMEM((2, t // qs, W, qs), F32)],
    )
    return pl.pallas_call(
        kern,
        grid_spec=grid_spec,
        out_shape=jax.ShapeDtypeStruct((B, L, D), F32),
        compiler_params=_params("parallel", "parallel", "arbitrary"),
        name="diff_attn_prompt",
    )(qi_tab, ki_tab, lam_p, q, k, v, subln.reshape(1, W))


def _attn_sample_kernel(pt_ref, lam_ref, qx_ref, *rest, G, n_groups, n_new, lam_init):
    del pt_ref
    ck_refs, cv_refs = rest[:G], rest[G:2 * G]
    kn_ref, vn_ref, subln_ref, o_ref, m_ref, l_ref, acc_ref = rest[2 * G:]
    H = DIFF_HEADS
    p = pl.program_id(1)
    R = qx_ref.shape[1]
    qb = qx_ref[0].astype(BF16)

    @pl.when(p == 0)
    def _():
        m_ref[...] = jnp.full(m_ref.shape, -jnp.inf, F32)
        l_ref[...] = jnp.zeros(l_ref.shape, F32)
        acc_ref[...] = jnp.zeros(acc_ref.shape, F32)

    def head_mask(width):
        rowi = lax.broadcasted_iota(jnp.int32, (R, width), 0)
        coli = lax.broadcasted_iota(jnp.int32, (R, width), 1)
        return rowi, coli, (coli & (H - 1)) == (rowi >> 3)

    def update(s_list, v_list):
        m_prev = m_ref[...]
        m_new = m_prev
        for s in s_list:
            m_new = jnp.maximum(m_new, jnp.max(s, axis=-1, keepdims=True))
        alpha = jnp.exp(m_prev - m_new)
        l_new = alpha * l_ref[...]
        acc = alpha * acc_ref[...]
        for s, v in zip(s_list, v_list):
            pr = jnp.exp(s - m_new)
            l_new = l_new + jnp.sum(pr, axis=-1, keepdims=True)
            acc = acc + _dot(pr, v)
        m_ref[...] = m_new
        l_ref[...] = l_new
        acc_ref[...] = acc

    def page_scores():
        _, _, hm = head_mask(ck_refs[0].shape[1])
        return [jnp.where(hm, _dot_nt(qb, r[0]), -jnp.inf) for r in ck_refs], [r[0] for r in cv_refs]

    if n_groups > 1:
        @pl.when(p < n_groups - 1)
        def _():
            update(*page_scores())

    @pl.when(p == n_groups - 1)
    def _():
        s_list, v_list = page_scores()
        rowi, coli, hm = head_mask(kn_ref.shape[1])
        causal = (coli >> 3) <= (rowi & (n_new - 1))
        s_new = jnp.where(jnp.logical_and(hm, causal), _dot_nt(qb, kn_ref[0]), -jnp.inf)
        update(s_list + [s_new], v_list + [vn_ref[0]])
        lam = _lambda(lam_ref, lam_init)
        o8 = acc_ref[...] / l_ref[...]
        for hd in range(H):
            blk = o8[hd * SUBLANES:(hd + 1) * SUBLANES]
            diff = blk - lam * pltpu.roll(blk, SUBLANES // 2, axis=0)
            o_ref[0, hd * SUBLANES:(hd + 1) * SUBLANES, :] = _rms(diff, subln_ref[...]) * (1.0 - lam_init)


def _attn_sample(qx, cache_k, cache_v, page_table, k_new, v_new, lam_p, subln, lam_init, n_new):
    B, R, W = qx.shape
    rows = cache_k.shape[1]
    n_pages = page_table.shape[1]
    G = math.gcd(n_pages, PAGES_PER_STEP)
    n_groups = n_pages // G
    kern = functools.partial(_attn_sample_kernel, G=G, n_groups=n_groups, n_new=n_new, lam_init=lam_init)

    def page_spec(j):
        return pl.BlockSpec((1, rows, W), lambda b, p, pt: (pt[b, p * G + j], 0, 0))

    def seq(r):
        return pl.BlockSpec((1, r, W), lambda b, p, pt: (b, 0, 0))

    grid_spec = pltpu.PrefetchScalarGridSpec(
        num_scalar_prefetch=1,
        grid=(B, n_groups),
        in_specs=([pl.BlockSpec(lam_p.shape, lambda b, p, pt: (0, 0)), seq(R)]
                  + [page_spec(j) for j in range(G)] * 2
                  + [seq(k_new.shape[1]), seq(v_new.shape[1]),
                     pl.BlockSpec((1, W), lambda b, p, pt: (0, 0))]),
        out_specs=seq(R),
        scratch_shapes=[pltpu.VMEM((R, 1), F32),
                        pltpu.VMEM((R, 1), F32),
                        pltpu.VMEM((R, W), F32)],
    )
    return pl.pallas_call(
        kern,
        grid_spec=grid_spec,
        out_shape=jax.ShapeDtypeStruct((B, R, W), F32),
        compiler_params=_params("parallel", "arbitrary"),
        name="diff_attn_sample",
    )(page_table, lam_p, qx, *([cache_k] * G), *([cache_v] * G), k_new, v_new, subln.reshape(1, W))


def _lambda_init(layer):
    return 0.8 - 0.6 * math.exp(-0.3 * layer)


def _pad_rows(x, rows):
    return jnp.pad(x, [(0, 0), (0, rows - x.shape[1])] + [(0, 0)] * (x.ndim - 2))


def kernel(x_prompt, x_sample, state_delta, state_conv, cache_k, cache_v, page_table, norm_a, w_in_a, conv_w_a, a_log, dt_bias, onorm_a, w_out_a, kv_norm, w_kv, norm_b, w_q_b, lam_b, subln_b, w_out_b, norm_m, w_rg, b_rg, w_re, b_re, w_gate_e, w_up_e, w_down_e, final_norm):
    D = x_prompt.shape[-1]
    H = GDN_HEADS
    CH = 3 * H * GDN_DK
    NH, W = DIFF_HEADS, 2 * DIFF_DH
    n_in = w_in_a.shape[-1]
    n_in_pad = -(-n_in // LANES) * LANES
    depth = norm_m.shape[0]
    n_a = norm_a.shape[0]
    assert depth == 2 and n_a == 1 and norm_b.shape[0] == 1

    w_in_bf = jnp.pad(w_in_a[0], ((0, 0), (0, n_in_pad - n_in))).astype(BF16)
    w_out_a_bf = w_out_a[0].astype(BF16)
    w_q_bf = w_q_b[0].astype(BF16)
    w_kv_bf = w_kv.astype(BF16)
    w_out_b_bf = w_out_b[0].astype(BF16)
    n_route = N_GROUPS + N_EXPERTS
    wr = jnp.pad(jnp.concatenate([w_rg, w_re], axis=-1), ((0, 0), (0, 0), (0, LANES - n_route)))
    br = jnp.pad(jnp.concatenate([b_rg, b_re], axis=-1), ((0, 0), (0, LANES - n_route))).reshape(depth, 1, LANES)
    alog_l = jnp.zeros((1, LANES), F32).at[0, H:2 * H].set(a_log[0])
    dtb_l = jnp.zeros((1, LANES), F32).at[0, H:2 * H].set(dt_bias[0])

    def gdn_layer(h, B, L, conv0, s0):
        T = B * L
        tm = min(256, T)
        proj = _norm_matmul(h, norm_a[0], w_in_bf, tm)
        C = min(GDN_CHUNK, -(-L // SUBLANES) * SUBLANES)
        Lp = -(-L // C) * C
        valid_last = L - (Lp - C)
        proj3 = _pad_rows(proj.reshape(B, L, n_in_pad), Lp)
        conv0p = jnp.pad(conv0, ((0, 0), (SUBLANES - (CONV_W - 1), 0), (0, 0)))
        bb = 2 if B % 2 == 0 else 1
        o3, s_new, cbuf = _gdn_mixer(proj3, conv0p, s0, conv_w_a[0], alog_l, dtb_l, onorm_a[0], C, valid_last, bb)
        h = _matmul_residual(o3[:, :L].reshape(T, D), w_out_a_bf, h, tm)
        return h, s_new[None], cbuf[:, SUBLANES - (CONV_W - 1):][None]

    def attn_layer(h, B, L, past_len, paged):
        T = B * L
        tm = min(256, T)
        pos = past_len + jnp.arange(L, dtype=jnp.int32)
        cos_t, sin_t = _rope_tables(pos)
        if L % tm != 0:
            cos_t = jnp.tile(cos_t, (T // L, 1))
            sin_t = jnp.tile(sin_t, (T // L, 1))
        q, k, v = _qkv_proj(h, norm_b[0], kv_norm, w_q_bf, w_kv_bf, cos_t, sin_t, tm)
        lam_init = _lambda_init(n_a)
        if paged is None:
            o = _attn_prompt(q.reshape(B, L, D), k.reshape(B, L, D), v.reshape(B, L, D),
                             lam_b[0], subln_b[0], lam_init, min(ATTN_TILE, L))
            o = o.reshape(T, D)
        else:
            ck, cv, pt = paged
            assert 2 * L == SUBLANES and NH == SUBLANES
            q5 = q.reshape(B, L, NH, 2, DIFF_DH).transpose(0, 2, 1, 3, 4)
            sel = jnp.eye(2, dtype=F32)[None, None, :, None, :, None]
            qx = (q5[:, :, None] * sel).reshape(B, NH * 2 * L, W)
            o = _attn_sample(qx, ck, cv, pt, k.reshape(B, L * NH, W), v.reshape(B, L * NH, W),
                             lam_b[0], subln_b[0], lam_init, L)
            o = o.reshape(B, NH, 2 * L, W)[:, :, :L].transpose(0, 2, 1, 3).reshape(T, D)
        h = _matmul_residual(o, w_out_b_bf, h, tm)
        kshape = (B, L, NH, W)
        return h, k.reshape(kshape), v.reshape(kshape)

    def moe(hp, hs, layer, last):
        return _hier_moe(hp, hs, norm_m[layer], wr[layer], br[layer], w_gate_e, w_up_e, w_down_e, layer,
                         final_norm, last)

    Bp, Lq, _ = x_prompt.shape
    Bs, Ls, _ = x_sample.shape
    hp = x_prompt.reshape(Bp * Lq, D)
    hs = x_sample.reshape(Bs * Ls, D)

    hp, sd_p, sc_p = gdn_layer(hp, Bp, Lq, jnp.zeros((Bp, CONV_W - 1, CH), F32),
                               jnp.zeros((Bp, H, GDN_DK, GDN_DK), F32))
    hs, sd_s, sc_s = gdn_layer(hs, Bs, Ls, state_conv[0], state_delta[0])
    hp, hs = moe(hp, hs, 0, False)

    n_phys, page, nh, w = cache_k.shape
    past_len = page_table.shape[1] * page
    ck = cache_k.reshape(n_phys, page * nh, w)
    cv = cache_v.reshape(n_phys, page * nh, w)
    hp, k_p, v_p = attn_layer(hp, Bp, Lq, 0, None)
    hs, k_s, v_s = attn_layer(hs, Bs, Ls, past_len, (ck, cv, page_table))
    y_p, y_s = moe(hp, hs, 1, True)
    return (y_p.reshape(Bp, Lq, D), y_s.reshape(Bs, Ls, D), sd_p, sc_p, k_p, v_p, sd_s, sc_s, k_s, v_s)
```

```python
import functools
import math

import numpy as np
import jax
import jax.numpy as jnp
from jax import lax
from jax.experimental import pallas as pl
from jax.experimental.pallas import tpu as pltpu

F32 = jnp.float32
BF16 = jnp.bfloat16
EPS = 1e-6
LANES = 128
SUBLANES = 8
VMEM_LIMIT = 56 * 1024 * 1024

CONV_W = 4
GDN_HEADS = 8
GDN_DK = 128
GDN_CHUNK = 64
DIFF_HEADS = 8
DIFF_DH = 64
ROPE_THETA = 10000.0
N_GROUPS = 4
EXP_PER_GROUP = 8
N_EXPERTS = N_GROUPS * EXP_PER_GROUP
TOP_K = 2
MOE_BLOCK = 128
ROUTE_OFF = N_GROUPS
ATTN_TILE = 2048
PAGES_PER_STEP = 16
ATTN_QSUB = LANES
ATTN_KSUB = 256
LOG2E = 1.4426950408889634
DMA_UNROLL = 8
ROUTER_TILE = 256
GATHER_AHEAD = 3
GATHER_SLOTS = GATHER_AHEAD + 1


def _params(*sem):
    return pltpu.CompilerParams(dimension_semantics=sem, vmem_limit_bytes=VMEM_LIMIT)


def _dot(a, b):
    return jnp.dot(a.astype(BF16), b.astype(BF16), preferred_element_type=F32)


def _dot_nt(a, b):
    return lax.dot_general(a.astype(BF16), b.astype(BF16), (((1,), (1,)), ((), ())),
                           preferred_element_type=F32)


def _dot_tn(a, b):
    return lax.dot_general(a.astype(BF16), b.astype(BF16), (((0,), (0,)), ((), ())),
                           preferred_element_type=F32)


def _split2(x):
    hi = x.astype(BF16)
    lo = (x - hi.astype(F32)).astype(BF16)
    return hi, lo


def _dot3(a, b):
    ah, al = _split2(a)
    bh, bl = _split2(b)
    d = functools.partial(jnp.dot, preferred_element_type=F32)
    return d(ah, bh) + (d(ah, bl) + d(al, bh))


def _dot_exact_lhs(a_bf, b):
    b1 = b.astype(BF16)
    r1 = b - b1.astype(F32)
    b2 = r1.astype(BF16)
    b3 = (r1 - b2.astype(F32)).astype(BF16)
    d = functools.partial(jnp.dot, preferred_element_type=F32)
    return d(a_bf, b1) + (d(a_bf, b2) + d(a_bf, b3))


def _sigmoid(x):
    return 1.0 / (1.0 + jnp.exp(-x))


def _silu(x):
    return x * _sigmoid(x)


def _softplus(x):
    return jnp.maximum(x, 0.0) + jnp.log1p(jnp.exp(-jnp.abs(x)))


def _rms(x, g):
    return x * lax.rsqrt(jnp.mean(x * x, axis=-1, keepdims=True) + EPS) * g


def _norm_mm_kernel(x_ref, g_ref, w_ref, o_ref):
    xn = _rms(x_ref[...], g_ref[...])
    o_ref[...] = jnp.dot(xn.astype(BF16), w_ref[...], preferred_element_type=F32)


def _norm_matmul(x, g, w_bf, tm):
    T, D = x.shape
    N = w_bf.shape[1]
    return pl.pallas_call(
        _norm_mm_kernel,
        grid=(T // tm,),
        in_specs=[pl.BlockSpec((tm, D), lambda i: (i, 0)),
                  pl.BlockSpec((1, D), lambda i: (0, 0)),
                  pl.BlockSpec((D, N), lambda i: (0, 0))],
        out_specs=pl.BlockSpec((tm, N), lambda i: (i, 0)),
        out_shape=jax.ShapeDtypeStruct((T, N), F32),
        compiler_params=_params("parallel"),
        name="norm_matmul",
    )(x, g.reshape(1, D), w_bf)


def _mm_res_kernel(x_ref, w_ref, h_ref, o_ref):
    o_ref[...] = h_ref[...] + jnp.dot(x_ref[...].astype(BF16), w_ref[...], preferred_element_type=F32)


def _matmul_residual(x, w_bf, h, tm):
    T, K = x.shape
    N = w_bf.shape[1]
    return pl.pallas_call(
        _mm_res_kernel,
        grid=(T // tm,),
        in_specs=[pl.BlockSpec((tm, K), lambda i: (i, 0)),
                  pl.BlockSpec((K, N), lambda i: (0, 0)),
                  pl.BlockSpec((tm, N), lambda i: (i, 0))],
        out_specs=pl.BlockSpec((tm, N), lambda i: (i, 0)),
        out_shape=jax.ShapeDtypeStruct((T, N), F32),
        compiler_params=_params("parallel"),
        name="matmul_residual",
    )(x, w_bf, h)


def _gdn_kernel(proj_ref, conv0_ref, s0_ref, cw_ref, alog_ref, dtb_ref, onorm_ref,
                o_ref, sout_ref, cout_ref, ext_ref, s_ref, *, C, nc, valid_last, bb):
    H, DK = GDN_HEADS, GDN_DK
    QK = H * DK
    CH = 3 * QK
    c = pl.program_id(1)
    masked = valid_last < C

    @pl.when(c == 0)
    def _():
        ext_ref[:, 0:SUBLANES, :] = conv0_ref[...]
        s_ref[...] = s0_ref[...]

    row1 = lax.broadcasted_iota(jnp.int32, (C, 1), 0)
    row = lax.broadcasted_iota(jnp.int32, (C, C), 0)
    col = lax.broadcasted_iota(jnp.int32, (C, C), 1)
    incl = row >= col
    strict = row > col
    eye = jnp.where(row == col, 1.0, 0.0).astype(F32)
    tril_bf = jnp.where(incl, 1.0, 0.0).astype(BF16)
    if masked:
        valid = jnp.logical_or(row1 < valid_last, c < nc - 1)

    seqs = []
    for b in range(bb):
        xin = proj_ref[b, :, 0:CH]
        ext_ref[b, SUBLANES:SUBLANES + C, :] = xin
        base = SUBLANES - (CONV_W - 1)
        acc = ext_ref[b, base:base + C, :] * cw_ref[0:1, :]
        for i in range(1, CONV_W - 1):
            acc = acc + ext_ref[b, base + i:base + i + C, :] * cw_ref[i:i + 1, :]
        acc = acc + xin * cw_ref[CONV_W - 1:CONV_W, :]
        qkv = _silu(acc)

        @pl.when(c == nc - 1)
        def _():
            cout_ref[b] = ext_ref[b, valid_last:valid_last + SUBLANES, :]

        ext_ref[b, 0:SUBLANES, :] = ext_ref[b, C:C + SUBLANES, :]

        ba = proj_ref[b, :, CH + QK:CH + QK + LANES]
        beta_t = _sigmoid(ba)
        g_t = -jnp.exp(alog_ref[...]) * _softplus(ba + dtb_ref[...])
        if masked:
            beta_t = jnp.where(valid, beta_t, 0.0)
            g_t = jnp.where(valid, g_t, 0.0)
        G = _dot_exact_lhs(tril_bf, g_t)
        g_last = G[C - 1:C, :]
        seqs.append(dict(qkv=qkv, beta=beta_t, G=G, GT=G.T, expG=jnp.exp(G),
                         exp_last=jnp.exp(g_last), k_scale=jnp.exp(g_last - G)))

    chains = [(b, hd) for b in range(bb) for hd in range(H)]
    st = []
    for b, hd in chains:
        sq = seqs[b]
        q = sq["qkv"][:, hd * DK:(hd + 1) * DK]
        k = sq["qkv"][:, QK + hd * DK:QK + (hd + 1) * DK]
        v = sq["qkv"][:, 2 * QK + hd * DK:2 * QK + (hd + 1) * DK]
        q = q * lax.rsqrt(jnp.sum(q * q, axis=-1, keepdims=True) + EPS) * (DK ** -0.5)
        k = k * lax.rsqrt(jnp.sum(k * k, axis=-1, keepdims=True) + EPS)
        if masked:
            k = jnp.where(valid, k, 0.0)
        bcol = sq["beta"][:, hd:hd + 1]
        gcol = sq["G"][:, H + hd:H + hd + 1]
        grow = sq["GT"][H + hd:H + hd + 1, :]
        eg = sq["expG"][:, H + hd:H + hd + 1]
        decay = jnp.exp(jnp.where(incl, gcol - grow, -jnp.inf))
        st.append(dict(q=q, k=k, decay=decay, eg=eg, bcol=bcol,
                       rhs=jnp.concatenate([bcol * v, (bcol * eg) * k], axis=1)))
    for d in st:
        kk = _dot_nt(d["k"], d["k"])
        d["p"] = -jnp.where(strict, d["bcol"] * kk * d["decay"], 0.0)
        d["t"] = eye + d["p"]
    for _ in range(int(math.log2(C)) - 1):
        for d in st:
            d["p"] = _dot3(d["p"], d["p"])
        for d in st:
            d["t"] = d["t"] + _dot3(d["t"], d["p"])
    for d in st:
        d["sol"] = _dot3(d["t"], d["rhs"])
    for d, (b, hd) in zip(st, chains):
        d["S"] = s_ref[b, hd]
        d["u"] = d["sol"][:, :DK] - _dot(d["sol"][:, DK:], d["S"])
    for d in st:
        d["qk"] = _dot_nt(d["q"], d["k"]) * d["decay"]
    for d in st:
        d["o"] = _dot(d["q"] * d["eg"], d["S"]) + _dot(d["qk"], d["u"])
    for d, (b, hd) in zip(st, chains):
        sq = seqs[b]
        s_ref[b, hd] = (sq["exp_last"][:, H + hd:H + hd + 1] * d["S"]
                        + _dot_tn(d["k"] * sq["k_scale"][:, H + hd:H + hd + 1], d["u"]))
    for d, (b, hd) in zip(st, chains):
        z = proj_ref[b, :, CH + hd * DK:CH + (hd + 1) * DK]
        o_ref[b, :, hd * DK:(hd + 1) * DK] = _rms(d["o"], onorm_ref[...]) * _silu(z)

    @pl.when(c == nc - 1)
    def _():
        sout_ref[...] = s_ref[...]


def _gdn_mixer(proj, conv0p, s0, conv_w, alog_l, dtb_l, onorm, C, valid_last, bb):
    B, Lp, NP = proj.shape
    nc = Lp // C
    H, DK = GDN_HEADS, GDN_DK
    D = H * DK
    CH = 3 * H * DK
    kern = functools.partial(_gdn_kernel, C=C, nc=nc, valid_last=valid_last, bb=bb)
    return pl.pallas_call(
        kern,
        grid=(B // bb, nc),
        in_specs=[pl.BlockSpec((bb, C, NP), lambda b, c: (b, c, 0)),
                  pl.BlockSpec((bb, SUBLANES, CH), lambda b, c: (b, 0, 0)),
                  pl.BlockSpec((bb, H, DK, DK), lambda b, c: (b, 0, 0, 0)),
                  pl.BlockSpec((CONV_W, CH), lambda b, c: (0, 0)),
                  pl.BlockSpec((1, LANES), lambda b, c: (0, 0)),
                  pl.BlockSpec((1, LANES), lambda b, c: (0, 0)),
                  pl.BlockSpec((1, DK), lambda b, c: (0, 0))],
        out_specs=[pl.BlockSpec((bb, C, D), lambda b, c: (b, c, 0)),
                   pl.BlockSpec((bb, H, DK, DK), lambda b, c: (b, 0, 0, 0)),
                   pl.BlockSpec((bb, SUBLANES, CH), lambda b, c: (b, 0, 0))],
        out_shape=[jax.ShapeDtypeStruct((B, Lp, D), F32),
                   jax.ShapeDtypeStruct((B, H, DK, DK), F32),
                   jax.ShapeDtypeStruct((B, SUBLANES, CH), F32)],
        scratch_shapes=[pltpu.VMEM((bb, SUBLANES + C, CH), F32),
                        pltpu.VMEM((bb, H, DK, DK), F32)],
        compiler_params=_params("parallel", "arbitrary"),
        name="gdn_mixer",
    )(proj, conv0p, s0, conv_w, alog_l, dtb_l, onorm.reshape(1, DK))


def _router_kernel(hp_ref, hs_ref, g_ref, wr_ref, br_ref, xn_ref, gate_ref, route_ref, cnt_ref, carry_ref,
                   *, n_p):
    i = pl.program_id(0)

    @pl.when(i == 0)
    def _():
        carry_ref[...] = jnp.zeros(carry_ref.shape, F32)

    x = jnp.where(i < n_p, hp_ref[...], hs_ref[...])
    xn = _rms(x, g_ref[...])
    xn_ref[...] = xn
    logits = _dot3(xn, wr_ref[...]) + br_ref[...]
    tm = logits.shape[0]
    lane = lax.broadcasted_iota(jnp.int32, (tm, LANES), 1).astype(F32)
    neg = -jnp.inf
    big = float(LANES)

    lg = jnp.where(lane < N_GROUPS, logits, neg)
    eg = jnp.exp(lg - jnp.max(lg, axis=-1, keepdims=True))
    pg = eg / jnp.sum(eg, axis=-1, keepdims=True)
    pmax = jnp.max(pg, axis=-1, keepdims=True)
    g_sel = jnp.min(jnp.where(pg == pmax, lane, big), axis=-1, keepdims=True)

    lo = ROUTE_OFF + EXP_PER_GROUP * g_sel
    in_grp = jnp.logical_and(lane >= lo, lane < lo + EXP_PER_GROUP)
    le = jnp.where(in_grp, logits, neg)
    ee = jnp.exp(le - jnp.max(le, axis=-1, keepdims=True))
    pe = ee / jnp.sum(ee, axis=-1, keepdims=True)
    pe = jnp.where(in_grp, pe, -1.0)
    p1 = jnp.max(pe, axis=-1, keepdims=True)
    i1 = jnp.min(jnp.where(pe == p1, lane, big), axis=-1, keepdims=True)
    pe2 = jnp.where(lane == i1, -1.0, pe)
    p2 = jnp.max(pe2, axis=-1, keepdims=True)
    i2 = jnp.min(jnp.where(pe2 == p2, lane, big), axis=-1, keepdims=True)
    den = p1 + p2
    gate_ref[...] = jnp.where(lane == 0, pmax * p1 / den, jnp.where(lane == 1, pmax * p2 / den, 0.0))

    e1 = i1 - ROUTE_OFF
    e2 = i2 - ROUTE_OFF
    oh1 = jnp.where(lane == e1, 1.0, 0.0)
    oh2 = jnp.where(lane == e2, 1.0, 0.0)
    both = oh1 + oh2
    earlier = (lax.broadcasted_iota(jnp.int32, (tm, tm), 0) > lax.broadcasted_iota(jnp.int32, (tm, tm), 1))
    before = (jnp.dot(jnp.where(earlier, 1.0, 0.0).astype(BF16), both.astype(BF16), preferred_element_type=F32)
              + carry_ref[0:1, :])
    r1 = jnp.sum(oh1 * before, axis=-1, keepdims=True)
    r2 = jnp.sum(oh2 * before, axis=-1, keepdims=True)
    carry_ref[...] = carry_ref[...] + jnp.sum(both, axis=0, keepdims=True)
    cnt_ref[...] = carry_ref[...]
    route_ref[...] = jnp.where(lane == 0, e1, jnp.where(lane == 1, e2, jnp.where(
        lane == 2, r1, jnp.where(lane == 3, r2, 0.0)))).astype(jnp.int32)


def _router(hp, hs, g, wr, br, tm):
    Tp, D = hp.shape
    Ts = hs.shape[0]
    n_p, n_s = Tp // tm, Ts // tm
    T = Tp + Ts
    kern = functools.partial(_router_kernel, n_p=n_p)
    row = pl.BlockSpec((tm, D), lambda i: (i, 0))
    lanes = pl.BlockSpec((tm, LANES), lambda i: (i, 0))
    return pl.pallas_call(
        kern,
        grid=(n_p + n_s,),
        in_specs=[pl.BlockSpec((tm, D), lambda i: (jnp.minimum(i, n_p - 1), 0)),
                  pl.BlockSpec((tm, D), lambda i: (jnp.maximum(i - n_p, 0), 0)),
                  pl.BlockSpec((1, D), lambda i: (0, 0)),
                  pl.BlockSpec((D, LANES), lambda i: (0, 0)),
                  pl.BlockSpec((1, LANES), lambda i: (0, 0))],
        out_specs=[row, lanes, lanes, pl.BlockSpec((SUBLANES, LANES), lambda i: (0, 0))],
        out_shape=[jax.ShapeDtypeStruct((T, D), F32),
                   jax.ShapeDtypeStruct((T, LANES), F32),
                   jax.ShapeDtypeStruct((T, LANES), jnp.int32),
                   jax.ShapeDtypeStruct((SUBLANES, LANES), F32)],
        scratch_shapes=[pltpu.VMEM((SUBLANES, LANES), F32)],
        compiler_params=_params("arbitrary"),
        name="moe_router",
    )(hp, hs, g.reshape(1, D), wr, br)


def _row_copy(src_hbm, row, dst, slot_idx, sem):
    return pltpu.make_async_copy(src_hbm.at[pl.ds(row, 1)], dst.at[slot_idx], sem)


def _expert_kernel(be_ref, rt_ref, nu_ref, xn_hbm, wg_ref, wu_ref, wd_ref, ys_ref,
                   xbuf, wg_bf, wu_bf, wd_bf, sem):
    i = pl.program_id(0)
    n_used = nu_ref[0]

    def issue(blk, slot):
        for r in range(MOE_BLOCK):
            _row_copy(xn_hbm, rt_ref[blk * MOE_BLOCK + r], xbuf, (slot, pl.ds(r, 1)), sem.at[slot]).start()

    def wait(slot):
        for r in range(MOE_BLOCK):
            _row_copy(xn_hbm, 0, xbuf, (slot, pl.ds(r, 1)), sem.at[slot]).wait()

    def issue_pos(pos):
        issue(jnp.minimum(pos, n_used - 1), pos % GATHER_SLOTS)

    @pl.when(i == 0)
    def _():
        for k in range(GATHER_AHEAD):
            issue_pos(jnp.int32(k))

    @pl.when(jnp.logical_or(i == 0, be_ref[i] != be_ref[jnp.maximum(i - 1, 0)]))
    def _():
        wg_bf[...] = wg_ref[0, 0].astype(BF16)
        wu_bf[...] = wu_ref[0, 0].astype(BF16)
        wd_bf[...] = wd_ref[0, 0].astype(BF16)

    @pl.when(i < n_used)
    def _():
        slot = i % GATHER_SLOTS
        wait(slot)
        issue_pos(i + GATHER_AHEAD)
        x = xbuf[slot].astype(BF16)
        gt = jnp.dot(x, wg_bf[...], preferred_element_type=F32)
        up = jnp.dot(x, wu_bf[...], preferred_element_type=F32)
        hid = _silu(gt) * up
        ys_ref[...] = jnp.dot(hid.astype(BF16), wd_bf[...], preferred_element_type=F32)

    @pl.when(i == n_used - 1)
    def _():
        for k in range(GATHER_AHEAD):
            wait((n_used + k) % GATHER_SLOTS)

    @pl.when(i >= n_used)
    def _():
        ys_ref[...] = jnp.zeros(ys_ref.shape, F32)


def _experts(xn, block_e, row_tok, n_used, w_gate, w_up, w_down, layer):
    T, D = xn.shape
    FF = w_gate.shape[-1]
    n_blocks = block_e.shape[0]
    P = n_blocks * MOE_BLOCK
    grid_spec = pltpu.PrefetchScalarGridSpec(
        num_scalar_prefetch=3,
        grid=(n_blocks,),
        in_specs=[pl.BlockSpec(memory_space=pl.ANY),
                  pl.BlockSpec((1, 1, D, FF), lambda i, be, rt, nu: (layer, be[i], 0, 0)),
                  pl.BlockSpec((1, 1, D, FF), lambda i, be, rt, nu: (layer, be[i], 0, 0)),
                  pl.BlockSpec((1, 1, FF, D), lambda i, be, rt, nu: (layer, be[i], 0, 0))],
        out_specs=pl.BlockSpec((MOE_BLOCK, D), lambda i, be, rt, nu: (i, 0)),
        scratch_shapes=[pltpu.VMEM((GATHER_SLOTS, MOE_BLOCK, D), F32),
                        pltpu.VMEM((D, FF), BF16),
                        pltpu.VMEM((D, FF), BF16),
                        pltpu.VMEM((FF, D), BF16),
                        pltpu.SemaphoreType.DMA((GATHER_SLOTS,))],
    )
    return pl.pallas_call(
        _expert_kernel,
        grid_spec=grid_spec,
        out_shape=jax.ShapeDtypeStruct((P, D), F32),
        compiler_params=_params("arbitrary"),
        name="moe_experts",
    )(block_e, row_tok, n_used, xn, w_gate, w_up, w_down)


def _combine_kernel(dest_ref, ys_hbm, hp_ref, hs_ref, gate_ref, fn_ref, op_ref, os_ref, ybuf, sem,
                    *, tm, n_p, n_steps, final_norm):
    i = pl.program_id(0)
    n = pl.num_programs(0)

    def issue(step, slot):
        def body(r, carry):
            for kk in range(TOP_K):
                d = dest_ref[(step * tm + r) * TOP_K + kk]
                _row_copy(ys_hbm, d, ybuf, (slot, kk, pl.ds(r, 1)), sem.at[slot]).start()
            return carry
        lax.fori_loop(0, tm, body, 0, unroll=DMA_UNROLL // TOP_K)

    def wait(slot):
        for r in range(tm):
            for kk in range(TOP_K):
                _row_copy(ys_hbm, 0, ybuf, (slot, kk, pl.ds(r, 1)), sem.at[slot]).wait()

    @pl.when(i == 0)
    def _():
        for k in range(min(GATHER_AHEAD, n_steps)):
            issue(k, k)

    @pl.when(i + GATHER_AHEAD < n)
    def _():
        issue(i + GATHER_AHEAD, (i + GATHER_AHEAD) % GATHER_SLOTS)

    slot = i % GATHER_SLOTS
    wait(slot)
    gate = gate_ref[...]
    moe = ybuf[slot, 0] * gate[:, 0:1] + ybuf[slot, 1] * gate[:, 1:2]

    def finish(h_ref, o_ref):
        y = h_ref[...] + moe
        if final_norm:
            y = _rms(y, fn_ref[...])
        o_ref[...] = y

    @pl.when(i < n_p)
    def _():
        finish(hp_ref, op_ref)

    @pl.when(i >= n_p)
    def _():
        finish(hs_ref, os_ref)


def _combine(ys, dest, hp, hs, gates, fnorm, tm, final_norm):
    Tp, D = hp.shape
    Ts = hs.shape[0]
    n_p, n_s = Tp // tm, Ts // tm
    kern = functools.partial(_combine_kernel, tm=tm, n_p=n_p, n_steps=n_p + n_s, final_norm=final_norm)
    p_blk = pl.BlockSpec((tm, D), lambda i, d: (jnp.minimum(i, n_p - 1), 0))
    s_blk = pl.BlockSpec((tm, D), lambda i, d: (jnp.maximum(i - n_p, 0), 0))
    grid_spec = pltpu.PrefetchScalarGridSpec(
        num_scalar_prefetch=1,
        grid=(n_p + n_s,),
        in_specs=[pl.BlockSpec(memory_space=pl.ANY), p_blk, s_blk,
                  pl.BlockSpec((tm, LANES), lambda i, d: (i, 0)),
                  pl.BlockSpec((1, D), lambda i, d: (0, 0))],
        out_specs=[p_blk, s_blk],
        scratch_shapes=[pltpu.VMEM((GATHER_SLOTS, TOP_K, tm, D), F32),
                        pltpu.SemaphoreType.DMA((GATHER_SLOTS,))],
    )
    return pl.pallas_call(
        kern,
        grid_spec=grid_spec,
        out_shape=[jax.ShapeDtypeStruct((Tp, D), F32), jax.ShapeDtypeStruct((Ts, D), F32)],
        compiler_params=_params("arbitrary"),
        name="moe_combine",
    )(dest, ys, hp, hs, gates, fnorm.reshape(1, D))


def _route_tables(route, counts):
    T = route.shape[0]
    A = T * TOP_K
    flat_e = route[:, 0:TOP_K].reshape(A)
    rank = route[:, TOP_K:2 * TOP_K].reshape(A)
    padded = (counts + MOE_BLOCK - 1) // MOE_BLOCK * MOE_BLOCK
    pad_end = jnp.cumsum(padded)
    pad_start = pad_end - padded
    dest = (pad_start[flat_e] + rank).astype(jnp.int32)
    n_blocks = -(-A // MOE_BLOCK) + N_EXPERTS
    P = n_blocks * MOE_BLOCK
    row_tok = jnp.zeros((P,), jnp.int32).at[dest].set(jnp.arange(A, dtype=jnp.int32) // TOP_K)
    starts = jnp.arange(n_blocks, dtype=jnp.int32) * MOE_BLOCK
    block_e = jnp.minimum(jnp.sum((pad_end[None, :] <= starts[:, None]).astype(jnp.int32), axis=1),
                          N_EXPERTS - 1)
    n_used = (pad_end[-1] // MOE_BLOCK).astype(jnp.int32).reshape(1)
    return dest, row_tok, block_e, n_used


def _hier_moe(hp, hs, norm_g, wr, br, w_gate, w_up, w_down, layer, fnorm, final_norm):
    t_all = math.gcd(hp.shape[0], hs.shape[0])
    xn, gates, route, cnt = _router(hp, hs, norm_g, wr, br, math.gcd(ROUTER_TILE, t_all))
    counts = cnt[0, :N_EXPERTS].astype(jnp.int32)
    dest, row_tok, block_e, n_used = _route_tables(route, counts)
    ys = _experts(xn, block_e, row_tok, n_used, w_gate, w_up, w_down, layer)
    return _combine(ys, dest, hp, hs, gates, fnorm, math.gcd(MOE_BLOCK, t_all), final_norm)


def _rope_tile(x, cos, sin_signed, lane):
    half = DIFF_DH // 2
    rot = jnp.where((lane & (DIFF_DH - 1)) < half,
                    pltpu.roll(x, LANES - half, axis=1),
                    pltpu.roll(x, half, axis=1))
    return x * cos + rot * sin_signed


def _qkv_kernel(h_ref, gq_ref, gkv_ref, wq_ref, wkv_ref, cos_ref, sin_ref, q_ref, k_ref, v_ref):
    x = h_ref[...]
    D = x.shape[1]
    xs = x * lax.rsqrt(jnp.mean(x * x, axis=-1, keepdims=True) + EPS)
    q = jnp.dot((xs * gq_ref[...]).astype(BF16), wq_ref[...], preferred_element_type=F32)
    kv = jnp.dot((xs * gkv_ref[...]).astype(BF16), wkv_ref[...], preferred_element_type=F32)
    cos = cos_ref[...]
    sin = sin_ref[...]
    lane = lax.broadcasted_iota(jnp.int32, cos.shape, 1)
    for hd in range(D // LANES):
        sl = slice(hd * LANES, (hd + 1) * LANES)
        q_ref[:, sl] = _rope_tile(q[:, sl], cos, sin, lane) * (DIFF_DH ** -0.5)
        k_ref[:, sl] = _rope_tile(kv[:, sl], cos, sin, lane)
    v_ref[...] = kv[:, D:]


def _qkv_proj(h, gq, gkv, wq_bf, wkv_bf, cos_t, sin_t, tm):
    T, D = h.shape
    nt = cos_t.shape[0] // tm
    row = pl.BlockSpec((tm, D), lambda i: (i, 0))
    vec = pl.BlockSpec((1, D), lambda i: (0, 0))
    tab = pl.BlockSpec((tm, LANES), lambda i: (i % nt, 0))
    return pl.pallas_call(
        _qkv_kernel,
        grid=(T // tm,),
        in_specs=[row, vec, vec,
                  pl.BlockSpec((D, D), lambda i: (0, 0)),
                  pl.BlockSpec((D, 2 * D), lambda i: (0, 0)),
                  tab, tab],
        out_specs=[row, row, row],
        out_shape=[jax.ShapeDtypeStruct((T, D), F32)] * 3,
        compiler_params=_params("parallel"),
        name="qkv_proj",
    )(h, gq.reshape(1, D), gkv.reshape(1, D), wq_bf, wkv_bf, cos_t, sin_t)


def _rope_tables(pos):
    half = DIFF_DH // 2
    inv = ROPE_THETA ** (-jnp.arange(half, dtype=F32) / half)
    ang = pos.astype(F32)[:, None] * inv[None, :]
    cos = jnp.cos(ang)
    sin = jnp.sin(ang)
    reps = LANES // DIFF_DH
    return (jnp.tile(jnp.concatenate([cos, cos], axis=1), (1, reps)),
            jnp.tile(jnp.concatenate([-sin, sin], axis=1), (1, reps)))


def _lambda(lam_ref, lam_init):
    lp = lam_ref[...]
    a = jnp.sum(lp[0:1, :] * lp[1:2, :], axis=-1, keepdims=True)
    b = jnp.sum(lp[2:3, :] * lp[3:4, :], axis=-1, keepdims=True)
    return jnp.exp(a) - jnp.exp(b) + lam_init


def _attn_prompt_kernel(qi_ref, ki_ref, lam_ref, q_ref, k_ref, v_ref, subln_ref, o_ref,
                        m_ref, l_ref, acc_ref, *, t, lam_init):
    j = pl.program_id(2)
    qi = qi_ref[j]
    ki = ki_ref[j]

    @pl.when(ki == 0)
    def _():
        m_ref[...] = jnp.full(m_ref.shape, -jnp.inf, F32)
        l_ref[...] = jnp.zeros(l_ref.shape, F32)
        acc_ref[...] = jnp.zeros(acc_ref.shape, F32)

    QS, KS = min(ATTN_QSUB, t), min(ATTN_KSUB, t)

    def update(diagonal):
        q = q_ref[0] * LOG2E
        kb = k_ref[0].astype(BF16)
        vT = v_ref[0].T.astype(BF16)
        lane = lax.broadcasted_iota(jnp.int32, q.shape, 1)
        for c in range(2):
            qc = jnp.where((lane < DIFF_DH) if c == 0 else (lane >= DIFF_DH), q, 0.0).astype(BF16)
            for qs in range(t // QS):
                qsub = qc[qs * QS:(qs + 1) * QS]
                m = m_ref[c, qs]
                l = l_ref[c, qs]
                acc = acc_ref[c, qs]
                for ks in range(t // KS):
                    if diagonal and ks * KS > qs * QS + QS - 1:
                        continue
                    sT = _dot_nt(kb[ks * KS:(ks + 1) * KS], qsub)
                    if diagonal and (ks + 1) * KS - 1 > qs * QS:
                        keyi = ks * KS + lax.broadcasted_iota(jnp.int32, (KS, QS), 0)
                        qidx = qs * QS + lax.broadcasted_iota(jnp.int32, (KS, QS), 1)
                        sT = jnp.where(keyi <= qidx, sT, -jnp.inf)
                    m_new = jnp.maximum(m, jnp.max(sT, axis=0, keepdims=True))
                    alpha = jnp.exp2(m - m_new)
                    p = jnp.exp2(sT - m_new)
                    l = alpha * l + jnp.sum(p, axis=0, keepdims=True)
                    acc = alpha * acc + jnp.dot(vT[:, ks * KS:(ks + 1) * KS], p.astype(BF16),
                                                preferred_element_type=F32)
                    m = m_new
                m_ref[c, qs] = m
                l_ref[c, qs] = l
                acc_ref[c, qs] = acc

    @pl.when(ki < qi)
    def _():
        update(False)

    @pl.when(ki == qi)
    def _():
        update(True)
        lam = _lambda(lam_ref, lam_init)
        for qs in range(t // QS):
            oT = acc_ref[0, qs] / l_ref[0, qs] - lam * (acc_ref[1, qs] / l_ref[1, qs])
            o_ref[0, qs * QS:(qs + 1) * QS, :] = _rms(oT.T, subln_ref[...]) * (1.0 - lam_init)


def _attn_prompt(q, k, v, lam_p, subln, lam_init, t):
    B, L, D = q.shape
    H = DIFF_HEADS
    W = D // H
    n = L // t
    pairs = [(qi, ki) for qi in range(n) for ki in range(qi + 1)]
    qi_tab = jnp.asarray(np.array([p[0] for p in pairs], np.int32))
    ki_tab = jnp.asarray(np.array([p[1] for p in pairs], np.int32))
    qs = min(ATTN_QSUB, t)
    assert t % qs == 0 and t % min(ATTN_KSUB, t) == 0
    kern = functools.partial(_attn_prompt_kernel, t=t, lam_init=lam_init)
    grid_spec = pltpu.PrefetchScalarGridSpec(
        num_scalar_prefetch=2,
        grid=(B, H, len(pairs)),
        in_specs=[pl.BlockSpec(lam_p.shape, lambda b, h, j, qt, kt: (0, 0)),
                  pl.BlockSpec((1, t, W), lambda b, h, j, qt, kt: (b, qt[j], h)),
                  pl.BlockSpec((1, t, W), lambda b, h, j, qt, kt: (b, kt[j], h)),
                  pl.BlockSpec((1, t, W), lambda b, h, j, qt, kt: (b, kt[j], h)),
                  pl.BlockSpec((1, W), lambda b, h, j, qt, kt: (0, 0))],
        out_specs=pl.BlockSpec((1, t, W), lambda b, h, j, qt, kt: (b, qt[j], h)),
        scratch_shapes=[pltpu.VMEM((2, t // qs, 1, qs), F32),
                        pltpu.VMEM((2, t // qs, 1, qs), F32),
                        pltpu.VMEM((2, t // qs, W, qs), F32)],
    )
    return pl.pallas_call(
        kern,
        grid_spec=grid_spec,
        out_shape=jax.ShapeDtypeStruct((B, L, D), F32),
        compiler_params=_params("parallel", "parallel", "arbitrary"),
        name="diff_attn_prompt",
    )(qi_tab, ki_tab, lam_p, q, k, v, subln.reshape(1, W))


def _attn_sample_kernel(pt_ref, lam_ref, qx_ref, *rest, G, n_groups, n_new, lam_init):
    del pt_ref
    ck_refs, cv_refs = rest[:G], rest[G:2 * G]
    kn_ref, vn_ref, subln_ref, o_ref, m_ref, l_ref, acc_ref = rest[2 * G:]
    H = DIFF_HEADS
    p = pl.program_id(1)
    R = qx_ref.shape[1]
    qb = qx_ref[0].astype(BF16)

    @pl.when(p == 0)
    def _():
        m_ref[...] = jnp.full(m_ref.shape, -jnp.inf, F32)
        l_ref[...] = jnp.zeros(l_ref.shape, F32)
        acc_ref[...] = jnp.zeros(acc_ref.shape, F32)

    def head_mask(width):
        rowi = lax.broadcasted_iota(jnp.int32, (R, width), 0)
        coli = lax.broadcasted_iota(jnp.int32, (R, width), 1)
        return rowi, coli, (coli & (H - 1)) == (rowi >> 3)

    def update(s_list, v_list):
        m_prev = m_ref[...]
        m_new = m_prev
        for s in s_list:
            m_new = jnp.maximum(m_new, jnp.max(s, axis=-1, keepdims=True))
        alpha = jnp.exp(m_prev - m_new)
        l_new = alpha * l_ref[...]
        acc = alpha * acc_ref[...]
        for s, v in zip(s_list, v_list):
            pr = jnp.exp(s - m_new)
            l_new = l_new + jnp.sum(pr, axis=-1, keepdims=True)
            acc = acc + _dot(pr, v)
        m_ref[...] = m_new
        l_ref[...] = l_new
        acc_ref[...] = acc

    def page_scores():
        _, _, hm = head_mask(ck_refs[0].shape[1])
        return [jnp.where(hm, _dot_nt(qb, r[0]), -jnp.inf) for r in ck_refs], [r[0] for r in cv_refs]

    if n_groups > 1:
        @pl.when(p < n_groups - 1)
        def _():
            update(*page_scores())

    @pl.when(p == n_groups - 1)
    def _():
        s_list, v_list = page_scores()
        rowi, coli, hm = head_mask(kn_ref.shape[1])
        causal = (coli >> 3) <= (rowi & (n_new - 1))
        s_new = jnp.where(jnp.logical_and(hm, causal), _dot_nt(qb, kn_ref[0]), -jnp.inf)
        update(s_list + [s_new], v_list + [vn_ref[0]])
        lam = _lambda(lam_ref, lam_init)
        o8 = acc_ref[...] / l_ref[...]
        for hd in range(H):
            blk = o8[hd * SUBLANES:(hd + 1) * SUBLANES]
            diff = blk - lam * pltpu.roll(blk, SUBLANES // 2, axis=0)
            o_ref[0, hd * SUBLANES:(hd + 1) * SUBLANES, :] = _rms(diff, subln_ref[...]) * (1.0 - lam_init)


def _attn_sample(qx, cache_k, cache_v, page_table, k_new, v_new, lam_p, subln, lam_init, n_new):
    B, R, W = qx.shape
    rows = cache_k.shape[1]
    n_pages = page_table.shape[1]
    G = math.gcd(n_pages, PAGES_PER_STEP)
    n_groups = n_pages // G
    kern = functools.partial(_attn_sample_kernel, G=G, n_groups=n_groups, n_new=n_new, lam_init=lam_init)

    def page_spec(j):
        return pl.BlockSpec((1, rows, W), lambda b, p, pt: (pt[b, p * G + j], 0, 0))

    def seq(r):
        return pl.BlockSpec((1, r, W), lambda b, p, pt: (b, 0, 0))

    grid_spec = pltpu.PrefetchScalarGridSpec(
        num_scalar_prefetch=1,
        grid=(B, n_groups),
        in_specs=([pl.BlockSpec(lam_p.shape, lambda b, p, pt: (0, 0)), seq(R)]
                  + [page_spec(j) for j in range(G)] * 2
                  + [seq(k_new.shape[1]), seq(v_new.shape[1]),
                     pl.BlockSpec((1, W), lambda b, p, pt: (0, 0))]),
        out_specs=seq(R),
        scratch_shapes=[pltpu.VMEM((R, 1), F32),
                        pltpu.VMEM((R, 1), F32),
                        pltpu.VMEM((R, W), F32)],
    )
    return pl.pallas_call(
        kern,
        grid_spec=grid_spec,
        out_shape=jax.ShapeDtypeStruct((B, R, W), F32),
        compiler_params=_params("parallel", "arbitrary"),
        name="diff_attn_sample",
    )(page_table, lam_p, qx, *([cache_k] * G), *([cache_v] * G), k_new, v_new, subln.reshape(1, W))


def _lambda_init(layer):
    return 0.8 - 0.6 * math.exp(-0.3 * layer)


def _pad_rows(x, rows):
    return jnp.pad(x, [(0, 0), (0, rows - x.shape[1])] + [(0, 0)] * (x.ndim - 2))


def kernel(x_prompt, x_sample, state_delta, state_conv, cache_k, cache_v, page_table, norm_a, w_in_a, conv_w_a, a_log, dt_bias, onorm_a, w_out_a, kv_norm, w_kv, norm_b, w_q_b, lam_b, subln_b, w_out_b, norm_m, w_rg, b_rg, w_re, b_re, w_gate_e, w_up_e, w_down_e, final_norm):
    D = x_prompt.shape[-1]
    H = GDN_HEADS
    CH = 3 * H * GDN_DK
    NH, W = DIFF_HEADS, 2 * DIFF_DH
    n_in = w_in_a.shape[-1]
    n_in_pad = -(-n_in // LANES) * LANES
    depth = norm_m.shape[0]
    n_a = norm_a.shape[0]
    assert depth == 2 and n_a == 1 and norm_b.shape[0] == 1

    w_in_bf = jnp.pad(w_in_a[0], ((0, 0), (0, n_in_pad - n_in))).astype(BF16)
    w_out_a_bf = w_out_a[0].astype(BF16)
    w_q_bf = w_q_b[0].astype(BF16)
    w_kv_bf = w_kv.astype(BF16)
    w_out_b_bf = w_out_b[0].astype(BF16)
    n_route = N_GROUPS + N_EXPERTS
    wr = jnp.pad(jnp.concatenate([w_rg, w_re], axis=-1), ((0, 0), (0, 0), (0, LANES - n_route)))
    br = jnp.pad(jnp.concatenate([b_rg, b_re], axis=-1), ((0, 0), (0, LANES - n_route))).reshape(depth, 1, LANES)
    alog_l = jnp.zeros((1, LANES), F32).at[0, H:2 * H].set(a_log[0])
    dtb_l = jnp.zeros((1, LANES), F32).at[0, H:2 * H].set(dt_bias[0])

    def gdn_layer(h, B, L, conv0, s0):
        T = B * L
        tm = min(256, T)
        proj = _norm_matmul(h, norm_a[0], w_in_bf, tm)
        C = min(GDN_CHUNK, -(-L // SUBLANES) * SUBLANES)
        Lp = -(-L // C) * C
        valid_last = L - (Lp - C)
        proj3 = _pad_rows(proj.reshape(B, L, n_in_pad), Lp)
        conv0p = jnp.pad(conv0, ((0, 0), (SUBLANES - (CONV_W - 1), 0), (0, 0)))
        bb = next(n for n in ((4, 2, 1) if C < GDN_CHUNK else (2, 1)) if B % n == 0)
        o3, s_new, cbuf = _gdn_mixer(proj3, conv0p, s0, conv_w_a[0], alog_l, dtb_l, onorm_a[0], C, valid_last, bb)
        h = _matmul_residual(o3[:, :L].reshape(T, D), w_out_a_bf, h, tm)
        return h, s_new[None], cbuf[:, SUBLANES - (CONV_W - 1):][None]

    def attn_layer(h, B, L, past_len, paged):
        T = B * L
        tm = min(256, T)
        pos = past_len + jnp.arange(L, dtype=jnp.int32)
        cos_t, sin_t = _rope_tables(pos)
        if L % tm != 0:
            cos_t = jnp.tile(cos_t, (T // L, 1))
            sin_t = jnp.tile(sin_t, (T // L, 1))
        q, k, v = _qkv_proj(h, norm_b[0], kv_norm, w_q_bf, w_kv_bf, cos_t, sin_t, tm)
        lam_init = _lambda_init(n_a)
        if paged is None:
            o = _attn_prompt(q.reshape(B, L, D), k.reshape(B, L, D), v.reshape(B, L, D),
                             lam_b[0], subln_b[0], lam_init, min(ATTN_TILE, L))
            o = o.reshape(T, D)
        else:
            ck, cv, pt = paged
            assert 2 * L == SUBLANES and NH == SUBLANES
            q5 = q.reshape(B, L, NH, 2, DIFF_DH).transpose(0, 2, 1, 3, 4)
            sel = jnp.eye(2, dtype=F32)[None, None, :, None, :, None]
            qx = (q5[:, :, None] * sel).reshape(B, NH * 2 * L, W)
            o = _attn_sample(qx, ck, cv, pt, k.reshape(B, L * NH, W), v.reshape(B, L * NH, W),
                             lam_b[0], subln_b[0], lam_init, L)
            o = o.reshape(B, NH, 2 * L, W)[:, :, :L].transpose(0, 2, 1, 3).reshape(T, D)
        h = _matmul_residual(o, w_out_b_bf, h, tm)
        kshape = (B, L, NH, W)
        return h, k.reshape(kshape), v.reshape(kshape)

    def moe(hp, hs, layer, last):
        return _hier_moe(hp, hs, norm_m[layer], wr[layer], br[layer], w_gate_e, w_up_e, w_down_e, layer,
                         final_norm, last)

    Bp, Lq, _ = x_prompt.shape
    Bs, Ls, _ = x_sample.shape
    hp = x_prompt.reshape(Bp * Lq, D)
    hs = x_sample.reshape(Bs * Ls, D)

    hp, sd_p, sc_p = gdn_layer(hp, Bp, Lq, jnp.zeros((Bp, CONV_W - 1, CH), F32),
                               jnp.zeros((Bp, H, GDN_DK, GDN_DK), F32))
    hs, sd_s, sc_s = gdn_layer(hs, Bs, Ls, state_conv[0], state_delta[0])
    hp, hs = moe(hp, hs, 0, False)

    n_phys, page, nh, w = cache_k.shape
    past_len = page_table.shape[1] * page
    ck = cache_k.reshape(n_phys, page * nh, w)
    cv = cache_v.reshape(n_phys, page * nh, w)
    hp, k_p, v_p = attn_layer(hp, Bp, Lq, 0, None)
    hs, k_s, v_s = attn_layer(hs, Bs, Ls, past_len, (ck, cv, page_table))
    y_p, y_s = moe(hp, hs, 1, True)
    return (y_p.reshape(Bp, Lq, D), y_s.reshape(Bs, Ls, D), sd_p, sc_p, k_p, v_p, sd_s, sc_s, k_s, v_s)
```

```python
import functools
import math

import numpy as np
import jax
import jax.numpy as jnp
from jax import lax
from jax.experimental import pallas as pl
from jax.experimental.pallas import tpu as pltpu

F32 = jnp.float32
BF16 = jnp.bfloat16
EPS = 1e-6
LANES = 128
SUBLANES = 8
VMEM_LIMIT = 56 * 1024 * 1024

CONV_W = 4
GDN_HEADS = 8
GDN_DK = 128
GDN_CHUNK = 64
DIFF_HEADS = 8
DIFF_DH = 64
ROPE_THETA = 10000.0
N_GROUPS = 4
EXP_PER_GROUP = 8
N_EXPERTS = N_GROUPS * EXP_PER_GROUP
TOP_K = 2
MOE_BLOCK = 128
ROUTE_OFF = N_GROUPS
ATTN_TILE = 2048
PAGES_PER_STEP = 16
ATTN_QSUB = LANES
ATTN_KSUB = 256
LOG2E = 1.4426950408889634
DMA_UNROLL = 8
ROUTER_TILE = 256
GATHER_AHEAD = 3
GATHER_SLOTS = GATHER_AHEAD + 1


def _params(*sem):
    return pltpu.CompilerParams(dimension_semantics=sem, vmem_limit_bytes=VMEM_LIMIT)


def _dot(a, b):
    return jnp.dot(a.astype(BF16), b.astype(BF16), preferred_element_type=F32)


def _dot_nt(a, b):
    return lax.dot_general(a.astype(BF16), b.astype(BF16), (((1,), (1,)), ((), ())),
                           preferred_element_type=F32)


def _dot_tn(a, b):
    return lax.dot_general(a.astype(BF16), b.astype(BF16), (((0,), (0,)), ((), ())),
                           preferred_element_type=F32)


def _split2(x):
    hi = x.astype(BF16)
    lo = (x - hi.astype(F32)).astype(BF16)
    return hi, lo


def _dot3(a, b):
    ah, al = _split2(a)
    bh, bl = _split2(b)
    d = functools.partial(jnp.dot, preferred_element_type=F32)
    return d(ah, bh) + (d(ah, bl) + d(al, bh))


def _dot_exact_lhs(a_bf, b):
    b1 = b.astype(BF16)
    r1 = b - b1.astype(F32)
    b2 = r1.astype(BF16)
    b3 = (r1 - b2.astype(F32)).astype(BF16)
    d = functools.partial(jnp.dot, preferred_element_type=F32)
    return d(a_bf, b1) + (d(a_bf, b2) + d(a_bf, b3))


def _sigmoid(x):
    return 1.0 / (1.0 + jnp.exp(-x))


def _silu(x):
    return x * _sigmoid(x)


def _softplus(x):
    return jnp.maximum(x, 0.0) + jnp.log1p(jnp.exp(-jnp.abs(x)))


def _rms(x, g):
    return x * lax.rsqrt(jnp.mean(x * x, axis=-1, keepdims=True) + EPS) * g


def _norm_mm_kernel(x_ref, g_ref, w_ref, o_ref):
    xn = _rms(x_ref[...], g_ref[...])
    o_ref[...] = jnp.dot(xn.astype(BF16), w_ref[...], preferred_element_type=F32)


def _norm_matmul(x, g, w_bf, tm):
    T, D = x.shape
    N = w_bf.shape[1]
    return pl.pallas_call(
        _norm_mm_kernel,
        grid=(T // tm,),
        in_specs=[pl.BlockSpec((tm, D), lambda i: (i, 0)),
                  pl.BlockSpec((1, D), lambda i: (0, 0)),
                  pl.BlockSpec((D, N), lambda i: (0, 0))],
        out_specs=pl.BlockSpec((tm, N), lambda i: (i, 0)),
        out_shape=jax.ShapeDtypeStruct((T, N), F32),
        compiler_params=_params("parallel"),
        name="norm_matmul",
    )(x, g.reshape(1, D), w_bf)


def _gdn_kernel(proj_ref, conv0_ref, s0_ref, cw_ref, alog_ref, dtb_ref, onorm_ref,
                o_ref, sout_ref, cout_ref, ext_ref, qkv_ref, s_ref, *, C, nc, valid_last, bb):
    H, DK = GDN_HEADS, GDN_DK
    QK = H * DK
    CH = 3 * QK
    c = pl.program_id(1)
    masked = valid_last < C

    @pl.when(c == 0)
    def _():
        ext_ref[:, 0:SUBLANES, :] = conv0_ref[...]
        s_ref[...] = s0_ref[...]

    row1 = lax.broadcasted_iota(jnp.int32, (C, 1), 0)
    row = lax.broadcasted_iota(jnp.int32, (C, C), 0)
    col = lax.broadcasted_iota(jnp.int32, (C, C), 1)
    incl = row >= col
    strict = row > col
    eye = jnp.where(row == col, 1.0, 0.0).astype(F32)
    tril_bf = jnp.where(incl, 1.0, 0.0).astype(BF16)
    if masked:
        valid = jnp.logical_or(row1 < valid_last, c < nc - 1)

    seqs = []
    for b in range(bb):
        ext_ref[b, SUBLANES:SUBLANES + C, :] = proj_ref[b, :, 0:CH]
        base = SUBLANES - (CONV_W - 1)
        for j in range(CH // LANES):
            sl = slice(j * LANES, (j + 1) * LANES)
            acc = ext_ref[b, base:base + C, sl] * cw_ref[0:1, sl]
            for i in range(1, CONV_W):
                acc = acc + ext_ref[b, base + i:base + i + C, sl] * cw_ref[i:i + 1, sl]
            qkv_ref[b, :, sl] = _silu(acc)

        @pl.when(c == nc - 1)
        def _():
            cout_ref[b] = ext_ref[b, valid_last:valid_last + SUBLANES, :]

        ext_ref[b, 0:SUBLANES, :] = ext_ref[b, C:C + SUBLANES, :]

        ba = proj_ref[b, :, CH + QK:CH + QK + LANES]
        beta_t = _sigmoid(ba)
        g_t = -jnp.exp(alog_ref[...]) * _softplus(ba + dtb_ref[...])
        if masked:
            beta_t = jnp.where(valid, beta_t, 0.0)
            g_t = jnp.where(valid, g_t, 0.0)
        G = _dot_exact_lhs(tril_bf, g_t)
        g_last = G[C - 1:C, :]
        seqs.append(dict(beta=beta_t, G=G, GT=G.T, expG=jnp.exp(G),
                         exp_last=jnp.exp(g_last), k_scale=jnp.exp(g_last - G)))

    chains = [(b, hd) for b in range(bb) for hd in range(H)]
    st = []
    for b, hd in chains:
        sq = seqs[b]
        q = qkv_ref[b, :, hd * DK:(hd + 1) * DK]
        k = qkv_ref[b, :, QK + hd * DK:QK + (hd + 1) * DK]
        v = qkv_ref[b, :, 2 * QK + hd * DK:2 * QK + (hd + 1) * DK]
        q = q * lax.rsqrt(jnp.sum(q * q, axis=-1, keepdims=True) + EPS) * (DK ** -0.5)
        k = k * lax.rsqrt(jnp.sum(k * k, axis=-1, keepdims=True) + EPS)
        if masked:
            k = jnp.where(valid, k, 0.0)
        bcol = sq["beta"][:, hd:hd + 1]
        gcol = sq["G"][:, H + hd:H + hd + 1]
        grow = sq["GT"][H + hd:H + hd + 1, :]
        eg = sq["expG"][:, H + hd:H + hd + 1]
        decay = jnp.exp(jnp.where(incl, gcol - grow, -jnp.inf))
        st.append(dict(q=q, k=k, decay=decay, eg=eg, bcol=bcol,
                       rhs=jnp.concatenate([bcol * v, (bcol * eg) * k], axis=1)))
    for d in st:
        kk = _dot_nt(d["k"], d["k"])
        d["p"] = -jnp.where(strict, d["bcol"] * kk * d["decay"], 0.0)
        d["t"] = eye + d["p"]
    for _ in range(int(math.log2(C)) - 1):
        for d in st:
            d["p"] = _dot3(d["p"], d["p"])
        for d in st:
            d["t"] = d["t"] + _dot3(d["t"], d["p"])
    for d in st:
        d["sol"] = _dot3(d["t"], d["rhs"])
    for d, (b, hd) in zip(st, chains):
        d["S"] = s_ref[b, hd]
        d["u"] = d["sol"][:, :DK] - _dot(d["sol"][:, DK:], d["S"])
    for d in st:
        d["qk"] = _dot_nt(d["q"], d["k"]) * d["decay"]
    for d in st:
        d["o"] = _dot(d["q"] * d["eg"], d["S"]) + _dot(d["qk"], d["u"])
    for d, (b, hd) in zip(st, chains):
        sq = seqs[b]
        s_ref[b, hd] = (sq["exp_last"][:, H + hd:H + hd + 1] * d["S"]
                        + _dot_tn(d["k"] * sq["k_scale"][:, H + hd:H + hd + 1], d["u"]))
    for d, (b, hd) in zip(st, chains):
        z = proj_ref[b, :, CH + hd * DK:CH + (hd + 1) * DK]
        o_ref[b, :, hd * DK:(hd + 1) * DK] = _rms(d["o"], onorm_ref[...]) * _silu(z)

    @pl.when(c == nc - 1)
    def _():
        sout_ref[...] = s_ref[...]


def _gdn_mixer(proj, conv0p, s0, conv_w, alog_l, dtb_l, onorm, C, valid_last, bb):
    B, Lp, NP = proj.shape
    nc = Lp // C
    H, DK = GDN_HEADS, GDN_DK
    D = H * DK
    CH = 3 * H * DK
    kern = functools.partial(_gdn_kernel, C=C, nc=nc, valid_last=valid_last, bb=bb)
    return pl.pallas_call(
        kern,
        grid=(B // bb, nc),
        in_specs=[pl.BlockSpec((bb, C, NP), lambda b, c: (b, c, 0)),
                  pl.BlockSpec((bb, SUBLANES, CH), lambda b, c: (b, 0, 0)),
                  pl.BlockSpec((bb, H, DK, DK), lambda b, c: (b, 0, 0, 0)),
                  pl.BlockSpec((CONV_W, CH), lambda b, c: (0, 0)),
                  pl.BlockSpec((1, LANES), lambda b, c: (0, 0)),
                  pl.BlockSpec((1, LANES), lambda b, c: (0, 0)),
                  pl.BlockSpec((1, DK), lambda b, c: (0, 0))],
        out_specs=[pl.BlockSpec((bb, C, D), lambda b, c: (b, c, 0)),
                   pl.BlockSpec((bb, H, DK, DK), lambda b, c: (b, 0, 0, 0)),
                   pl.BlockSpec((bb, SUBLANES, CH), lambda b, c: (b, 0, 0))],
        out_shape=[jax.ShapeDtypeStruct((B, Lp, D), F32),
                   jax.ShapeDtypeStruct((B, H, DK, DK), F32),
                   jax.ShapeDtypeStruct((B, SUBLANES, CH), F32)],
        scratch_shapes=[pltpu.VMEM((bb, SUBLANES + C, CH), F32),
                        pltpu.VMEM((bb, C, CH), F32),
                        pltpu.VMEM((bb, H, DK, DK), F32)],
        compiler_params=_params("parallel", "arbitrary"),
        name="gdn_mixer",
    )(proj, conv0p, s0, conv_w, alog_l, dtb_l, onorm.reshape(1, DK))


def _router_kernel(op_ref, os_ref, hp_ref, hs_ref, wo_ref, g_ref, wr_ref, br_ref,
                   hop_ref, hos_ref, xn_ref, gate_ref, route_ref, cnt_ref, carry_ref, *, n_p):
    i = pl.program_id(0)

    @pl.when(i == 0)
    def _():
        carry_ref[...] = jnp.zeros(carry_ref.shape, F32)

    prompt = i < n_p
    x = (jnp.where(prompt, hp_ref[...], hs_ref[...])
         + jnp.dot(jnp.where(prompt, op_ref[...], os_ref[...]).astype(BF16), wo_ref[...],
                   preferred_element_type=F32))

    @pl.when(prompt)
    def _():
        hop_ref[...] = x

    @pl.when(jnp.logical_not(prompt))
    def _():
        hos_ref[...] = x

    xn = _rms(x, g_ref[...])
    xn_ref[...] = xn
    logits = _dot3(xn, wr_ref[...]) + br_ref[...]
    tm = logits.shape[0]
    lane = lax.broadcasted_iota(jnp.int32, (tm, LANES), 1).astype(F32)
    neg = -jnp.inf
    big = float(LANES)

    lg = jnp.where(lane < N_GROUPS, logits, neg)
    eg = jnp.exp(lg - jnp.max(lg, axis=-1, keepdims=True))
    pg = eg / jnp.sum(eg, axis=-1, keepdims=True)
    pmax = jnp.max(pg, axis=-1, keepdims=True)
    g_sel = jnp.min(jnp.where(pg == pmax, lane, big), axis=-1, keepdims=True)

    lo = ROUTE_OFF + EXP_PER_GROUP * g_sel
    in_grp = jnp.logical_and(lane >= lo, lane < lo + EXP_PER_GROUP)
    le = jnp.where(in_grp, logits, neg)
    ee = jnp.exp(le - jnp.max(le, axis=-1, keepdims=True))
    pe = ee / jnp.sum(ee, axis=-1, keepdims=True)
    pe = jnp.where(in_grp, pe, -1.0)
    p1 = jnp.max(pe, axis=-1, keepdims=True)
    i1 = jnp.min(jnp.where(pe == p1, lane, big), axis=-1, keepdims=True)
    pe2 = jnp.where(lane == i1, -1.0, pe)
    p2 = jnp.max(pe2, axis=-1, keepdims=True)
    i2 = jnp.min(jnp.where(pe2 == p2, lane, big), axis=-1, keepdims=True)
    den = p1 + p2
    gate_ref[...] = jnp.where(lane == 0, pmax * p1 / den, jnp.where(lane == 1, pmax * p2 / den, 0.0))

    e1 = i1 - ROUTE_OFF
    e2 = i2 - ROUTE_OFF
    oh1 = jnp.where(lane == e1, 1.0, 0.0)
    oh2 = jnp.where(lane == e2, 1.0, 0.0)
    both = oh1 + oh2
    earlier = (lax.broadcasted_iota(jnp.int32, (tm, tm), 0) > lax.broadcasted_iota(jnp.int32, (tm, tm), 1))
    before = (jnp.dot(jnp.where(earlier, 1.0, 0.0).astype(BF16), both.astype(BF16), preferred_element_type=F32)
              + carry_ref[0:1, :])
    r1 = jnp.sum(oh1 * before, axis=-1, keepdims=True)
    r2 = jnp.sum(oh2 * before, axis=-1, keepdims=True)
    carry_ref[...] = carry_ref[...] + jnp.sum(both, axis=0, keepdims=True)
    cnt_ref[...] = carry_ref[...]
    route_ref[...] = jnp.where(lane == 0, e1, jnp.where(lane == 1, e2, jnp.where(
        lane == 2, r1, jnp.where(lane == 3, r2, 0.0)))).astype(jnp.int32)


def _router(op, os, hp, hs, wo_bf, g, wr, br, tm):
    Tp, D = hp.shape
    Ts = hs.shape[0]
    n_p, n_s = Tp // tm, Ts // tm
    T = Tp + Ts
    kern = functools.partial(_router_kernel, n_p=n_p)
    p_blk = pl.BlockSpec((tm, D), lambda i: (jnp.minimum(i, n_p - 1), 0))
    s_blk = pl.BlockSpec((tm, D), lambda i: (jnp.maximum(i - n_p, 0), 0))
    row = pl.BlockSpec((tm, D), lambda i: (i, 0))
    lanes = pl.BlockSpec((tm, LANES), lambda i: (i, 0))
    return pl.pallas_call(
        kern,
        grid=(n_p + n_s,),
        in_specs=[p_blk, s_blk, p_blk, s_blk,
                  pl.BlockSpec((D, D), lambda i: (0, 0)),
                  pl.BlockSpec((1, D), lambda i: (0, 0)),
                  pl.BlockSpec((D, LANES), lambda i: (0, 0)),
                  pl.BlockSpec((1, LANES), lambda i: (0, 0))],
        out_specs=[p_blk, s_blk, row, lanes, lanes, pl.BlockSpec((SUBLANES, LANES), lambda i: (0, 0))],
        out_shape=[jax.ShapeDtypeStruct((Tp, D), F32),
                   jax.ShapeDtypeStruct((Ts, D), F32),
                   jax.ShapeDtypeStruct((T, D), F32),
                   jax.ShapeDtypeStruct((T, LANES), F32),
                   jax.ShapeDtypeStruct((T, LANES), jnp.int32),
                   jax.ShapeDtypeStruct((SUBLANES, LANES), F32)],
        scratch_shapes=[pltpu.VMEM((SUBLANES, LANES), F32)],
        compiler_params=_params("arbitrary"),
        name="moe_router",
    )(op, os, hp, hs, wo_bf, g.reshape(1, D), wr, br)


def _row_copy(src_hbm, row, dst, slot_idx, sem):
    return pltpu.make_async_copy(src_hbm.at[pl.ds(row, 1)], dst.at[slot_idx], sem)


def _expert_kernel(be_ref, rt_ref, nu_ref, xn_hbm, wg_ref, wu_ref, wd_ref, ys_ref,
                   xbuf, wg_bf, wu_bf, wd_bf, sem):
    i = pl.program_id(0)
    n_used = nu_ref[0]

    def issue(blk, slot):
        for r in range(MOE_BLOCK):
            _row_copy(xn_hbm, rt_ref[blk * MOE_BLOCK + r], xbuf, (slot, pl.ds(r, 1)), sem.at[slot]).start()

    def wait(slot):
        for r in range(MOE_BLOCK):
            _row_copy(xn_hbm, 0, xbuf, (slot, pl.ds(r, 1)), sem.at[slot]).wait()

    def issue_pos(pos):
        issue(jnp.minimum(pos, n_used - 1), pos % GATHER_SLOTS)

    @pl.when(i == 0)
    def _():
        for k in range(GATHER_AHEAD):
            issue_pos(jnp.int32(k))

    @pl.when(jnp.logical_or(i == 0, be_ref[i] != be_ref[jnp.maximum(i - 1, 0)]))
    def _():
        wg_bf[...] = wg_ref[0, 0].astype(BF16)
        wu_bf[...] = wu_ref[0, 0].astype(BF16)
        wd_bf[...] = wd_ref[0, 0].astype(BF16)

    @pl.when(i < n_used)
    def _():
        slot = i % GATHER_SLOTS
        wait(slot)
        issue_pos(i + GATHER_AHEAD)
        x = xbuf[slot].astype(BF16)
        gt = jnp.dot(x, wg_bf[...], preferred_element_type=F32)
        up = jnp.dot(x, wu_bf[...], preferred_element_type=F32)
        hid = _silu(gt) * up
        ys_ref[...] = jnp.dot(hid.astype(BF16), wd_bf[...], preferred_element_type=F32)

    @pl.when(i == n_used - 1)
    def _():
        for k in range(GATHER_AHEAD):
            wait((n_used + k) % GATHER_SLOTS)

    @pl.when(i >= n_used)
    def _():
        ys_ref[...] = jnp.zeros(ys_ref.shape, F32)


def _experts(xn, block_e, row_tok, n_used, w_gate, w_up, w_down, layer):
    T, D = xn.shape
    FF = w_gate.shape[-1]
    n_blocks = block_e.shape[0]
    P = n_blocks * MOE_BLOCK
    grid_spec = pltpu.PrefetchScalarGridSpec(
        num_scalar_prefetch=3,
        grid=(n_blocks,),
        in_specs=[pl.BlockSpec(memory_space=pl.ANY),
                  pl.BlockSpec((1, 1, D, FF), lambda i, be, rt, nu: (layer, be[i], 0, 0)),
                  pl.BlockSpec((1, 1, D, FF), lambda i, be, rt, nu: (layer, be[i], 0, 0)),
                  pl.BlockSpec((1, 1, FF, D), lambda i, be, rt, nu: (layer, be[i], 0, 0))],
        out_specs=pl.BlockSpec((MOE_BLOCK, D), lambda i, be, rt, nu: (i, 0)),
        scratch_shapes=[pltpu.VMEM((GATHER_SLOTS, MOE_BLOCK, D), F32),
                        pltpu.VMEM((D, FF), BF16),
                        pltpu.VMEM((D, FF), BF16),
                        pltpu.VMEM((FF, D), BF16),
                        pltpu.SemaphoreType.DMA((GATHER_SLOTS,))],
    )
    return pl.pallas_call(
        _expert_kernel,
        grid_spec=grid_spec,
        out_shape=jax.ShapeDtypeStruct((P, D), F32),
        compiler_params=_params("arbitrary"),
        name="moe_experts",
    )(block_e, row_tok, n_used, xn, w_gate, w_up, w_down)


def _combine_kernel(dest_ref, ys_hbm, hp_ref, hs_ref, gate_ref, fn_ref, op_ref, os_ref, ybuf, sem,
                    *, tm, n_p, n_steps, final_norm):
    i = pl.program_id(0)
    n = pl.num_programs(0)

    def issue(step, slot):
        def body(r, carry):
            for kk in range(TOP_K):
                d = dest_ref[(step * tm + r) * TOP_K + kk]
                _row_copy(ys_hbm, d, ybuf, (slot, kk, pl.ds(r, 1)), sem.at[slot]).start()
            return carry
        lax.fori_loop(0, tm, body, 0, unroll=DMA_UNROLL // TOP_K)

    def wait(slot):
        for r in range(tm):
            for kk in range(TOP_K):
                _row_copy(ys_hbm, 0, ybuf, (slot, kk, pl.ds(r, 1)), sem.at[slot]).wait()

    @pl.when(i == 0)
    def _():
        for k in range(min(GATHER_AHEAD, n_steps)):
            issue(k, k)

    @pl.when(i + GATHER_AHEAD < n)
    def _():
        issue(i + GATHER_AHEAD, (i + GATHER_AHEAD) % GATHER_SLOTS)

    slot = i % GATHER_SLOTS
    wait(slot)
    gate = gate_ref[...]
    moe = ybuf[slot, 0] * gate[:, 0:1] + ybuf[slot, 1] * gate[:, 1:2]

    def finish(h_ref, o_ref):
        y = h_ref[...] + moe
        if final_norm:
            y = _rms(y, fn_ref[...])
        o_ref[...] = y

    @pl.when(i < n_p)
    def _():
        finish(hp_ref, op_ref)

    @pl.when(i >= n_p)
    def _():
        finish(hs_ref, os_ref)


def _combine(ys, dest, hp, hs, gates, fnorm, tm, final_norm):
    Tp, D = hp.shape
    Ts = hs.shape[0]
    n_p, n_s = Tp // tm, Ts // tm
    kern = functools.partial(_combine_kernel, tm=tm, n_p=n_p, n_steps=n_p + n_s, final_norm=final_norm)
    p_blk = pl.BlockSpec((tm, D), lambda i, d: (jnp.minimum(i, n_p - 1), 0))
    s_blk = pl.BlockSpec((tm, D), lambda i, d: (jnp.maximum(i - n_p, 0), 0))
    grid_spec = pltpu.PrefetchScalarGridSpec(
        num_scalar_prefetch=1,
        grid=(n_p + n_s,),
        in_specs=[pl.BlockSpec(memory_space=pl.ANY), p_blk, s_blk,
                  pl.BlockSpec((tm, LANES), lambda i, d: (i, 0)),
                  pl.BlockSpec((1, D), lambda i, d: (0, 0))],
        out_specs=[p_blk, s_blk],
        scratch_shapes=[pltpu.VMEM((GATHER_SLOTS, TOP_K, tm, D), F32),
                        pltpu.SemaphoreType.DMA((GATHER_SLOTS,))],
    )
    return pl.pallas_call(
        kern,
        grid_spec=grid_spec,
        out_shape=[jax.ShapeDtypeStruct((Tp, D), F32), jax.ShapeDtypeStruct((Ts, D), F32)],
        compiler_params=_params("arbitrary"),
        name="moe_combine",
    )(dest, ys, hp, hs, gates, fnorm.reshape(1, D))


def _route_tables(route, counts):
    T = route.shape[0]
    A = T * TOP_K
    flat_e = route[:, 0:TOP_K].reshape(A)
    rank = route[:, TOP_K:2 * TOP_K].reshape(A)
    padded = (counts + MOE_BLOCK - 1) // MOE_BLOCK * MOE_BLOCK
    pad_end = jnp.cumsum(padded)
    pad_start = pad_end - padded
    dest = (pad_start[flat_e] + rank).astype(jnp.int32)
    n_blocks = -(-A // MOE_BLOCK) + N_EXPERTS
    P = n_blocks * MOE_BLOCK
    row_tok = jnp.zeros((P,), jnp.int32).at[dest].set(jnp.arange(A, dtype=jnp.int32) // TOP_K,
                                                      unique_indices=True, mode='promise_in_bounds')
    starts = jnp.arange(n_blocks, dtype=jnp.int32) * MOE_BLOCK
    block_e = jnp.minimum(jnp.sum((pad_end[None, :] <= starts[:, None]).astype(jnp.int32), axis=1),
                          N_EXPERTS - 1)
    n_used = (pad_end[-1] // MOE_BLOCK).astype(jnp.int32).reshape(1)
    return dest, row_tok, block_e, n_used


def _hier_moe(op, os, hp, hs, wo_bf, norm_g, wr, br, w_gate, w_up, w_down, layer, fnorm, final_norm):
    t_all = math.gcd(hp.shape[0], hs.shape[0])
    hp, hs, xn, gates, route, cnt = _router(op, os, hp, hs, wo_bf, norm_g, wr, br, math.gcd(ROUTER_TILE, t_all))
    counts = cnt[0, :N_EXPERTS].astype(jnp.int32)
    dest, row_tok, block_e, n_used = _route_tables(route, counts)
    ys = _experts(xn, block_e, row_tok, n_used, w_gate, w_up, w_down, layer)
    return _combine(ys, dest, hp, hs, gates, fnorm, math.gcd(MOE_BLOCK, t_all), final_norm)


def _rope_tile(x, cos, sin_signed, lane):
    half = DIFF_DH // 2
    rot = jnp.where((lane & (DIFF_DH - 1)) < half,
                    pltpu.roll(x, LANES - half, axis=1),
                    pltpu.roll(x, half, axis=1))
    return x * cos + rot * sin_signed


def _qkv_kernel(h_ref, gq_ref, gkv_ref, wq_ref, wkv_ref, cos_ref, sin_ref, q_ref, k_ref, v_ref):
    x = h_ref[...]
    D = x.shape[1]
    xs = x * lax.rsqrt(jnp.mean(x * x, axis=-1, keepdims=True) + EPS)
    q = jnp.dot((xs * gq_ref[...]).astype(BF16), wq_ref[...], preferred_element_type=F32)
    kv = jnp.dot((xs * gkv_ref[...]).astype(BF16), wkv_ref[...], preferred_element_type=F32)
    cos = cos_ref[...]
    sin = sin_ref[...]
    lane = lax.broadcasted_iota(jnp.int32, cos.shape, 1)
    for hd in range(D // LANES):
        sl = slice(hd * LANES, (hd + 1) * LANES)
        q_ref[:, sl] = _rope_tile(q[:, sl], cos, sin, lane) * (DIFF_DH ** -0.5)
        k_ref[:, sl] = _rope_tile(kv[:, sl], cos, sin, lane)
    v_ref[...] = kv[:, D:]


def _qkv_proj(h, gq, gkv, wq_bf, wkv_bf, cos_t, sin_t, tm):
    T, D = h.shape
    nt = cos_t.shape[0] // tm
    row = pl.BlockSpec((tm, D), lambda i: (i, 0))
    vec = pl.BlockSpec((1, D), lambda i: (0, 0))
    tab = pl.BlockSpec((tm, LANES), lambda i: (i % nt, 0))
    return pl.pallas_call(
        _qkv_kernel,
        grid=(T // tm,),
        in_specs=[row, vec, vec,
                  pl.BlockSpec((D, D), lambda i: (0, 0)),
                  pl.BlockSpec((D, 2 * D), lambda i: (0, 0)),
                  tab, tab],
        out_specs=[row, row, row],
        out_shape=[jax.ShapeDtypeStruct((T, D), F32)] * 3,
        compiler_params=_params("parallel"),
        name="qkv_proj",
    )(h, gq.reshape(1, D), gkv.reshape(1, D), wq_bf, wkv_bf, cos_t, sin_t)


def _rope_tables(pos):
    half = DIFF_DH // 2
    inv = ROPE_THETA ** (-jnp.arange(half, dtype=F32) / half)
    ang = pos.astype(F32)[:, None] * inv[None, :]
    cos = jnp.cos(ang)
    sin = jnp.sin(ang)
    reps = LANES // DIFF_DH
    return (jnp.tile(jnp.concatenate([cos, cos], axis=1), (1, reps)),
            jnp.tile(jnp.concatenate([-sin, sin], axis=1), (1, reps)))


def _lambda(lam_ref, lam_init):
    lp = lam_ref[...]
    a = jnp.sum(lp[0:1, :] * lp[1:2, :], axis=-1, keepdims=True)
    b = jnp.sum(lp[2:3, :] * lp[3:4, :], axis=-1, keepdims=True)
    return jnp.exp(a) - jnp.exp(b) + lam_init


def _attn_prompt_kernel(qi_ref, ki_ref, lam_ref, q_ref, k_ref, v_ref, subln_ref, o_ref,
                        m_ref, l_ref, acc_ref, *, t, lam_init):
    j = pl.program_id(2)
    qi = qi_ref[j]
    ki = ki_ref[j]

    @pl.when(ki == 0)
    def _():
        m_ref[...] = jnp.full(m_ref.shape, -jnp.inf, F32)
        l_ref[...] = jnp.zeros(l_ref.shape, F32)
        acc_ref[...] = jnp.zeros(acc_ref.shape, F32)

    QS, KS = min(ATTN_QSUB, t), min(ATTN_KSUB, t)

    def update(diagonal):
        q = q_ref[0] * LOG2E
        kb = k_ref[0].astype(BF16)
        vT = v_ref[0].T.astype(BF16)
        lane = lax.broadcasted_iota(jnp.int32, q.shape, 1)
        for c in range(2):
            qc = jnp.where((lane < DIFF_DH) if c == 0 else (lane >= DIFF_DH), q, 0.0).astype(BF16)
            for qs in range(t // QS):
                qsub = qc[qs * QS:(qs + 1) * QS]
                m = m_ref[c, qs]
                l = l_ref[c, qs]
                acc = acc_ref[c, qs]
                for ks in range(t // KS):
                    if diagonal and ks * KS > qs * QS + QS - 1:
                        continue
                    sT = _dot_nt(kb[ks * KS:(ks + 1) * KS], qsub)
                    if diagonal and (ks + 1) * KS - 1 > qs * QS:
                        keyi = ks * KS + lax.broadcasted_iota(jnp.int32, (KS, QS), 0)
                        qidx = qs * QS + lax.broadcasted_iota(jnp.int32, (KS, QS), 1)
                        sT = jnp.where(keyi <= qidx, sT, -jnp.inf)
                    m_new = jnp.maximum(m, jnp.max(sT, axis=0, keepdims=True))
                    alpha = jnp.exp2(m - m_new)
                    p = jnp.exp2(sT - m_new)
                    l = alpha * l + jnp.sum(p, axis=0, keepdims=True)
                    acc = alpha * acc + jnp.dot(vT[:, ks * KS:(ks + 1) * KS], p.astype(BF16),
                                                preferred_element_type=F32)
                    m = m_new
                m_ref[c, qs] = m
                l_ref[c, qs] = l
                acc_ref[c, qs] = acc

    @pl.when(ki < qi)
    def _():
        update(False)

    @pl.when(ki == qi)
    def _():
        update(True)
        lam = _lambda(lam_ref, lam_init)
        for qs in range(t // QS):
            oT = acc_ref[0, qs] / l_ref[0, qs] - lam * (acc_ref[1, qs] / l_ref[1, qs])
            o_ref[0, qs * QS:(qs + 1) * QS, :] = _rms(oT.T, subln_ref[...]) * (1.0 - lam_init)


def _attn_prompt(q, k, v, lam_p, subln, lam_init, t):
    B, L, D = q.shape
    H = DIFF_HEADS
    W = D // H
    n = L // t
    pairs = [(qi, ki) for qi in range(n) for ki in range(qi + 1)]
    qi_tab = jnp.asarray(np.array([p[0] for p in pairs], np.int32))
    ki_tab = jnp.asarray(np.array([p[1] for p in pairs], np.int32))
    qs = min(ATTN_QSUB, t)
    assert t % qs == 0 and t % min(ATTN_KSUB, t) == 0
    kern = functools.partial(_attn_prompt_kernel, t=t, lam_init=lam_init)
    grid_spec = pltpu.PrefetchScalarGridSpec(
        num_scalar_prefetch=2,
        grid=(B, H, len(pairs)),
        in_specs=[pl.BlockSpec(lam_p.shape, lambda b, h, j, qt, kt: (0, 0)),
                  pl.BlockSpec((1, t, W), lambda b, h, j, qt, kt: (b, qt[j], h)),
                  pl.BlockSpec((1, t, W), lambda b, h, j, qt, kt: (b, kt[j], h)),
                  pl.BlockSpec((1, t, W), lambda b, h, j, qt, kt: (b, kt[j], h)),
                  pl.BlockSpec((1, W), lambda b, h, j, qt, kt: (0, 0))],
        out_specs=pl.BlockSpec((1, t, W), lambda b, h, j, qt, kt: (b, qt[j], h)),
        scratch_shapes=[pltpu.VMEM((2, t // qs, 1, qs), F32),
                        pltpu.VMEM((2, t // qs, 1, qs), F32),
                        pltpu.VMEM((2, t // qs, W, qs), F32)],
    )
    return pl.pallas_call(
        kern,
        grid_spec=grid_spec,
        out_shape=jax.ShapeDtypeStruct((B, L, D), F32),
        compiler_params=_params("parallel", "parallel", "arbitrary"),
        name="diff_attn_prompt",
    )(qi_tab, ki_tab, lam_p, q, k, v, subln.reshape(1, W))


def _attn_sample_kernel(pt_ref, lam_ref, qx_ref, *rest, G, n_groups, n_new, lam_init):
    del pt_ref
    ck_refs, cv_refs = rest[:G], rest[G:2 * G]
    kn_ref, vn_ref, subln_ref, o_ref, m_ref, l_ref, acc_ref = rest[2 * G:]
    H = DIFF_HEADS
    p = pl.program_id(1)
    R = qx_ref.shape[1]
    qb = qx_ref[0].astype(BF16)

    @pl.when(p == 0)
    def _():
        m_ref[...] = jnp.full(m_ref.shape, -jnp.inf, F32)
        l_ref[...] = jnp.zeros(l_ref.shape, F32)
        acc_ref[...] = jnp.zeros(acc_ref.shape, F32)

    def head_mask(width):
        rowi = lax.broadcasted_iota(jnp.int32, (R, width), 0)
        coli = lax.broadcasted_iota(jnp.int32, (R, width), 1)
        return rowi, coli, (coli & (H - 1)) == (rowi >> 3)

    def update(s_list, v_list):
        m_prev = m_ref[...]
        m_new = m_prev
        for s in s_list:
            m_new = jnp.maximum(m_new, jnp.max(s, axis=-1, keepdims=True))
        alpha = jnp.exp(m_prev - m_new)
        l_new = alpha * l_ref[...]
        acc = alpha * acc_ref[...]
        for s, v in zip(s_list, v_list):
            pr = jnp.exp(s - m_new)
            l_new = l_new + jnp.sum(pr, axis=-1, keepdims=True)
            acc = acc + _dot(pr, v)
        m_ref[...] = m_new
        l_ref[...] = l_new
        acc_ref[...] = acc

    def page_scores():
        _, _, hm = head_mask(ck_refs[0].shape[1])
        return [jnp.where(hm, _dot_nt(qb, r[0]), -jnp.inf) for r in ck_refs], [r[0] for r in cv_refs]

    if n_groups > 1:
        @pl.when(p < n_groups - 1)
        def _():
            update(*page_scores())

    @pl.when(p == n_groups - 1)
    def _():
        s_list, v_list = page_scores()
        rowi, coli, hm = head_mask(kn_ref.shape[1])
        causal = (coli >> 3) <= (rowi & (n_new - 1))
        s_new = jnp.where(jnp.logical_and(hm, causal), _dot_nt(qb, kn_ref[0]), -jnp.inf)
        update(s_list + [s_new], v_list + [vn_ref[0]])
        lam = _lambda(lam_ref, lam_init)
        o8 = acc_ref[...] / l_ref[...]
        for hd in range(H):
            blk = o8[hd * SUBLANES:(hd + 1) * SUBLANES]
            diff = blk - lam * pltpu.roll(blk, SUBLANES // 2, axis=0)
            o_ref[0, hd * SUBLANES:(hd + 1) * SUBLANES, :] = _rms(diff, subln_ref[...]) * (1.0 - lam_init)


def _attn_sample(qx, cache_k, cache_v, page_table, k_new, v_new, lam_p, subln, lam_init, n_new):
    B, R, W = qx.shape
    rows = cache_k.shape[1]
    n_pages = page_table.shape[1]
    G = math.gcd(n_pages, PAGES_PER_STEP)
    n_groups = n_pages // G
    kern = functools.partial(_attn_sample_kernel, G=G, n_groups=n_groups, n_new=n_new, lam_init=lam_init)

    def page_spec(j):
        return pl.BlockSpec((1, rows, W), lambda b, p, pt: (pt[b, p * G + j], 0, 0))

    def seq(r):
        return pl.BlockSpec((1, r, W), lambda b, p, pt: (b, 0, 0))

    grid_spec = pltpu.PrefetchScalarGridSpec(
        num_scalar_prefetch=1,
        grid=(B, n_groups),
        in_specs=([pl.BlockSpec(lam_p.shape, lambda b, p, pt: (0, 0)), seq(R)]
                  + [page_spec(j) for j in range(G)] * 2
                  + [seq(k_new.shape[1]), seq(v_new.shape[1]),
                     pl.BlockSpec((1, W), lambda b, p, pt: (0, 0))]),
        out_specs=seq(R),
        scratch_shapes=[pltpu.VMEM((R, 1), F32),
                        pltpu.VMEM((R, 1), F32),
                        pltpu.VMEM((R, W), F32)],
    )
    return pl.pallas_call(
        kern,
        grid_spec=grid_spec,
        out_shape=jax.ShapeDtypeStruct((B, R, W), F32),
        compiler_params=_params("parallel", "arbitrary"),
        name="diff_attn_sample",
    )(page_table, lam_p, qx, *([cache_k] * G), *([cache_v] * G), k_new, v_new, subln.reshape(1, W))


def _lambda_init(layer):
    return 0.8 - 0.6 * math.exp(-0.3 * layer)


def _pad_rows(x, rows):
    return jnp.pad(x, [(0, 0), (0, rows - x.shape[1])] + [(0, 0)] * (x.ndim - 2))


def kernel(x_prompt, x_sample, state_delta, state_conv, cache_k, cache_v, page_table, norm_a, w_in_a, conv_w_a, a_log, dt_bias, onorm_a, w_out_a, kv_norm, w_kv, norm_b, w_q_b, lam_b, subln_b, w_out_b, norm_m, w_rg, b_rg, w_re, b_re, w_gate_e, w_up_e, w_down_e, final_norm):
    D = x_prompt.shape[-1]
    H = GDN_HEADS
    CH = 3 * H * GDN_DK
    NH, W = DIFF_HEADS, 2 * DIFF_DH
    n_in = w_in_a.shape[-1]
    n_in_pad = -(-n_in // LANES) * LANES
    depth = norm_m.shape[0]
    n_a = norm_a.shape[0]
    assert depth == 2 and n_a == 1 and norm_b.shape[0] == 1

    w_in_bf = jnp.pad(w_in_a[0], ((0, 0), (0, n_in_pad - n_in))).astype(BF16)
    w_out_a_bf = w_out_a[0].astype(BF16)
    w_q_bf = w_q_b[0].astype(BF16)
    w_kv_bf = w_kv.astype(BF16)
    w_out_b_bf = w_out_b[0].astype(BF16)
    n_route = N_GROUPS + N_EXPERTS
    wr = jnp.pad(jnp.concatenate([w_rg, w_re], axis=-1), ((0, 0), (0, 0), (0, LANES - n_route)))
    br = jnp.pad(jnp.concatenate([b_rg, b_re], axis=-1), ((0, 0), (0, LANES - n_route))).reshape(depth, 1, LANES)
    alog_l = jnp.zeros((1, LANES), F32).at[0, H:2 * H].set(a_log[0])
    dtb_l = jnp.zeros((1, LANES), F32).at[0, H:2 * H].set(dt_bias[0])

    def gdn_layer(h, B, L, conv0, s0):
        T = B * L
        tm = min(256, T)
        proj = _norm_matmul(h, norm_a[0], w_in_bf, tm)
        C = min(GDN_CHUNK, -(-L // SUBLANES) * SUBLANES)
        Lp = -(-L // C) * C
        valid_last = L - (Lp - C)
        proj3 = _pad_rows(proj.reshape(B, L, n_in_pad), Lp)
        conv0p = jnp.pad(conv0, ((0, 0), (SUBLANES - (CONV_W - 1), 0), (0, 0)))
        bb = next(n for n in ((4, 2, 1) if C < GDN_CHUNK else (2, 1)) if B % n == 0)
        o3, s_new, cbuf = _gdn_mixer(proj3, conv0p, s0, conv_w_a[0], alog_l, dtb_l, onorm_a[0], C, valid_last, bb)
        return o3[:, :L].reshape(T, D), s_new[None], cbuf[:, SUBLANES - (CONV_W - 1):][None]

    def attn_layer(h, B, L, past_len, paged):
        T = B * L
        tm = min(256, T)
        pos = past_len + jnp.arange(L, dtype=jnp.int32)
        cos_t, sin_t = _rope_tables(pos)
        if L % tm != 0:
            cos_t = jnp.tile(cos_t, (T // L, 1))
            sin_t = jnp.tile(sin_t, (T // L, 1))
        q, k, v = _qkv_proj(h, norm_b[0], kv_norm, w_q_bf, w_kv_bf, cos_t, sin_t, tm)
        lam_init = _lambda_init(n_a)
        if paged is None:
            o = _attn_prompt(q.reshape(B, L, D), k.reshape(B, L, D), v.reshape(B, L, D),
                             lam_b[0], subln_b[0], lam_init, min(ATTN_TILE, L))
            o = o.reshape(T, D)
        else:
            ck, cv, pt = paged
            assert 2 * L == SUBLANES and NH == SUBLANES
            q5 = q.reshape(B, L, NH, 2, DIFF_DH).transpose(0, 2, 1, 3, 4)
            sel = jnp.eye(2, dtype=F32)[None, None, :, None, :, None]
            qx = (q5[:, :, None] * sel).reshape(B, NH * 2 * L, W)
            o = _attn_sample(qx, ck, cv, pt, k.reshape(B, L * NH, W), v.reshape(B, L * NH, W),
                             lam_b[0], subln_b[0], lam_init, L)
            o = o.reshape(B, NH, 2 * L, W)[:, :, :L].transpose(0, 2, 1, 3).reshape(T, D)
        kshape = (B, L, NH, W)
        return o, k.reshape(kshape), v.reshape(kshape)

    def moe(op, os, hp, hs, wo_bf, layer, last):
        return _hier_moe(op, os, hp, hs, wo_bf, norm_m[layer], wr[layer], br[layer], w_gate_e, w_up_e, w_down_e,
                         layer, final_norm, last)

    Bp, Lq, _ = x_prompt.shape
    Bs, Ls, _ = x_sample.shape
    hp = x_prompt.reshape(Bp * Lq, D)
    hs = x_sample.reshape(Bs * Ls, D)

    op, sd_p, sc_p = gdn_layer(hp, Bp, Lq, jnp.zeros((Bp, CONV_W - 1, CH), F32),
                               jnp.zeros((Bp, H, GDN_DK, GDN_DK), F32))
    os, sd_s, sc_s = gdn_layer(hs, Bs, Ls, state_conv[0], state_delta[0])
    hp, hs = moe(op, os, hp, hs, w_out_a_bf, 0, False)

    n_phys, page, nh, w = cache_k.shape
    past_len = page_table.shape[1] * page
    ck = cache_k.reshape(n_phys, page * nh, w)
    cv = cache_v.reshape(n_phys, page * nh, w)
    op, k_p, v_p = attn_layer(hp, Bp, Lq, 0, None)
    os, k_s, v_s = attn_layer(hs, Bs, Ls, past_len, (ck, cv, page_table))
    y_p, y_s = moe(op, os, hp, hs, w_out_b_bf, 1, True)
    return (y_p.reshape(Bp, Lq, D), y_s.reshape(Bs, Ls, D), sd_p, sc_p, k_p, v_p, sd_s, sc_s, k_s, v_s)
```

```python
import functools
import math

import numpy as np
import jax
import jax.numpy as jnp
from jax import lax
from jax.experimental import pallas as pl
from jax.experimental.pallas import tpu as pltpu

F32 = jnp.float32
BF16 = jnp.bfloat16
EPS = 1e-6
LANES = 128
SUBLANES = 8
VMEM_LIMIT = 56 * 1024 * 1024

CONV_W = 4
GDN_HEADS = 8
GDN_DK = 128
GDN_CHUNK = 64
DIFF_HEADS = 8
DIFF_DH = 64
ROPE_THETA = 10000.0
N_GROUPS = 4
EXP_PER_GROUP = 8
N_EXPERTS = N_GROUPS * EXP_PER_GROUP
TOP_K = 2
MOE_BLOCK = 128
ROUTE_OFF = N_GROUPS
ATTN_TILE = 2048
PAGES_PER_STEP = 16
ATTN_QSUB = LANES
ATTN_KSUB = 256
LOG2E = 1.4426950408889634
DMA_UNROLL = 8
ROUTER_TILE = 256
GATHER_AHEAD = 3
GATHER_SLOTS = GATHER_AHEAD + 1
COMBINE_TILE = 256
TOKEN_TILE = 512


def _params(*sem):
    return pltpu.CompilerParams(dimension_semantics=sem, vmem_limit_bytes=VMEM_LIMIT)


def _dot(a, b):
    return jnp.dot(a.astype(BF16), b.astype(BF16), preferred_element_type=F32)


def _dot_nt(a, b):
    return lax.dot_general(a.astype(BF16), b.astype(BF16), (((1,), (1,)), ((), ())),
                           preferred_element_type=F32)


def _dot_tn(a, b):
    return lax.dot_general(a.astype(BF16), b.astype(BF16), (((0,), (0,)), ((), ())),
                           preferred_element_type=F32)


def _split2(x):
    hi = x.astype(BF16)
    lo = (x - hi.astype(F32)).astype(BF16)
    return hi, lo


def _dot3(a, b):
    ah, al = _split2(a)
    bh, bl = _split2(b)
    d = functools.partial(jnp.dot, preferred_element_type=F32)
    return d(ah, bh) + (d(ah, bl) + d(al, bh))


def _dot_exact_lhs(a_bf, b):
    b1 = b.astype(BF16)
    r1 = b - b1.astype(F32)
    b2 = r1.astype(BF16)
    b3 = (r1 - b2.astype(F32)).astype(BF16)
    d = functools.partial(jnp.dot, preferred_element_type=F32)
    return d(a_bf, b1) + (d(a_bf, b2) + d(a_bf, b3))


def _sigmoid(x):
    return 1.0 / (1.0 + jnp.exp(-x))


def _silu(x):
    return x * _sigmoid(x)


def _softplus(x):
    return jnp.maximum(x, 0.0) + jnp.log1p(jnp.exp(-jnp.abs(x)))


def _rms(x, g):
    return x * lax.rsqrt(jnp.mean(x * x, axis=-1, keepdims=True) + EPS) * g


def _norm_mm_kernel(x_ref, g_ref, w_ref, o_ref):
    xn = _rms(x_ref[...], g_ref[...])
    o_ref[...] = jnp.dot(xn.astype(BF16), w_ref[...], preferred_element_type=F32)


def _norm_matmul(x, g, w_bf, tm):
    T, D = x.shape
    N = w_bf.shape[1]
    return pl.pallas_call(
        _norm_mm_kernel,
        grid=(T // tm,),
        in_specs=[pl.BlockSpec((tm, D), lambda i: (i, 0)),
                  pl.BlockSpec((1, D), lambda i: (0, 0)),
                  pl.BlockSpec((D, N), lambda i: (0, 0))],
        out_specs=pl.BlockSpec((tm, N), lambda i: (i, 0)),
        out_shape=jax.ShapeDtypeStruct((T, N), F32),
        compiler_params=_params("parallel"),
        name="norm_matmul",
    )(x, g.reshape(1, D), w_bf)


def _gdn_kernel(proj_ref, conv0_ref, s0_ref, cw_ref, alog_ref, dtb_ref, onorm_ref,
                o_ref, sout_ref, cout_ref, ext_ref, qkv_ref, s_ref, *, C, nc, valid_last, bb):
    H, DK = GDN_HEADS, GDN_DK
    QK = H * DK
    CH = 3 * QK
    c = pl.program_id(1)
    masked = valid_last < C

    @pl.when(c == 0)
    def _():
        ext_ref[:, 0:SUBLANES, :] = conv0_ref[...]
        s_ref[...] = s0_ref[...]

    row1 = lax.broadcasted_iota(jnp.int32, (C, 1), 0)
    row = lax.broadcasted_iota(jnp.int32, (C, C), 0)
    col = lax.broadcasted_iota(jnp.int32, (C, C), 1)
    incl = row >= col
    strict = row > col
    eye = jnp.where(row == col, 1.0, 0.0).astype(F32)
    tril_bf = jnp.where(incl, 1.0, 0.0).astype(BF16)
    if masked:
        valid = jnp.logical_or(row1 < valid_last, c < nc - 1)

    seqs = []
    for b in range(bb):
        ext_ref[b, SUBLANES:SUBLANES + C, :] = proj_ref[b, :, 0:CH]
        base = SUBLANES - (CONV_W - 1)
        for j in range(CH // LANES):
            sl = slice(j * LANES, (j + 1) * LANES)
            acc = ext_ref[b, base:base + C, sl] * cw_ref[0:1, sl]
            for i in range(1, CONV_W):
                acc = acc + ext_ref[b, base + i:base + i + C, sl] * cw_ref[i:i + 1, sl]
            qkv_ref[b, :, sl] = _silu(acc)

        @pl.when(c == nc - 1)
        def _():
            cout_ref[b] = ext_ref[b, valid_last:valid_last + SUBLANES, :]

        ext_ref[b, 0:SUBLANES, :] = ext_ref[b, C:C + SUBLANES, :]

        ba = proj_ref[b, :, CH + QK:CH + QK + LANES]
        beta_t = _sigmoid(ba)
        g_t = -jnp.exp(alog_ref[...]) * _softplus(ba + dtb_ref[...])
        if masked:
            beta_t = jnp.where(valid, beta_t, 0.0)
            g_t = jnp.where(valid, g_t, 0.0)
        G = _dot_exact_lhs(tril_bf, g_t)
        g_last = G[C - 1:C, :]
        seqs.append(dict(beta=beta_t, G=G, GT=G.T, expG=jnp.exp(G),
                         exp_last=jnp.exp(g_last), k_scale=jnp.exp(g_last - G)))

    chains = [(b, hd) for b in range(bb) for hd in range(H)]
    st = []
    for b, hd in chains:
        sq = seqs[b]
        q = qkv_ref[b, :, hd * DK:(hd + 1) * DK]
        k = qkv_ref[b, :, QK + hd * DK:QK + (hd + 1) * DK]
        v = qkv_ref[b, :, 2 * QK + hd * DK:2 * QK + (hd + 1) * DK]
        q = q * lax.rsqrt(jnp.sum(q * q, axis=-1, keepdims=True) + EPS) * (DK ** -0.5)
        k = k * lax.rsqrt(jnp.sum(k * k, axis=-1, keepdims=True) + EPS)
        if masked:
            k = jnp.where(valid, k, 0.0)
        bcol = sq["beta"][:, hd:hd + 1]
        gcol = sq["G"][:, H + hd:H + hd + 1]
        grow = sq["GT"][H + hd:H + hd + 1, :]
        eg = sq["expG"][:, H + hd:H + hd + 1]
        decay = jnp.exp(jnp.where(incl, gcol - grow, -jnp.inf))
        st.append(dict(q=q, k=k, decay=decay, eg=eg, bcol=bcol,
                       rhs=jnp.concatenate([bcol * v, (bcol * eg) * k], axis=1)))
    for d in st:
        kk = _dot_nt(d["k"], d["k"])
        d["p"] = -jnp.where(strict, d["bcol"] * kk * d["decay"], 0.0)
        d["t"] = eye + d["p"]
    for _ in range(int(math.log2(C)) - 1):
        for d in st:
            d["p"] = _dot3(d["p"], d["p"])
        for d in st:
            d["t"] = d["t"] + _dot3(d["t"], d["p"])
    for d in st:
        d["sol"] = _dot3(d["t"], d["rhs"])
    for d, (b, hd) in zip(st, chains):
        d["S"] = s_ref[b, hd]
        d["u"] = d["sol"][:, :DK] - _dot(d["sol"][:, DK:], d["S"])
    for d in st:
        d["qk"] = _dot_nt(d["q"], d["k"]) * d["decay"]
    for d in st:
        d["o"] = _dot(d["q"] * d["eg"], d["S"]) + _dot(d["qk"], d["u"])
    for d, (b, hd) in zip(st, chains):
        sq = seqs[b]
        s_ref[b, hd] = (sq["exp_last"][:, H + hd:H + hd + 1] * d["S"]
                        + _dot_tn(d["k"] * sq["k_scale"][:, H + hd:H + hd + 1], d["u"]))
    for d, (b, hd) in zip(st, chains):
        z = proj_ref[b, :, CH + hd * DK:CH + (hd + 1) * DK]
        o_ref[b, :, hd * DK:(hd + 1) * DK] = _rms(d["o"], onorm_ref[...]) * _silu(z)

    @pl.when(c == nc - 1)
    def _():
        sout_ref[...] = s_ref[...]


def _gdn_mixer(proj, conv0p, s0, conv_w, alog_l, dtb_l, onorm, C, valid_last, bb):
    B, Lp, NP = proj.shape
    nc = Lp // C
    H, DK = GDN_HEADS, GDN_DK
    D = H * DK
    CH = 3 * H * DK
    kern = functools.partial(_gdn_kernel, C=C, nc=nc, valid_last=valid_last, bb=bb)
    return pl.pallas_call(
        kern,
        grid=(B // bb, nc),
        in_specs=[pl.BlockSpec((bb, C, NP), lambda b, c: (b, c, 0)),
                  pl.BlockSpec((bb, SUBLANES, CH), lambda b, c: (b, 0, 0)),
                  pl.BlockSpec((bb, H, DK, DK), lambda b, c: (b, 0, 0, 0)),
                  pl.BlockSpec((CONV_W, CH), lambda b, c: (0, 0)),
                  pl.BlockSpec((1, LANES), lambda b, c: (0, 0)),
                  pl.BlockSpec((1, LANES), lambda b, c: (0, 0)),
                  pl.BlockSpec((1, DK), lambda b, c: (0, 0))],
        out_specs=[pl.BlockSpec((bb, C, D), lambda b, c: (b, c, 0)),
                   pl.BlockSpec((bb, H, DK, DK), lambda b, c: (b, 0, 0, 0)),
                   pl.BlockSpec((bb, SUBLANES, CH), lambda b, c: (b, 0, 0))],
        out_shape=[jax.ShapeDtypeStruct((B, Lp, D), F32),
                   jax.ShapeDtypeStruct((B, H, DK, DK), F32),
                   jax.ShapeDtypeStruct((B, SUBLANES, CH), F32)],
        scratch_shapes=[pltpu.VMEM((bb, SUBLANES + C, CH), F32),
                        pltpu.VMEM((bb, C, CH), F32),
                        pltpu.VMEM((bb, H, DK, DK), F32)],
        compiler_params=_params("parallel", "arbitrary"),
        name="gdn_mixer",
    )(proj, conv0p, s0, conv_w, alog_l, dtb_l, onorm.reshape(1, DK))


def _router_kernel(op_ref, os_ref, hp_ref, hs_ref, wo_ref, g_ref, wr_ref, br_ref,
                   hop_ref, hos_ref, xn_ref, gate_ref, route_ref, cnt_ref, carry_ref, *, n_p):
    i = pl.program_id(0)

    @pl.when(i == 0)
    def _():
        carry_ref[...] = jnp.zeros(carry_ref.shape, F32)

    prompt = i < n_p
    x = (jnp.where(prompt, hp_ref[...], hs_ref[...])
         + jnp.dot(jnp.where(prompt, op_ref[...], os_ref[...]).astype(BF16), wo_ref[...],
                   preferred_element_type=F32))

    @pl.when(prompt)
    def _():
        hop_ref[...] = x

    @pl.when(jnp.logical_not(prompt))
    def _():
        hos_ref[...] = x

    xn = _rms(x, g_ref[...])
    xn_ref[...] = xn
    logits = _dot3(xn, wr_ref[...]) + br_ref[...]
    tm = logits.shape[0]
    lane = lax.broadcasted_iota(jnp.int32, (tm, LANES), 1).astype(F32)
    neg = -jnp.inf
    big = float(LANES)

    lg = jnp.where(lane < N_GROUPS, logits, neg)
    eg = jnp.exp(lg - jnp.max(lg, axis=-1, keepdims=True))
    pg = eg / jnp.sum(eg, axis=-1, keepdims=True)
    pmax = jnp.max(pg, axis=-1, keepdims=True)
    g_sel = jnp.min(jnp.where(pg == pmax, lane, big), axis=-1, keepdims=True)

    lo = ROUTE_OFF + EXP_PER_GROUP * g_sel
    in_grp = jnp.logical_and(lane >= lo, lane < lo + EXP_PER_GROUP)
    le = jnp.where(in_grp, logits, neg)
    ee = jnp.exp(le - jnp.max(le, axis=-1, keepdims=True))
    pe = ee / jnp.sum(ee, axis=-1, keepdims=True)
    pe = jnp.where(in_grp, pe, -1.0)
    p1 = jnp.max(pe, axis=-1, keepdims=True)
    i1 = jnp.min(jnp.where(pe == p1, lane, big), axis=-1, keepdims=True)
    pe2 = jnp.where(lane == i1, -1.0, pe)
    p2 = jnp.max(pe2, axis=-1, keepdims=True)
    i2 = jnp.min(jnp.where(pe2 == p2, lane, big), axis=-1, keepdims=True)
    den = p1 + p2
    gate_ref[...] = jnp.where(lane == 0, pmax * p1 / den, jnp.where(lane == 1, pmax * p2 / den, 0.0))

    e1 = i1 - ROUTE_OFF
    e2 = i2 - ROUTE_OFF
    oh1 = jnp.where(lane == e1, 1.0, 0.0)
    oh2 = jnp.where(lane == e2, 1.0, 0.0)
    both = oh1 + oh2
    earlier = (lax.broadcasted_iota(jnp.int32, (tm, tm), 0) > lax.broadcasted_iota(jnp.int32, (tm, tm), 1))
    before = (jnp.dot(jnp.where(earlier, 1.0, 0.0).astype(BF16), both.astype(BF16), preferred_element_type=F32)
              + carry_ref[0:1, :])
    r1 = jnp.sum(oh1 * before, axis=-1, keepdims=True)
    r2 = jnp.sum(oh2 * before, axis=-1, keepdims=True)
    carry_ref[...] = carry_ref[...] + jnp.sum(both, axis=0, keepdims=True)
    cnt_ref[...] = carry_ref[...]
    route_ref[...] = jnp.where(lane == 0, e1, jnp.where(lane == 1, e2, jnp.where(
        lane == 2, r1, jnp.where(lane == 3, r2, 0.0)))).astype(jnp.int32)


def _router(op, os, hp, hs, wo_bf, g, wr, br, tm):
    Tp, D = hp.shape
    Ts = hs.shape[0]
    n_p, n_s = Tp // tm, Ts // tm
    T = Tp + Ts
    kern = functools.partial(_router_kernel, n_p=n_p)
    p_blk = pl.BlockSpec((tm, D), lambda i: (jnp.minimum(i, n_p - 1), 0))
    s_blk = pl.BlockSpec((tm, D), lambda i: (jnp.maximum(i - n_p, 0), 0))
    row = pl.BlockSpec((tm, D), lambda i: (i, 0))
    lanes = pl.BlockSpec((tm, LANES), lambda i: (i, 0))
    return pl.pallas_call(
        kern,
        grid=(n_p + n_s,),
        in_specs=[p_blk, s_blk, p_blk, s_blk,
                  pl.BlockSpec((D, D), lambda i: (0, 0)),
                  pl.BlockSpec((1, D), lambda i: (0, 0)),
                  pl.BlockSpec((D, LANES), lambda i: (0, 0)),
                  pl.BlockSpec((1, LANES), lambda i: (0, 0))],
        out_specs=[p_blk, s_blk, row, lanes, lanes, pl.BlockSpec((SUBLANES, LANES), lambda i: (0, 0))],
        out_shape=[jax.ShapeDtypeStruct((Tp, D), F32),
                   jax.ShapeDtypeStruct((Ts, D), F32),
                   jax.ShapeDtypeStruct((T, D), F32),
                   jax.ShapeDtypeStruct((T, LANES), F32),
                   jax.ShapeDtypeStruct((T, LANES), jnp.int32),
                   jax.ShapeDtypeStruct((SUBLANES, LANES), F32)],
        scratch_shapes=[pltpu.VMEM((SUBLANES, LANES), F32)],
        compiler_params=_params("arbitrary"),
        name="moe_router",
    )(op, os, hp, hs, wo_bf, g.reshape(1, D), wr, br)


def _row_copy(src_hbm, row, dst, slot_idx, sem):
    return pltpu.make_async_copy(src_hbm.at[pl.ds(row, 1)], dst.at[slot_idx], sem)


def _expert_kernel(be_ref, rt_ref, nu_ref, xn_hbm, wg_ref, wu_ref, wd_ref, ys_ref,
                   xbuf, wg_bf, wu_bf, wd_bf, sem):
    i = pl.program_id(0)
    n_used = nu_ref[0]

    def issue(blk, slot):
        for r in range(MOE_BLOCK):
            _row_copy(xn_hbm, rt_ref[blk * MOE_BLOCK + r], xbuf, (slot, pl.ds(r, 1)), sem.at[slot]).start()

    def wait(slot):
        for r in range(MOE_BLOCK):
            _row_copy(xn_hbm, 0, xbuf, (slot, pl.ds(r, 1)), sem.at[slot]).wait()

    def issue_pos(pos):
        issue(jnp.minimum(pos, n_used - 1), pos % GATHER_SLOTS)

    @pl.when(i == 0)
    def _():
        for k in range(GATHER_AHEAD):
            issue_pos(jnp.int32(k))

    @pl.when(jnp.logical_or(i == 0, be_ref[i] != be_ref[jnp.maximum(i - 1, 0)]))
    def _():
        wg_bf[...] = wg_ref[0, 0].astype(BF16)
        wu_bf[...] = wu_ref[0, 0].astype(BF16)
        wd_bf[...] = wd_ref[0, 0].astype(BF16)

    @pl.when(i < n_used)
    def _():
        slot = i % GATHER_SLOTS
        wait(slot)
        issue_pos(i + GATHER_AHEAD)
        x = xbuf[slot].astype(BF16)
        gt = jnp.dot(x, wg_bf[...], preferred_element_type=F32)
        up = jnp.dot(x, wu_bf[...], preferred_element_type=F32)
        hid = _silu(gt) * up
        ys_ref[...] = jnp.dot(hid.astype(BF16), wd_bf[...], preferred_element_type=F32)

    @pl.when(i == n_used - 1)
    def _():
        for k in range(GATHER_AHEAD):
            wait((n_used + k) % GATHER_SLOTS)

    @pl.when(i >= n_used)
    def _():
        ys_ref[...] = jnp.zeros(ys_ref.shape, F32)


def _experts(xn, block_e, row_tok, n_used, w_gate, w_up, w_down, layer):
    T, D = xn.shape
    FF = w_gate.shape[-1]
    n_blocks = block_e.shape[0]
    P = n_blocks * MOE_BLOCK
    grid_spec = pltpu.PrefetchScalarGridSpec(
        num_scalar_prefetch=3,
        grid=(n_blocks,),
        in_specs=[pl.BlockSpec(memory_space=pl.ANY),
                  pl.BlockSpec((1, 1, D, FF), lambda i, be, rt, nu: (layer, be[i], 0, 0)),
                  pl.BlockSpec((1, 1, D, FF), lambda i, be, rt, nu: (layer, be[i], 0, 0)),
                  pl.BlockSpec((1, 1, FF, D), lambda i, be, rt, nu: (layer, be[i], 0, 0))],
        out_specs=pl.BlockSpec((MOE_BLOCK, D), lambda i, be, rt, nu: (i, 0)),
        scratch_shapes=[pltpu.VMEM((GATHER_SLOTS, MOE_BLOCK, D), F32),
                        pltpu.VMEM((D, FF), BF16),
                        pltpu.VMEM((D, FF), BF16),
                        pltpu.VMEM((FF, D), BF16),
                        pltpu.SemaphoreType.DMA((GATHER_SLOTS,))],
    )
    return pl.pallas_call(
        _expert_kernel,
        grid_spec=grid_spec,
        out_shape=jax.ShapeDtypeStruct((P, D), F32),
        compiler_params=_params("arbitrary"),
        name="moe_experts",
    )(block_e, row_tok, n_used, xn, w_gate, w_up, w_down)


def _combine_kernel(dest_ref, ys_hbm, hp_ref, hs_ref, gate_ref, fn_ref, op_ref, os_ref, ybuf, sem,
                    *, tm, n_p, n_steps, final_norm):
    i = pl.program_id(0)
    n = pl.num_programs(0)

    def issue(step, slot):
        def body(r, carry):
            for kk in range(TOP_K):
                d = dest_ref[(step * tm + r) * TOP_K + kk]
                _row_copy(ys_hbm, d, ybuf, (slot, kk, pl.ds(r, 1)), sem.at[slot]).start()
            return carry
        lax.fori_loop(0, tm, body, 0, unroll=DMA_UNROLL // TOP_K)

    def wait(slot):
        for r in range(tm):
            for kk in range(TOP_K):
                _row_copy(ys_hbm, 0, ybuf, (slot, kk, pl.ds(r, 1)), sem.at[slot]).wait()

    @pl.when(i == 0)
    def _():
        for k in range(min(GATHER_AHEAD, n_steps)):
            issue(k, k)

    @pl.when(i + GATHER_AHEAD < n)
    def _():
        issue(i + GATHER_AHEAD, (i + GATHER_AHEAD) % GATHER_SLOTS)

    slot = i % GATHER_SLOTS
    wait(slot)
    gate = gate_ref[...]
    moe = ybuf[slot, 0] * gate[:, 0:1] + ybuf[slot, 1] * gate[:, 1:2]

    def finish(h_ref, o_ref):
        y = h_ref[...] + moe
        if final_norm:
            y = _rms(y, fn_ref[...])
        o_ref[...] = y

    @pl.when(i < n_p)
    def _():
        finish(hp_ref, op_ref)

    @pl.when(i >= n_p)
    def _():
        finish(hs_ref, os_ref)


def _combine(ys, dest, hp, hs, gates, fnorm, tm, final_norm):
    Tp, D = hp.shape
    Ts = hs.shape[0]
    n_p, n_s = Tp // tm, Ts // tm
    kern = functools.partial(_combine_kernel, tm=tm, n_p=n_p, n_steps=n_p + n_s, final_norm=final_norm)
    p_blk = pl.BlockSpec((tm, D), lambda i, d: (jnp.minimum(i, n_p - 1), 0))
    s_blk = pl.BlockSpec((tm, D), lambda i, d: (jnp.maximum(i - n_p, 0), 0))
    grid_spec = pltpu.PrefetchScalarGridSpec(
        num_scalar_prefetch=1,
        grid=(n_p + n_s,),
        in_specs=[pl.BlockSpec(memory_space=pl.ANY), p_blk, s_blk,
                  pl.BlockSpec((tm, LANES), lambda i, d: (i, 0)),
                  pl.BlockSpec((1, D), lambda i, d: (0, 0))],
        out_specs=[p_blk, s_blk],
        scratch_shapes=[pltpu.VMEM((GATHER_SLOTS, TOP_K, tm, D), F32),
                        pltpu.SemaphoreType.DMA((GATHER_SLOTS,))],
    )
    return pl.pallas_call(
        kern,
        grid_spec=grid_spec,
        out_shape=[jax.ShapeDtypeStruct((Tp, D), F32), jax.ShapeDtypeStruct((Ts, D), F32)],
        compiler_params=_params("arbitrary"),
        name="moe_combine",
    )(dest, ys, hp, hs, gates, fnorm.reshape(1, D))


def _route_tables(route, counts):
    T = route.shape[0]
    A = T * TOP_K
    flat_e = route[:, 0:TOP_K].reshape(A)
    rank = route[:, TOP_K:2 * TOP_K].reshape(A)
    padded = (counts + MOE_BLOCK - 1) // MOE_BLOCK * MOE_BLOCK
    pad_end = jnp.cumsum(padded)
    pad_start = pad_end - padded
    dest = (pad_start[flat_e] + rank).astype(jnp.int32)
    n_blocks = -(-A // MOE_BLOCK) + N_EXPERTS
    P = n_blocks * MOE_BLOCK
    row_tok = jnp.zeros((P,), jnp.int32).at[dest].set(jnp.arange(A, dtype=jnp.int32) // TOP_K,
                                                      unique_indices=True, mode='promise_in_bounds')
    starts = jnp.arange(n_blocks, dtype=jnp.int32) * MOE_BLOCK
    block_e = jnp.minimum(jnp.sum((pad_end[None, :] <= starts[:, None]).astype(jnp.int32), axis=1),
                          N_EXPERTS - 1)
    n_used = (pad_end[-1] // MOE_BLOCK).astype(jnp.int32).reshape(1)
    return dest, row_tok, block_e, n_used


def _hier_moe(op, os, hp, hs, wo_bf, norm_g, wr, br, w_gate, w_up, w_down, layer, fnorm, final_norm):
    t_all = math.gcd(hp.shape[0], hs.shape[0])
    hp, hs, xn, gates, route, cnt = _router(op, os, hp, hs, wo_bf, norm_g, wr, br, math.gcd(ROUTER_TILE, t_all))
    counts = cnt[0, :N_EXPERTS].astype(jnp.int32)
    dest, row_tok, block_e, n_used = _route_tables(route, counts)
    ys = _experts(xn, block_e, row_tok, n_used, w_gate, w_up, w_down, layer)
    return _combine(ys, dest, hp, hs, gates, fnorm, math.gcd(COMBINE_TILE, t_all), final_norm)


def _rope_tile(x, cos, sin_signed, lane):
    half = DIFF_DH // 2
    rot = jnp.where((lane & (DIFF_DH - 1)) < half,
                    pltpu.roll(x, LANES - half, axis=1),
                    pltpu.roll(x, half, axis=1))
    return x * cos + rot * sin_signed


def _qkv_kernel(h_ref, gq_ref, gkv_ref, wq_ref, wkv_ref, cos_ref, sin_ref, q_ref, k_ref, v_ref):
    x = h_ref[...]
    D = x.shape[1]
    xs = x * lax.rsqrt(jnp.mean(x * x, axis=-1, keepdims=True) + EPS)
    q = jnp.dot((xs * gq_ref[...]).astype(BF16), wq_ref[...], preferred_element_type=F32)
    kv = jnp.dot((xs * gkv_ref[...]).astype(BF16), wkv_ref[...], preferred_element_type=F32)
    cos = cos_ref[...]
    sin = sin_ref[...]
    lane = lax.broadcasted_iota(jnp.int32, cos.shape, 1)
    for hd in range(D // LANES):
        sl = slice(hd * LANES, (hd + 1) * LANES)
        q_ref[:, sl] = _rope_tile(q[:, sl], cos, sin, lane) * (DIFF_DH ** -0.5)
        k_ref[:, sl] = _rope_tile(kv[:, sl], cos, sin, lane)
    v_ref[...] = kv[:, D:]


def _qkv_proj(h, gq, gkv, wq_bf, wkv_bf, cos_t, sin_t, tm):
    T, D = h.shape
    nt = cos_t.shape[0] // tm
    row = pl.BlockSpec((tm, D), lambda i: (i, 0))
    vec = pl.BlockSpec((1, D), lambda i: (0, 0))
    tab = pl.BlockSpec((tm, LANES), lambda i: (i % nt, 0))
    return pl.pallas_call(
        _qkv_kernel,
        grid=(T // tm,),
        in_specs=[row, vec, vec,
                  pl.BlockSpec((D, D), lambda i: (0, 0)),
                  pl.BlockSpec((D, 2 * D), lambda i: (0, 0)),
                  tab, tab],
        out_specs=[row, row, row],
        out_shape=[jax.ShapeDtypeStruct((T, D), F32)] * 3,
        compiler_params=_params("parallel"),
        name="qkv_proj",
    )(h, gq.reshape(1, D), gkv.reshape(1, D), wq_bf, wkv_bf, cos_t, sin_t)


def _rope_tables(pos):
    half = DIFF_DH // 2
    inv = ROPE_THETA ** (-jnp.arange(half, dtype=F32) / half)
    ang = pos.astype(F32)[:, None] * inv[None, :]
    cos = jnp.cos(ang)
    sin = jnp.sin(ang)
    reps = LANES // DIFF_DH
    return (jnp.tile(jnp.concatenate([cos, cos], axis=1), (1, reps)),
            jnp.tile(jnp.concatenate([-sin, sin], axis=1), (1, reps)))


def _lambda(lam_ref, lam_init):
    lp = lam_ref[...]
    a = jnp.sum(lp[0:1, :] * lp[1:2, :], axis=-1, keepdims=True)
    b = jnp.sum(lp[2:3, :] * lp[3:4, :], axis=-1, keepdims=True)
    return jnp.exp(a) - jnp.exp(b) + lam_init


def _attn_prompt_kernel(qi_ref, ki_ref, lam_ref, q_ref, k_ref, v_ref, subln_ref, o_ref,
                        m_ref, l_ref, acc_ref, *, t, lam_init):
    j = pl.program_id(2)
    qi = qi_ref[j]
    ki = ki_ref[j]

    @pl.when(ki == 0)
    def _():
        m_ref[...] = jnp.full(m_ref.shape, -jnp.inf, F32)
        l_ref[...] = jnp.zeros(l_ref.shape, F32)
        acc_ref[...] = jnp.zeros(acc_ref.shape, F32)

    QS, KS = min(ATTN_QSUB, t), min(ATTN_KSUB, t)

    def update(diagonal):
        q = q_ref[0] * LOG2E
        kb = k_ref[0].astype(BF16)
        vT = v_ref[0].T.astype(BF16)
        lane = lax.broadcasted_iota(jnp.int32, q.shape, 1)
        for c in range(2):
            qc = jnp.where((lane < DIFF_DH) if c == 0 else (lane >= DIFF_DH), q, 0.0).astype(BF16)
            for qs in range(t // QS):
                qsub = qc[qs * QS:(qs + 1) * QS]
                m = m_ref[c, qs]
                l = l_ref[c, qs]
                acc = acc_ref[c, qs]
                for ks in range(t // KS):
                    if diagonal and ks * KS > qs * QS + QS - 1:
                        continue
                    sT = _dot_nt(kb[ks * KS:(ks + 1) * KS], qsub)
                    if diagonal and (ks + 1) * KS - 1 > qs * QS:
                        keyi = ks * KS + lax.broadcasted_iota(jnp.int32, (KS, QS), 0)
                        qidx = qs * QS + lax.broadcasted_iota(jnp.int32, (KS, QS), 1)
                        sT = jnp.where(keyi <= qidx, sT, -jnp.inf)
                    m_new = jnp.maximum(m, jnp.max(sT, axis=0, keepdims=True))
                    alpha = jnp.exp2(m - m_new)
                    p = jnp.exp2(sT - m_new)
                    l = alpha * l + jnp.sum(p, axis=0, keepdims=True)
                    acc = alpha * acc + jnp.dot(vT[:, ks * KS:(ks + 1) * KS], p.astype(BF16),
                                                preferred_element_type=F32)
                    m = m_new
                m_ref[c, qs] = m
                l_ref[c, qs] = l
                acc_ref[c, qs] = acc

    @pl.when(ki < qi)
    def _():
        update(False)

    @pl.when(ki == qi)
    def _():
        update(True)
        lam = _lambda(lam_ref, lam_init)
        for qs in range(t // QS):
            oT = acc_ref[0, qs] / l_ref[0, qs] - lam * (acc_ref[1, qs] / l_ref[1, qs])
            o_ref[0, qs * QS:(qs + 1) * QS, :] = _rms(oT.T, subln_ref[...]) * (1.0 - lam_init)


def _attn_prompt(q, k, v, lam_p, subln, lam_init, t):
    B, L, D = q.shape
    H = DIFF_HEADS
    W = D // H
    n = L // t
    pairs = [(qi, ki) for qi in range(n) for ki in range(qi + 1)]
    qi_tab = jnp.asarray(np.array([p[0] for p in pairs], np.int32))
    ki_tab = jnp.asarray(np.array([p[1] for p in pairs], np.int32))
    qs = min(ATTN_QSUB, t)
    assert t % qs == 0 and t % min(ATTN_KSUB, t) == 0
    kern = functools.partial(_attn_prompt_kernel, t=t, lam_init=lam_init)
    grid_spec = pltpu.PrefetchScalarGridSpec(
        num_scalar_prefetch=2,
        grid=(B, H, len(pairs)),
        in_specs=[pl.BlockSpec(lam_p.shape, lambda b, h, j, qt, kt: (0, 0)),
                  pl.BlockSpec((1, t, W), lambda b, h, j, qt, kt: (b, qt[j], h)),
                  pl.BlockSpec((1, t, W), lambda b, h, j, qt, kt: (b, kt[j], h)),
                  pl.BlockSpec((1, t, W), lambda b, h, j, qt, kt: (b, kt[j], h)),
                  pl.BlockSpec((1, W), lambda b, h, j, qt, kt: (0, 0))],
        out_specs=pl.BlockSpec((1, t, W), lambda b, h, j, qt, kt: (b, qt[j], h)),
        scratch_shapes=[pltpu.VMEM((2, t // qs, 1, qs), F32),
                        pltpu.VMEM((2, t // qs, 1, qs), F32),
                        pltpu.VMEM((2, t // qs, W, qs), F32)],
    )
    return pl.pallas_call(
        kern,
        grid_spec=grid_spec,
        out_shape=jax.ShapeDtypeStruct((B, L, D), F32),
        compiler_params=_params("parallel", "parallel", "arbitrary"),
        name="diff_attn_prompt",
    )(qi_tab, ki_tab, lam_p, q, k, v, subln.reshape(1, W))


def _attn_sample_kernel(pt_ref, lam_ref, qx_ref, *rest, G, n_groups, n_new, lam_init):
    del pt_ref
    ck_refs, cv_refs = rest[:G], rest[G:2 * G]
    kn_ref, vn_ref, subln_ref, o_ref, m_ref, l_ref, acc_ref = rest[2 * G:]
    H = DIFF_HEADS
    p = pl.program_id(1)
    R = qx_ref.shape[1]
    qb = qx_ref[0].astype(BF16)

    @pl.when(p == 0)
    def _():
        m_ref[...] = jnp.full(m_ref.shape, -jnp.inf, F32)
        l_ref[...] = jnp.zeros(l_ref.shape, F32)
        acc_ref[...] = jnp.zeros(acc_ref.shape, F32)

    def head_mask(width):
        rowi = lax.broadcasted_iota(jnp.int32, (R, width), 0)
        coli = lax.broadcasted_iota(jnp.int32, (R, width), 1)
        return rowi, coli, (coli & (H - 1)) == (rowi >> 3)

    def update(s_list, v_list):
        m_prev = m_ref[...]
        m_new = m_prev
        for s in s_list:
            m_new = jnp.maximum(m_new, jnp.max(s, axis=-1, keepdims=True))
        alpha = jnp.exp(m_prev - m_new)
        l_new = alpha * l_ref[...]
        acc = alpha * acc_ref[...]
        for s, v in zip(s_list, v_list):
            pr = jnp.exp(s - m_new)
            l_new = l_new + jnp.sum(pr, axis=-1, keepdims=True)
            acc = acc + _dot(pr, v)
        m_ref[...] = m_new
        l_ref[...] = l_new
        acc_ref[...] = acc

    def page_scores():
        _, _, hm = head_mask(ck_refs[0].shape[1])
        return [jnp.where(hm, _dot_nt(qb, r[0]), -jnp.inf) for r in ck_refs], [r[0] for r in cv_refs]

    if n_groups > 1:
        @pl.when(p < n_groups - 1)
        def _():
            update(*page_scores())

    @pl.when(p == n_groups - 1)
    def _():
        s_list, v_list = page_scores()
        rowi, coli, hm = head_mask(kn_ref.shape[1])
        causal = (coli >> 3) <= (rowi & (n_new - 1))
        s_new = jnp.where(jnp.logical_and(hm, causal), _dot_nt(qb, kn_ref[0]), -jnp.inf)
        update(s_list + [s_new], v_list + [vn_ref[0]])
        lam = _lambda(lam_ref, lam_init)
        o8 = acc_ref[...] / l_ref[...]
        for hd in range(H):
            blk = o8[hd * SUBLANES:(hd + 1) * SUBLANES]
            diff = blk - lam * pltpu.roll(blk, SUBLANES // 2, axis=0)
            o_ref[0, hd * SUBLANES:(hd + 1) * SUBLANES, :] = _rms(diff, subln_ref[...]) * (1.0 - lam_init)


def _attn_sample(qx, cache_k, cache_v, page_table, k_new, v_new, lam_p, subln, lam_init, n_new):
    B, R, W = qx.shape
    rows = cache_k.shape[1]
    n_pages = page_table.shape[1]
    G = math.gcd(n_pages, PAGES_PER_STEP)
    n_groups = n_pages // G
    kern = functools.partial(_attn_sample_kernel, G=G, n_groups=n_groups, n_new=n_new, lam_init=lam_init)

    def page_spec(j):
        return pl.BlockSpec((1, rows, W), lambda b, p, pt: (pt[b, p * G + j], 0, 0))

    def seq(r):
        return pl.BlockSpec((1, r, W), lambda b, p, pt: (b, 0, 0))

    grid_spec = pltpu.PrefetchScalarGridSpec(
        num_scalar_prefetch=1,
        grid=(B, n_groups),
        in_specs=([pl.BlockSpec(lam_p.shape, lambda b, p, pt: (0, 0)), seq(R)]
                  + [page_spec(j) for j in range(G)] * 2
                  + [seq(k_new.shape[1]), seq(v_new.shape[1]),
                     pl.BlockSpec((1, W), lambda b, p, pt: (0, 0))]),
        out_specs=seq(R),
        scratch_shapes=[pltpu.VMEM((R, 1), F32),
                        pltpu.VMEM((R, 1), F32),
                        pltpu.VMEM((R, W), F32)],
    )
    return pl.pallas_call(
        kern,
        grid_spec=grid_spec,
        out_shape=jax.ShapeDtypeStruct((B, R, W), F32),
        compiler_params=_params("parallel", "arbitrary"),
        name="diff_attn_sample",
    )(page_table, lam_p, qx, *([cache_k] * G), *([cache_v] * G), k_new, v_new, subln.reshape(1, W))


def _lambda_init(layer):
    return 0.8 - 0.6 * math.exp(-0.3 * layer)


def _pad_rows(x, rows):
    return jnp.pad(x, [(0, 0), (0, rows - x.shape[1])] + [(0, 0)] * (x.ndim - 2))


def kernel(x_prompt, x_sample, state_delta, state_conv, cache_k, cache_v, page_table, norm_a, w_in_a, conv_w_a, a_log, dt_bias, onorm_a, w_out_a, kv_norm, w_kv, norm_b, w_q_b, lam_b, subln_b, w_out_b, norm_m, w_rg, b_rg, w_re, b_re, w_gate_e, w_up_e, w_down_e, final_norm):
    D = x_prompt.shape[-1]
    H = GDN_HEADS
    CH = 3 * H * GDN_DK
    NH, W = DIFF_HEADS, 2 * DIFF_DH
    n_in = w_in_a.shape[-1]
    n_in_pad = -(-n_in // LANES) * LANES
    depth = norm_m.shape[0]
    n_a = norm_a.shape[0]
    assert depth == 2 and n_a == 1 and norm_b.shape[0] == 1

    w_in_bf = jnp.pad(w_in_a[0], ((0, 0), (0, n_in_pad - n_in))).astype(BF16)
    w_out_a_bf = w_out_a[0].astype(BF16)
    w_q_bf = w_q_b[0].astype(BF16)
    w_kv_bf = w_kv.astype(BF16)
    w_out_b_bf = w_out_b[0].astype(BF16)
    n_route = N_GROUPS + N_EXPERTS
    wr = jnp.pad(jnp.concatenate([w_rg, w_re], axis=-1), ((0, 0), (0, 0), (0, LANES - n_route)))
    br = jnp.pad(jnp.concatenate([b_rg, b_re], axis=-1), ((0, 0), (0, LANES - n_route))).reshape(depth, 1, LANES)
    alog_l = jnp.zeros((1, LANES), F32).at[0, H:2 * H].set(a_log[0])
    dtb_l = jnp.zeros((1, LANES), F32).at[0, H:2 * H].set(dt_bias[0])

    def gdn_layer(h, B, L, conv0, s0):
        T = B * L
        tm = min(TOKEN_TILE, T)
        proj = _norm_matmul(h, norm_a[0], w_in_bf, tm)
        C = min(GDN_CHUNK, -(-L // SUBLANES) * SUBLANES)
        Lp = -(-L // C) * C
        valid_last = L - (Lp - C)
        proj3 = _pad_rows(proj.reshape(B, L, n_in_pad), Lp)
        conv0p = jnp.pad(conv0, ((0, 0), (SUBLANES - (CONV_W - 1), 0), (0, 0)))
        bb = next(n for n in ((4, 2, 1) if C < GDN_CHUNK else (2, 1)) if B % n == 0)
        o3, s_new, cbuf = _gdn_mixer(proj3, conv0p, s0, conv_w_a[0], alog_l, dtb_l, onorm_a[0], C, valid_last, bb)
        return o3[:, :L].reshape(T, D), s_new[None], cbuf[:, SUBLANES - (CONV_W - 1):][None]

    def attn_layer(h, B, L, past_len, paged):
        T = B * L
        tm = min(TOKEN_TILE, T)
        pos = past_len + jnp.arange(L, dtype=jnp.int32)
        cos_t, sin_t = _rope_tables(pos)
        if L % tm != 0:
            cos_t = jnp.tile(cos_t, (T // L, 1))
            sin_t = jnp.tile(sin_t, (T // L, 1))
        q, k, v = _qkv_proj(h, norm_b[0], kv_norm, w_q_bf, w_kv_bf, cos_t, sin_t, tm)
        lam_init = _lambda_init(n_a)
        if paged is None:
            o = _attn_prompt(q.reshape(B, L, D), k.reshape(B, L, D), v.reshape(B, L, D),
                             lam_b[0], subln_b[0], lam_init, min(ATTN_TILE, L))
            o = o.reshape(T, D)
        else:
            ck, cv, pt = paged
            assert 2 * L == SUBLANES and NH == SUBLANES
            q5 = q.reshape(B, L, NH, 2, DIFF_DH).transpose(0, 2, 1, 3, 4)
            sel = jnp.eye(2, dtype=F32)[None, None, :, None, :, None]
            qx = (q5[:, :, None] * sel).reshape(B, NH * 2 * L, W)
            o = _attn_sample(qx, ck, cv, pt, k.reshape(B, L * NH, W), v.reshape(B, L * NH, W),
                             lam_b[0], subln_b[0], lam_init, L)
            o = o.reshape(B, NH, 2 * L, W)[:, :, :L].transpose(0, 2, 1, 3).reshape(T, D)
        kshape = (B, L, NH, W)
        return o, k.reshape(kshape), v.reshape(kshape)

    def moe(op, os, hp, hs, wo_bf, layer, last):
        return _hier_moe(op, os, hp, hs, wo_bf, norm_m[layer], wr[layer], br[layer], w_gate_e, w_up_e, w_down_e,
                         layer, final_norm, last)

    Bp, Lq, _ = x_prompt.shape
    Bs, Ls, _ = x_sample.shape
    hp = x_prompt.reshape(Bp * Lq, D)
    hs = x_sample.reshape(Bs * Ls, D)

    op, sd_p, sc_p = gdn_layer(hp, Bp, Lq, jnp.zeros((Bp, CONV_W - 1, CH), F32),
                               jnp.zeros((Bp, H, GDN_DK, GDN_DK), F32))
    os, sd_s, sc_s = gdn_layer(hs, Bs, Ls, state_conv[0], state_delta[0])
    hp, hs = moe(op, os, hp, hs, w_out_a_bf, 0, False)

    n_phys, page, nh, w = cache_k.shape
    past_len = page_table.shape[1] * page
    ck = cache_k.reshape(n_phys, page * nh, w)
    cv = cache_v.reshape(n_phys, page * nh, w)
    op, k_p, v_p = attn_layer(hp, Bp, Lq, 0, None)
    os, k_s, v_s = attn_layer(hs, Bs, Ls, past_len, (ck, cv, page_table))
    y_p, y_s = moe(op, os, hp, hs, w_out_b_bf, 1, True)
    return (y_p.reshape(Bp, Lq, D), y_s.reshape(Bs, Ls, D), sd_p, sc_p, k_p, v_p, sd_s, sc_s, k_s, v_s)
```

```python
import functools
import math

import numpy as np
import jax
import jax.numpy as jnp
from jax import lax
from jax.experimental import pallas as pl
from jax.experimental.pallas import tpu as pltpu

F32 = jnp.float32
BF16 = jnp.bfloat16
EPS = 1e-6
LANES = 128
SUBLANES = 8
VMEM_LIMIT = 56 * 1024 * 1024

CONV_W = 4
GDN_HEADS = 8
GDN_DK = 128
GDN_CHUNK = 64
DIFF_HEADS = 8
DIFF_DH = 64
ROPE_THETA = 10000.0
N_GROUPS = 4
EXP_PER_GROUP = 8
N_EXPERTS = N_GROUPS * EXP_PER_GROUP
TOP_K = 2
MOE_BLOCK = 128
ROUTE_OFF = N_GROUPS
ATTN_TILE = 2048
PAGES_PER_STEP = 16
ATTN_QSUB = LANES
ATTN_KSUB = 256
LOG2E = 1.4426950408889634
DMA_UNROLL = 8
ROUTER_TILE = 512
GATHER_AHEAD = 3
GATHER_SLOTS = GATHER_AHEAD + 1
COMBINE_TILE = 256
TOKEN_TILE = 512


def _params(*sem):
    return pltpu.CompilerParams(dimension_semantics=sem, vmem_limit_bytes=VMEM_LIMIT)


def _dot(a, b):
    return jnp.dot(a.astype(BF16), b.astype(BF16), preferred_element_type=F32)


def _dot_nt(a, b):
    return lax.dot_general(a.astype(BF16), b.astype(BF16), (((1,), (1,)), ((), ())),
                           preferred_element_type=F32)


def _dot_tn(a, b):
    return lax.dot_general(a.astype(BF16), b.astype(BF16), (((0,), (0,)), ((), ())),
                           preferred_element_type=F32)


def _split2(x):
    hi = x.astype(BF16)
    lo = (x - hi.astype(F32)).astype(BF16)
    return hi, lo


def _dot3(a, b):
    ah, al = _split2(a)
    bh, bl = _split2(b)
    d = functools.partial(jnp.dot, preferred_element_type=F32)
    return d(ah, bh) + (d(ah, bl) + d(al, bh))


def _dot_exact_lhs(a_bf, b):
    b1 = b.astype(BF16)
    r1 = b - b1.astype(F32)
    b2 = r1.astype(BF16)
    b3 = (r1 - b2.astype(F32)).astype(BF16)
    d = functools.partial(jnp.dot, preferred_element_type=F32)
    return d(a_bf, b1) + (d(a_bf, b2) + d(a_bf, b3))


def _sigmoid(x):
    return 1.0 / (1.0 + jnp.exp(-x))


def _silu(x):
    return x * _sigmoid(x)


def _softplus(x):
    return jnp.maximum(x, 0.0) + jnp.log1p(jnp.exp(-jnp.abs(x)))


def _rms(x, g):
    return x * lax.rsqrt(jnp.mean(x * x, axis=-1, keepdims=True) + EPS) * g


def _norm_mm_kernel(x_ref, g_ref, w_ref, o_ref):
    xn = _rms(x_ref[...], g_ref[...])
    o_ref[...] = jnp.dot(xn.astype(BF16), w_ref[...], preferred_element_type=F32)


def _norm_matmul(x, g, w_bf, tm):
    T, D = x.shape
    N = w_bf.shape[1]
    return pl.pallas_call(
        _norm_mm_kernel,
        grid=(T // tm,),
        in_specs=[pl.BlockSpec((tm, D), lambda i: (i, 0)),
                  pl.BlockSpec((1, D), lambda i: (0, 0)),
                  pl.BlockSpec((D, N), lambda i: (0, 0))],
        out_specs=pl.BlockSpec((tm, N), lambda i: (i, 0)),
        out_shape=jax.ShapeDtypeStruct((T, N), F32),
        compiler_params=_params("parallel"),
        name="norm_matmul",
    )(x, g.reshape(1, D), w_bf)


def _gdn_kernel(proj_ref, conv0_ref, s0_ref, cw_ref, alog_ref, dtb_ref, onorm_ref,
                o_ref, sout_ref, cout_ref, ext_ref, qkv_ref, s_ref, *, C, nc, valid_last, bb):
    H, DK = GDN_HEADS, GDN_DK
    QK = H * DK
    CH = 3 * QK
    c = pl.program_id(1)
    masked = valid_last < C

    @pl.when(c == 0)
    def _():
        ext_ref[:, 0:SUBLANES, :] = conv0_ref[...]
        s_ref[...] = s0_ref[...]

    row1 = lax.broadcasted_iota(jnp.int32, (C, 1), 0)
    row = lax.broadcasted_iota(jnp.int32, (C, C), 0)
    col = lax.broadcasted_iota(jnp.int32, (C, C), 1)
    incl = row >= col
    strict = row > col
    eye = jnp.where(row == col, 1.0, 0.0).astype(F32)
    tril_bf = jnp.where(incl, 1.0, 0.0).astype(BF16)
    if masked:
        valid = jnp.logical_or(row1 < valid_last, c < nc - 1)

    seqs = []
    for b in range(bb):
        ext_ref[b, SUBLANES:SUBLANES + C, :] = proj_ref[b, :, 0:CH]
        base = SUBLANES - (CONV_W - 1)
        for j in range(CH // LANES):
            sl = slice(j * LANES, (j + 1) * LANES)
            acc = ext_ref[b, base:base + C, sl] * cw_ref[0:1, sl]
            for i in range(1, CONV_W):
                acc = acc + ext_ref[b, base + i:base + i + C, sl] * cw_ref[i:i + 1, sl]
            qkv_ref[b, :, sl] = _silu(acc)

        @pl.when(c == nc - 1)
        def _():
            cout_ref[b] = ext_ref[b, valid_last:valid_last + SUBLANES, :]

        ext_ref[b, 0:SUBLANES, :] = ext_ref[b, C:C + SUBLANES, :]

        ba = proj_ref[b, :, CH + QK:CH + QK + LANES]
        beta_t = _sigmoid(ba)
        g_t = -jnp.exp(alog_ref[...]) * _softplus(ba + dtb_ref[...])
        if masked:
            beta_t = jnp.where(valid, beta_t, 0.0)
            g_t = jnp.where(valid, g_t, 0.0)
        G = _dot_exact_lhs(tril_bf, g_t)
        g_last = G[C - 1:C, :]
        seqs.append(dict(beta=beta_t, G=G, GT=G.T, expG=jnp.exp(G),
                         exp_last=jnp.exp(g_last), k_scale=jnp.exp(g_last - G)))

    chains = [(b, hd) for b in range(bb) for hd in range(H)]
    st = []
    for b, hd in chains:
        sq = seqs[b]
        q = qkv_ref[b, :, hd * DK:(hd + 1) * DK]
        k = qkv_ref[b, :, QK + hd * DK:QK + (hd + 1) * DK]
        v = qkv_ref[b, :, 2 * QK + hd * DK:2 * QK + (hd + 1) * DK]
        q = q * lax.rsqrt(jnp.sum(q * q, axis=-1, keepdims=True) + EPS) * (DK ** -0.5)
        k = k * lax.rsqrt(jnp.sum(k * k, axis=-1, keepdims=True) + EPS)
        if masked:
            k = jnp.where(valid, k, 0.0)
        bcol = sq["beta"][:, hd:hd + 1]
        gcol = sq["G"][:, H + hd:H + hd + 1]
        grow = sq["GT"][H + hd:H + hd + 1, :]
        eg = sq["expG"][:, H + hd:H + hd + 1]
        decay = jnp.exp(jnp.where(incl, gcol - grow, -jnp.inf))
        st.append(dict(q=q, k=k, decay=decay, eg=eg, bcol=bcol,
                       rhs=jnp.concatenate([bcol * v, (bcol * eg) * k], axis=1)))
    for d in st:
        kk = _dot_nt(d["k"], d["k"])
        d["p"] = -jnp.where(strict, d["bcol"] * kk * d["decay"], 0.0)
        d["t"] = eye + d["p"]
    for _ in range(int(math.log2(C)) - 1):
        for d in st:
            d["p"] = _dot3(d["p"], d["p"])
        for d in st:
            d["t"] = d["t"] + _dot3(d["t"], d["p"])
    for d in st:
        d["sol"] = _dot3(d["t"], d["rhs"])
    for d, (b, hd) in zip(st, chains):
        d["S"] = s_ref[b, hd]
        d["u"] = d["sol"][:, :DK] - _dot(d["sol"][:, DK:], d["S"])
    for d in st:
        d["qk"] = _dot_nt(d["q"], d["k"]) * d["decay"]
    for d in st:
        d["o"] = _dot(d["q"] * d["eg"], d["S"]) + _dot(d["qk"], d["u"])
    for d, (b, hd) in zip(st, chains):
        sq = seqs[b]
        s_ref[b, hd] = (sq["exp_last"][:, H + hd:H + hd + 1] * d["S"]
                        + _dot_tn(d["k"] * sq["k_scale"][:, H + hd:H + hd + 1], d["u"]))
    for d, (b, hd) in zip(st, chains):
        z = proj_ref[b, :, CH + hd * DK:CH + (hd + 1) * DK]
        o_ref[b, :, hd * DK:(hd + 1) * DK] = _rms(d["o"], onorm_ref[...]) * _silu(z)

    @pl.when(c == nc - 1)
    def _():
        sout_ref[...] = s_ref[...]


def _gdn_mixer(proj, conv0p, s0, conv_w, alog_l, dtb_l, onorm, C, valid_last, bb):
    B, Lp, NP = proj.shape
    nc = Lp // C
    H, DK = GDN_HEADS, GDN_DK
    D = H * DK
    CH = 3 * H * DK
    kern = functools.partial(_gdn_kernel, C=C, nc=nc, valid_last=valid_last, bb=bb)
    return pl.pallas_call(
        kern,
        grid=(B // bb, nc),
        in_specs=[pl.BlockSpec((bb, C, NP), lambda b, c: (b, c, 0)),
                  pl.BlockSpec((bb, SUBLANES, CH), lambda b, c: (b, 0, 0)),
                  pl.BlockSpec((bb, H, DK, DK), lambda b, c: (b, 0, 0, 0)),
                  pl.BlockSpec((CONV_W, CH), lambda b, c: (0, 0)),
                  pl.BlockSpec((1, LANES), lambda b, c: (0, 0)),
                  pl.BlockSpec((1, LANES), lambda b, c: (0, 0)),
                  pl.BlockSpec((1, DK), lambda b, c: (0, 0))],
        out_specs=[pl.BlockSpec((bb, C, D), lambda b, c: (b, c, 0)),
                   pl.BlockSpec((bb, H, DK, DK), lambda b, c: (b, 0, 0, 0)),
                   pl.BlockSpec((bb, SUBLANES, CH), lambda b, c: (b, 0, 0))],
        out_shape=[jax.ShapeDtypeStruct((B, Lp, D), F32),
                   jax.ShapeDtypeStruct((B, H, DK, DK), F32),
                   jax.ShapeDtypeStruct((B, SUBLANES, CH), F32)],
        scratch_shapes=[pltpu.VMEM((bb, SUBLANES + C, CH), F32),
                        pltpu.VMEM((bb, C, CH), F32),
                        pltpu.VMEM((bb, H, DK, DK), F32)],
        compiler_params=_params("parallel", "arbitrary"),
        name="gdn_mixer",
    )(proj, conv0p, s0, conv_w, alog_l, dtb_l, onorm.reshape(1, DK))


def _router_kernel(op_ref, os_ref, hp_ref, hs_ref, wo_ref, g_ref, wr_ref, br_ref,
                   hop_ref, hos_ref, xn_ref, gate_ref, route_ref, cnt_ref, carry_ref, *, n_p):
    i = pl.program_id(0)

    @pl.when(i == 0)
    def _():
        carry_ref[...] = jnp.zeros(carry_ref.shape, F32)

    prompt = i < n_p
    x = (jnp.where(prompt, hp_ref[...], hs_ref[...])
         + jnp.dot(jnp.where(prompt, op_ref[...], os_ref[...]).astype(BF16), wo_ref[...],
                   preferred_element_type=F32))

    @pl.when(prompt)
    def _():
        hop_ref[...] = x

    @pl.when(jnp.logical_not(prompt))
    def _():
        hos_ref[...] = x

    xn = _rms(x, g_ref[...])
    xn_ref[...] = xn
    logits = _dot3(xn, wr_ref[...]) + br_ref[...]
    tm = logits.shape[0]
    lane = lax.broadcasted_iota(jnp.int32, (tm, LANES), 1).astype(F32)
    neg = -jnp.inf
    big = float(LANES)

    lg = jnp.where(lane < N_GROUPS, logits, neg)
    eg = jnp.exp(lg - jnp.max(lg, axis=-1, keepdims=True))
    pg = eg / jnp.sum(eg, axis=-1, keepdims=True)
    pmax = jnp.max(pg, axis=-1, keepdims=True)
    g_sel = jnp.min(jnp.where(pg == pmax, lane, big), axis=-1, keepdims=True)

    lo = ROUTE_OFF + EXP_PER_GROUP * g_sel
    in_grp = jnp.logical_and(lane >= lo, lane < lo + EXP_PER_GROUP)
    le = jnp.where(in_grp, logits, neg)
    ee = jnp.exp(le - jnp.max(le, axis=-1, keepdims=True))
    pe = ee / jnp.sum(ee, axis=-1, keepdims=True)
    pe = jnp.where(in_grp, pe, -1.0)
    p1 = jnp.max(pe, axis=-1, keepdims=True)
    i1 = jnp.min(jnp.where(pe == p1, lane, big), axis=-1, keepdims=True)
    pe2 = jnp.where(lane == i1, -1.0, pe)
    p2 = jnp.max(pe2, axis=-1, keepdims=True)
    i2 = jnp.min(jnp.where(pe2 == p2, lane, big), axis=-1, keepdims=True)
    den = p1 + p2
    gate_ref[...] = jnp.where(lane == 0, pmax * p1 / den, jnp.where(lane == 1, pmax * p2 / den, 0.0))

    e1 = i1 - ROUTE_OFF
    e2 = i2 - ROUTE_OFF
    oh1 = jnp.where(lane == e1, 1.0, 0.0)
    oh2 = jnp.where(lane == e2, 1.0, 0.0)
    both = oh1 + oh2
    earlier = (lax.broadcasted_iota(jnp.int32, (tm, tm), 0) > lax.broadcasted_iota(jnp.int32, (tm, tm), 1))
    before = (jnp.dot(jnp.where(earlier, 1.0, 0.0).astype(BF16), both.astype(BF16), preferred_element_type=F32)
              + carry_ref[0:1, :])
    r1 = jnp.sum(oh1 * before, axis=-1, keepdims=True)
    r2 = jnp.sum(oh2 * before, axis=-1, keepdims=True)
    carry_ref[...] = carry_ref[...] + jnp.sum(both, axis=0, keepdims=True)
    cnt_ref[...] = carry_ref[...]
    route_ref[...] = jnp.where(lane == 0, e1, jnp.where(lane == 1, e2, jnp.where(
        lane == 2, r1, jnp.where(lane == 3, r2, 0.0)))).astype(jnp.int32)


def _router(op, os, hp, hs, wo_bf, g, wr, br, tm):
    Tp, D = hp.shape
    Ts = hs.shape[0]
    n_p, n_s = Tp // tm, Ts // tm
    T = Tp + Ts
    kern = functools.partial(_router_kernel, n_p=n_p)
    p_blk = pl.BlockSpec((tm, D), lambda i: (jnp.minimum(i, n_p - 1), 0))
    s_blk = pl.BlockSpec((tm, D), lambda i: (jnp.maximum(i - n_p, 0), 0))
    row = pl.BlockSpec((tm, D), lambda i: (i, 0))
    lanes = pl.BlockSpec((tm, LANES), lambda i: (i, 0))
    return pl.pallas_call(
        kern,
        grid=(n_p + n_s,),
        in_specs=[p_blk, s_blk, p_blk, s_blk,
                  pl.BlockSpec((D, D), lambda i: (0, 0)),
                  pl.BlockSpec((1, D), lambda i: (0, 0)),
                  pl.BlockSpec((D, LANES), lambda i: (0, 0)),
                  pl.BlockSpec((1, LANES), lambda i: (0, 0))],
        out_specs=[p_blk, s_blk, row, lanes, lanes, pl.BlockSpec((SUBLANES, LANES), lambda i: (0, 0))],
        out_shape=[jax.ShapeDtypeStruct((Tp, D), F32),
                   jax.ShapeDtypeStruct((Ts, D), F32),
                   jax.ShapeDtypeStruct((T, D), F32),
                   jax.ShapeDtypeStruct((T, LANES), F32),
                   jax.ShapeDtypeStruct((T, LANES), jnp.int32),
                   jax.ShapeDtypeStruct((SUBLANES, LANES), F32)],
        scratch_shapes=[pltpu.VMEM((SUBLANES, LANES), F32)],
        compiler_params=_params("arbitrary"),
        name="moe_router",
    )(op, os, hp, hs, wo_bf, g.reshape(1, D), wr, br)


def _row_copy(src_hbm, row, dst, slot_idx, sem):
    return pltpu.make_async_copy(src_hbm.at[pl.ds(row, 1)], dst.at[slot_idx], sem)


def _expert_kernel(be_ref, rt_ref, nu_ref, xn_hbm, wg_ref, wu_ref, wd_ref, ys_ref,
                   xbuf, wg_bf, wu_bf, wd_bf, sem):
    i = pl.program_id(0)
    n_used = nu_ref[0]

    def issue(blk, slot):
        for r in range(MOE_BLOCK):
            _row_copy(xn_hbm, rt_ref[blk * MOE_BLOCK + r], xbuf, (slot, pl.ds(r, 1)), sem.at[slot]).start()

    def wait(slot):
        for r in range(MOE_BLOCK):
            _row_copy(xn_hbm, 0, xbuf, (slot, pl.ds(r, 1)), sem.at[slot]).wait()

    def issue_pos(pos):
        issue(jnp.minimum(pos, n_used - 1), pos % GATHER_SLOTS)

    @pl.when(i == 0)
    def _():
        for k in range(GATHER_AHEAD):
            issue_pos(jnp.int32(k))

    @pl.when(jnp.logical_or(i == 0, be_ref[i] != be_ref[jnp.maximum(i - 1, 0)]))
    def _():
        wg_bf[...] = wg_ref[0, 0].astype(BF16)
        wu_bf[...] = wu_ref[0, 0].astype(BF16)
        wd_bf[...] = wd_ref[0, 0].astype(BF16)

    @pl.when(i < n_used)
    def _():
        slot = i % GATHER_SLOTS
        wait(slot)
        issue_pos(i + GATHER_AHEAD)
        x = xbuf[slot].astype(BF16)
        gt = jnp.dot(x, wg_bf[...], preferred_element_type=F32)
        up = jnp.dot(x, wu_bf[...], preferred_element_type=F32)
        hid = _silu(gt) * up
        ys_ref[...] = jnp.dot(hid.astype(BF16), wd_bf[...], preferred_element_type=F32)

    @pl.when(i == n_used - 1)
    def _():
        for k in range(GATHER_AHEAD):
            wait((n_used + k) % GATHER_SLOTS)

    @pl.when(i >= n_used)
    def _():
        ys_ref[...] = jnp.zeros(ys_ref.shape, F32)


def _experts(xn, block_e, row_tok, n_used, w_gate, w_up, w_down, layer):
    T, D = xn.shape
    FF = w_gate.shape[-1]
    n_blocks = block_e.shape[0]
    P = n_blocks * MOE_BLOCK
    grid_spec = pltpu.PrefetchScalarGridSpec(
        num_scalar_prefetch=3,
        grid=(n_blocks,),
        in_specs=[pl.BlockSpec(memory_space=pl.ANY),
                  pl.BlockSpec((1, 1, D, FF), lambda i, be, rt, nu: (layer, be[i], 0, 0)),
                  pl.BlockSpec((1, 1, D, FF), lambda i, be, rt, nu: (layer, be[i], 0, 0)),
                  pl.BlockSpec((1, 1, FF, D), lambda i, be, rt, nu: (layer, be[i], 0, 0))],
        out_specs=pl.BlockSpec((MOE_BLOCK, D), lambda i, be, rt, nu: (i, 0)),
        scratch_shapes=[pltpu.VMEM((GATHER_SLOTS, MOE_BLOCK, D), F32),
                        pltpu.VMEM((D, FF), BF16),
                        pltpu.VMEM((D, FF), BF16),
                        pltpu.VMEM((FF, D), BF16),
                        pltpu.SemaphoreType.DMA((GATHER_SLOTS,))],
    )
    return pl.pallas_call(
        _expert_kernel,
        grid_spec=grid_spec,
        out_shape=jax.ShapeDtypeStruct((P, D), F32),
        compiler_params=_params("arbitrary"),
        name="moe_experts",
    )(block_e, row_tok, n_used, xn, w_gate, w_up, w_down)


def _combine_kernel(dest_ref, ys_hbm, hp_ref, hs_ref, gate_ref, fn_ref, op_ref, os_ref, ybuf, sem,
                    *, tm, n_p, n_steps, final_norm):
    i = pl.program_id(0)
    n = pl.num_programs(0)

    def issue(step, slot):
        def body(r, carry):
            for kk in range(TOP_K):
                d = dest_ref[(step * tm + r) * TOP_K + kk]
                _row_copy(ys_hbm, d, ybuf, (slot, kk, pl.ds(r, 1)), sem.at[slot]).start()
            return carry
        lax.fori_loop(0, tm, body, 0, unroll=DMA_UNROLL // TOP_K)

    def wait(slot):
        for r in range(tm):
            for kk in range(TOP_K):
                _row_copy(ys_hbm, 0, ybuf, (slot, kk, pl.ds(r, 1)), sem.at[slot]).wait()

    @pl.when(i == 0)
    def _():
        for k in range(min(GATHER_AHEAD, n_steps)):
            issue(k, k)

    @pl.when(i + GATHER_AHEAD < n)
    def _():
        issue(i + GATHER_AHEAD, (i + GATHER_AHEAD) % GATHER_SLOTS)

    slot = i % GATHER_SLOTS
    wait(slot)
    gate = gate_ref[...]
    moe = ybuf[slot, 0] * gate[:, 0:1] + ybuf[slot, 1] * gate[:, 1:2]

    def finish(h_ref, o_ref):
        y = h_ref[...] + moe
        if final_norm:
            y = _rms(y, fn_ref[...])
        o_ref[...] = y

    @pl.when(i < n_p)
    def _():
        finish(hp_ref, op_ref)

    @pl.when(i >= n_p)
    def _():
        finish(hs_ref, os_ref)


def _combine(ys, dest, hp, hs, gates, fnorm, tm, final_norm):
    Tp, D = hp.shape
    Ts = hs.shape[0]
    n_p, n_s = Tp // tm, Ts // tm
    kern = functools.partial(_combine_kernel, tm=tm, n_p=n_p, n_steps=n_p + n_s, final_norm=final_norm)
    p_blk = pl.BlockSpec((tm, D), lambda i, d: (jnp.minimum(i, n_p - 1), 0))
    s_blk = pl.BlockSpec((tm, D), lambda i, d: (jnp.maximum(i - n_p, 0), 0))
    grid_spec = pltpu.PrefetchScalarGridSpec(
        num_scalar_prefetch=1,
        grid=(n_p + n_s,),
        in_specs=[pl.BlockSpec(memory_space=pl.ANY), p_blk, s_blk,
                  pl.BlockSpec((tm, LANES), lambda i, d: (i, 0)),
                  pl.BlockSpec((1, D), lambda i, d: (0, 0))],
        out_specs=[p_blk, s_blk],
        scratch_shapes=[pltpu.VMEM((GATHER_SLOTS, TOP_K, tm, D), F32),
                        pltpu.SemaphoreType.DMA((GATHER_SLOTS,))],
    )
    return pl.pallas_call(
        kern,
        grid_spec=grid_spec,
        out_shape=[jax.ShapeDtypeStruct((Tp, D), F32), jax.ShapeDtypeStruct((Ts, D), F32)],
        compiler_params=_params("arbitrary"),
        name="moe_combine",
    )(dest, ys, hp, hs, gates, fnorm.reshape(1, D))


def _route_tables(route, counts):
    T = route.shape[0]
    A = T * TOP_K
    flat_e = route[:, 0:TOP_K].reshape(A)
    rank = route[:, TOP_K:2 * TOP_K].reshape(A)
    padded = (counts + MOE_BLOCK - 1) // MOE_BLOCK * MOE_BLOCK
    pad_end = jnp.cumsum(padded)
    pad_start = pad_end - padded
    dest = (pad_start[flat_e] + rank).astype(jnp.int32)
    n_blocks = -(-A // MOE_BLOCK) + N_EXPERTS
    P = n_blocks * MOE_BLOCK
    row_tok = jnp.zeros((P,), jnp.int32).at[dest].set(jnp.arange(A, dtype=jnp.int32) // TOP_K,
                                                      unique_indices=True, mode='promise_in_bounds')
    starts = jnp.arange(n_blocks, dtype=jnp.int32) * MOE_BLOCK
    block_e = jnp.minimum(jnp.sum((pad_end[None, :] <= starts[:, None]).astype(jnp.int32), axis=1),
                          N_EXPERTS - 1)
    n_used = (pad_end[-1] // MOE_BLOCK).astype(jnp.int32).reshape(1)
    return dest, row_tok, block_e, n_used


def _hier_moe(op, os, hp, hs, wo_bf, norm_g, wr, br, w_gate, w_up, w_down, layer, fnorm, final_norm):
    t_all = math.gcd(hp.shape[0], hs.shape[0])
    hp, hs, xn, gates, route, cnt = _router(op, os, hp, hs, wo_bf, norm_g, wr, br, math.gcd(ROUTER_TILE, t_all))
    counts = cnt[0, :N_EXPERTS].astype(jnp.int32)
    dest, row_tok, block_e, n_used = _route_tables(route, counts)
    ys = _experts(xn, block_e, row_tok, n_used, w_gate, w_up, w_down, layer)
    return _combine(ys, dest, hp, hs, gates, fnorm, math.gcd(COMBINE_TILE, t_all), final_norm)


def _rope_tile(x, cos, sin_signed, lane):
    half = DIFF_DH // 2
    rot = jnp.where((lane & (DIFF_DH - 1)) < half,
                    pltpu.roll(x, LANES - half, axis=1),
                    pltpu.roll(x, half, axis=1))
    return x * cos + rot * sin_signed


def _qkv_kernel(h_ref, gq_ref, gkv_ref, wq_ref, wkv_ref, cos_ref, sin_ref, q_ref, k_ref, v_ref):
    x = h_ref[...]
    D = x.shape[1]
    xs = x * lax.rsqrt(jnp.mean(x * x, axis=-1, keepdims=True) + EPS)
    q = jnp.dot((xs * gq_ref[...]).astype(BF16), wq_ref[...], preferred_element_type=F32)
    kv = jnp.dot((xs * gkv_ref[...]).astype(BF16), wkv_ref[...], preferred_element_type=F32)
    cos = cos_ref[...]
    sin = sin_ref[...]
    lane = lax.broadcasted_iota(jnp.int32, cos.shape, 1)
    for hd in range(D // LANES):
        sl = slice(hd * LANES, (hd + 1) * LANES)
        q_ref[:, sl] = _rope_tile(q[:, sl], cos, sin, lane) * (DIFF_DH ** -0.5)
        k_ref[:, sl] = _rope_tile(kv[:, sl], cos, sin, lane)
    v_ref[...] = kv[:, D:]


def _qkv_proj(h, gq, gkv, wq_bf, wkv_bf, cos_t, sin_t, tm):
    T, D = h.shape
    nt = cos_t.shape[0] // tm
    row = pl.BlockSpec((tm, D), lambda i: (i, 0))
    vec = pl.BlockSpec((1, D), lambda i: (0, 0))
    tab = pl.BlockSpec((tm, LANES), lambda i: (i % nt, 0))
    return pl.pallas_call(
        _qkv_kernel,
        grid=(T // tm,),
        in_specs=[row, vec, vec,
                  pl.BlockSpec((D, D), lambda i: (0, 0)),
                  pl.BlockSpec((D, 2 * D), lambda i: (0, 0)),
                  tab, tab],
        out_specs=[row, row, row],
        out_shape=[jax.ShapeDtypeStruct((T, D), F32)] * 3,
        compiler_params=_params("parallel"),
        name="qkv_proj",
    )(h, gq.reshape(1, D), gkv.reshape(1, D), wq_bf, wkv_bf, cos_t, sin_t)


def _rope_tables(pos):
    half = DIFF_DH // 2
    inv = ROPE_THETA ** (-jnp.arange(half, dtype=F32) / half)
    ang = pos.astype(F32)[:, None] * inv[None, :]
    cos = jnp.cos(ang)
    sin = jnp.sin(ang)
    reps = LANES // DIFF_DH
    return (jnp.tile(jnp.concatenate([cos, cos], axis=1), (1, reps)),
            jnp.tile(jnp.concatenate([-sin, sin], axis=1), (1, reps)))


def _lambda(lam_ref, lam_init):
    lp = lam_ref[...]
    a = jnp.sum(lp[0:1, :] * lp[1:2, :], axis=-1, keepdims=True)
    b = jnp.sum(lp[2:3, :] * lp[3:4, :], axis=-1, keepdims=True)
    return jnp.exp(a) - jnp.exp(b) + lam_init


def _attn_prompt_kernel(qi_ref, ki_ref, lam_ref, q_ref, k_ref, v_ref, subln_ref, o_ref,
                        m_ref, l_ref, acc_ref, *, t, lam_init):
    j = pl.program_id(2)
    qi = qi_ref[j]
    ki = ki_ref[j]

    @pl.when(ki == 0)
    def _():
        m_ref[...] = jnp.full(m_ref.shape, -jnp.inf, F32)
        l_ref[...] = jnp.zeros(l_ref.shape, F32)
        acc_ref[...] = jnp.zeros(acc_ref.shape, F32)

    QS, KS = min(ATTN_QSUB, t), min(ATTN_KSUB, t)

    def update(diagonal):
        q = q_ref[0] * LOG2E
        kb = k_ref[0].astype(BF16)
        vT = v_ref[0].T.astype(BF16)
        lane = lax.broadcasted_iota(jnp.int32, q.shape, 1)
        for c in range(2):
            qc = jnp.where((lane < DIFF_DH) if c == 0 else (lane >= DIFF_DH), q, 0.0).astype(BF16)
            for qs in range(t // QS):
                qsub = qc[qs * QS:(qs + 1) * QS]
                m = m_ref[c, qs]
                l = l_ref[c, qs]
                acc = acc_ref[c, qs]
                for ks in range(t // KS):
                    if diagonal and ks * KS > qs * QS + QS - 1:
                        continue
                    sT = _dot_nt(kb[ks * KS:(ks + 1) * KS], qsub)
                    if diagonal and (ks + 1) * KS - 1 > qs * QS:
                        keyi = ks * KS + lax.broadcasted_iota(jnp.int32, (KS, QS), 0)
                        qidx = qs * QS + lax.broadcasted_iota(jnp.int32, (KS, QS), 1)
                        sT = jnp.where(keyi <= qidx, sT, -jnp.inf)
                    m_new = jnp.maximum(m, jnp.max(sT, axis=0, keepdims=True))
                    alpha = jnp.exp2(m - m_new)
                    p = jnp.exp2(sT - m_new)
                    l = alpha * l + jnp.sum(p, axis=0, keepdims=True)
                    acc = alpha * acc + jnp.dot(vT[:, ks * KS:(ks + 1) * KS], p.astype(BF16),
                                                preferred_element_type=F32)
                    m = m_new
                m_ref[c, qs] = m
                l_ref[c, qs] = l
                acc_ref[c, qs] = acc

    @pl.when(ki < qi)
    def _():
        update(False)

    @pl.when(ki == qi)
    def _():
        update(True)
        lam = _lambda(lam_ref, lam_init)
        for qs in range(t // QS):
            oT = acc_ref[0, qs] / l_ref[0, qs] - lam * (acc_ref[1, qs] / l_ref[1, qs])
            o_ref[0, qs * QS:(qs + 1) * QS, :] = _rms(oT.T, subln_ref[...]) * (1.0 - lam_init)


def _attn_prompt(q, k, v, lam_p, subln, lam_init, t):
    B, L, D = q.shape
    H = DIFF_HEADS
    W = D // H
    n = L // t
    pairs = [(qi, ki) for qi in range(n) for ki in range(qi + 1)]
    qi_tab = jnp.asarray(np.array([p[0] for p in pairs], np.int32))
    ki_tab = jnp.asarray(np.array([p[1] for p in pairs], np.int32))
    qs = min(ATTN_QSUB, t)
    assert t % qs == 0 and t % min(ATTN_KSUB, t) == 0
    kern = functools.partial(_attn_prompt_kernel, t=t, lam_init=lam_init)
    grid_spec = pltpu.PrefetchScalarGridSpec(
        num_scalar_prefetch=2,
        grid=(B, H, len(pairs)),
        in_specs=[pl.BlockSpec(lam_p.shape, lambda b, h, j, qt, kt: (0, 0)),
                  pl.BlockSpec((1, t, W), lambda b, h, j, qt, kt: (b, qt[j], h)),
                  pl.BlockSpec((1, t, W), lambda b, h, j, qt, kt: (b, kt[j], h)),
                  pl.BlockSpec((1, t, W), lambda b, h, j, qt, kt: (b, kt[j], h)),
                  pl.BlockSpec((1, W), lambda b, h, j, qt, kt: (0, 0))],
        out_specs=pl.BlockSpec((1, t, W), lambda b, h, j, qt, kt: (b, qt[j], h)),
        scratch_shapes=[pltpu.VMEM((2, t // qs, 1, qs), F32),
                        pltpu.VMEM((2, t // qs, 1, qs), F32),
                        pltpu.VMEM((2, t // qs, W, qs), F32)],
    )
    return pl.pallas_call(
        kern,
        grid_spec=grid_spec,
        out_shape=jax.ShapeDtypeStruct((B, L, D), F32),
        compiler_params=_params("parallel", "parallel", "arbitrary"),
        name="diff_attn_prompt",
    )(qi_tab, ki_tab, lam_p, q, k, v, subln.reshape(1, W))


def _attn_sample_kernel(pt_ref, lam_ref, qx_ref, *rest, G, n_groups, n_new, lam_init):
    del pt_ref
    ck_refs, cv_refs = rest[:G], rest[G:2 * G]
    kn_ref, vn_ref, subln_ref, o_ref, m_ref, l_ref, acc_ref = rest[2 * G:]
    H = DIFF_HEADS
    p = pl.program_id(1)
    R = qx_ref.shape[1]
    qb = qx_ref[0].astype(BF16)

    @pl.when(p == 0)
    def _():
        m_ref[...] = jnp.full(m_ref.shape, -jnp.inf, F32)
        l_ref[...] = jnp.zeros(l_ref.shape, F32)
        acc_ref[...] = jnp.zeros(acc_ref.shape, F32)

    def head_mask(width):
        rowi = lax.broadcasted_iota(jnp.int32, (R, width), 0)
        coli = lax.broadcasted_iota(jnp.int32, (R, width), 1)
        return rowi, coli, (coli & (H - 1)) == (rowi >> 3)

    def update(s_list, v_list):
        m_prev = m_ref[...]
        m_new = m_prev
        for s in s_list:
            m_new = jnp.maximum(m_new, jnp.max(s, axis=-1, keepdims=True))
        alpha = jnp.exp(m_prev - m_new)
        l_new = alpha * l_ref[...]
        acc = alpha * acc_ref[...]
        for s, v in zip(s_list, v_list):
            pr = jnp.exp(s - m_new)
            l_new = l_new + jnp.sum(pr, axis=-1, keepdims=True)
            acc = acc + _dot(pr, v)
        m_ref[...] = m_new
        l_ref[...] = l_new
        acc_ref[...] = acc

    def page_scores():
        _, _, hm = head_mask(ck_refs[0].shape[1])
        return [jnp.where(hm, _dot_nt(qb, r[0]), -jnp.inf) for r in ck_refs], [r[0] for r in cv_refs]

    if n_groups > 1:
        @pl.when(p < n_groups - 1)
        def _():
            update(*page_scores())

    @pl.when(p == n_groups - 1)
    def _():
        s_list, v_list = page_scores()
        rowi, coli, hm = head_mask(kn_ref.shape[1])
        causal = (coli >> 3) <= (rowi & (n_new - 1))
        s_new = jnp.where(jnp.logical_and(hm, causal), _dot_nt(qb, kn_ref[0]), -jnp.inf)
        update(s_list + [s_new], v_list + [vn_ref[0]])
        lam = _lambda(lam_ref, lam_init)
        o8 = acc_ref[...] / l_ref[...]
        for hd in range(H):
            blk = o8[hd * SUBLANES:(hd + 1) * SUBLANES]
            diff = blk - lam * pltpu.roll(blk, SUBLANES // 2, axis=0)
            o_ref[0, hd * SUBLANES:(hd + 1) * SUBLANES, :] = _rms(diff, subln_ref[...]) * (1.0 - lam_init)


def _attn_sample(qx, cache_k, cache_v, page_table, k_new, v_new, lam_p, subln, lam_init, n_new):
    B, R, W = qx.shape
    rows = cache_k.shape[1]
    n_pages = page_table.shape[1]
    G = math.gcd(n_pages, PAGES_PER_STEP)
    n_groups = n_pages // G
    kern = functools.partial(_attn_sample_kernel, G=G, n_groups=n_groups, n_new=n_new, lam_init=lam_init)

    def page_spec(j):
        return pl.BlockSpec((1, rows, W), lambda b, p, pt: (pt[b, p * G + j], 0, 0))

    def seq(r):
        return pl.BlockSpec((1, r, W), lambda b, p, pt: (b, 0, 0))

    grid_spec = pltpu.PrefetchScalarGridSpec(
        num_scalar_prefetch=1,
        grid=(B, n_groups),
        in_specs=([pl.BlockSpec(lam_p.shape, lambda b, p, pt: (0, 0)), seq(R)]
                  + [page_spec(j) for j in range(G)] * 2
                  + [seq(k_new.shape[1]), seq(v_new.shape[1]),
                     pl.BlockSpec((1, W), lambda b, p, pt: (0, 0))]),
        out_specs=seq(R),
        scratch_shapes=[pltpu.VMEM((R, 1), F32),
                        pltpu.VMEM((R, 1), F32),
                        pltpu.VMEM((R, W), F32)],
    )
    return pl.pallas_call(
        kern,
        grid_spec=grid_spec,
        out_shape=jax.ShapeDtypeStruct((B, R, W), F32),
        compiler_params=_params("parallel", "arbitrary"),
        name="diff_attn_sample",
    )(page_table, lam_p, qx, *([cache_k] * G), *([cache_v] * G), k_new, v_new, subln.reshape(1, W))


def _lambda_init(layer):
    return 0.8 - 0.6 * math.exp(-0.3 * layer)


def _pad_rows(x, rows):
    return jnp.pad(x, [(0, 0), (0, rows - x.shape[1])] + [(0, 0)] * (x.ndim - 2))


def kernel(x_prompt, x_sample, state_delta, state_conv, cache_k, cache_v, page_table, norm_a, w_in_a, conv_w_a, a_log, dt_bias, onorm_a, w_out_a, kv_norm, w_kv, norm_b, w_q_b, lam_b, subln_b, w_out_b, norm_m, w_rg, b_rg, w_re, b_re, w_gate_e, w_up_e, w_down_e, final_norm):
    D = x_prompt.shape[-1]
    H = GDN_HEADS
    CH = 3 * H * GDN_DK
    NH, W = DIFF_HEADS, 2 * DIFF_DH
    n_in = w_in_a.shape[-1]
    n_in_pad = -(-n_in // LANES) * LANES
    depth = norm_m.shape[0]
    n_a = norm_a.shape[0]
    assert depth == 2 and n_a == 1 and norm_b.shape[0] == 1

    w_in_bf = jnp.pad(w_in_a[0], ((0, 0), (0, n_in_pad - n_in))).astype(BF16)
    w_out_a_bf = w_out_a[0].astype(BF16)
    w_q_bf = w_q_b[0].astype(BF16)
    w_kv_bf = w_kv.astype(BF16)
    w_out_b_bf = w_out_b[0].astype(BF16)
    n_route = N_GROUPS + N_EXPERTS
    wr = jnp.pad(jnp.concatenate([w_rg, w_re], axis=-1), ((0, 0), (0, 0), (0, LANES - n_route)))
    br = jnp.pad(jnp.concatenate([b_rg, b_re], axis=-1), ((0, 0), (0, LANES - n_route))).reshape(depth, 1, LANES)
    alog_l = jnp.zeros((1, LANES), F32).at[0, H:2 * H].set(a_log[0])
    dtb_l = jnp.zeros((1, LANES), F32).at[0, H:2 * H].set(dt_bias[0])

    def gdn_layer(h, B, L, conv0, s0):
        T = B * L
        tm = min(TOKEN_TILE, T)
        proj = _norm_matmul(h, norm_a[0], w_in_bf, tm)
        C = min(GDN_CHUNK, -(-L // SUBLANES) * SUBLANES)
        Lp = -(-L // C) * C
        valid_last = L - (Lp - C)
        proj3 = _pad_rows(proj.reshape(B, L, n_in_pad), Lp)
        conv0p = jnp.pad(conv0, ((0, 0), (SUBLANES - (CONV_W - 1), 0), (0, 0)))
        bb = next(n for n in ((4, 2, 1) if C < GDN_CHUNK else (2, 1)) if B % n == 0)
        o3, s_new, cbuf = _gdn_mixer(proj3, conv0p, s0, conv_w_a[0], alog_l, dtb_l, onorm_a[0], C, valid_last, bb)
        return o3[:, :L].reshape(T, D), s_new[None], cbuf[:, SUBLANES - (CONV_W - 1):][None]

    def attn_layer(h, B, L, past_len, paged):
        T = B * L
        tm = min(TOKEN_TILE, T)
        pos = past_len + jnp.arange(L, dtype=jnp.int32)
        cos_t, sin_t = _rope_tables(pos)
        if L % tm != 0:
            cos_t = jnp.tile(cos_t, (T // L, 1))
            sin_t = jnp.tile(sin_t, (T // L, 1))
        q, k, v = _qkv_proj(h, norm_b[0], kv_norm, w_q_bf, w_kv_bf, cos_t, sin_t, tm)
        lam_init = _lambda_init(n_a)
        if paged is None:
            o = _attn_prompt(q.reshape(B, L, D), k.reshape(B, L, D), v.reshape(B, L, D),
                             lam_b[0], subln_b[0], lam_init, min(ATTN_TILE, L))
            o = o.reshape(T, D)
        else:
            ck, cv, pt = paged
            assert 2 * L == SUBLANES and NH == SUBLANES
            q5 = q.reshape(B, L, NH, 2, DIFF_DH).transpose(0, 2, 1, 3, 4)
            sel = jnp.eye(2, dtype=F32)[None, None, :, None, :, None]
            qx = (q5[:, :, None] * sel).reshape(B, NH * 2 * L, W)
            o = _attn_sample(qx, ck, cv, pt, k.reshape(B, L * NH, W), v.reshape(B, L * NH, W),
                             lam_b[0], subln_b[0], lam_init, L)
            o = o.reshape(B, NH, 2 * L, W)[:, :, :L].transpose(0, 2, 1, 3).reshape(T, D)
        kshape = (B, L, NH, W)
        return o, k.reshape(kshape), v.reshape(kshape)

    def moe(op, os, hp, hs, wo_bf, layer, last):
        return _hier_moe(op, os, hp, hs, wo_bf, norm_m[layer], wr[layer], br[layer], w_gate_e, w_up_e, w_down_e,
                         layer, final_norm, last)

    Bp, Lq, _ = x_prompt.shape
    Bs, Ls, _ = x_sample.shape
    hp = x_prompt.reshape(Bp * Lq, D)
    hs = x_sample.reshape(Bs * Ls, D)

    op, sd_p, sc_p = gdn_layer(hp, Bp, Lq, jnp.zeros((Bp, CONV_W - 1, CH), F32),
                               jnp.zeros((Bp, H, GDN_DK, GDN_DK), F32))
    os, sd_s, sc_s = gdn_layer(hs, Bs, Ls, state_conv[0], state_delta[0])
    hp, hs = moe(op, os, hp, hs, w_out_a_bf, 0, False)

    n_phys, page, nh, w = cache_k.shape
    past_len = page_table.shape[1] * page
    ck = cache_k.reshape(n_phys, page * nh, w)
    cv = cache_v.reshape(n_phys, page * nh, w)
    op, k_p, v_p = attn_layer(hp, Bp, Lq, 0, None)
    os, k_s, v_s = attn_layer(hs, Bs, Ls, past_len, (ck, cv, page_table))
    y_p, y_s = moe(op, os, hp, hs, w_out_b_bf, 1, True)
    return (y_p.reshape(Bp, Lq, D), y_s.reshape(Bs, Ls, D), sd_p, sc_p, k_p, v_p, sd_s, sc_s, k_s, v_s)
```
